```python
import math
import jax, jax.numpy as jnp
from jax import lax
import numpy as np

D_MODEL = 1024
BATCH = 8
SEQ = 4096
DEPTH = 2

GRID_W = 64
CTX_LEN = 256
EPS = 1e-6

LRU_WIDTH = 512
LRU_BLOCKS = 8
LRU_BLOCK = LRU_WIDTH // LRU_BLOCKS
LRU_CONV = 4
LRU_C = 8.0

DA_HEADS = 4
DA_HEAD_DIM = 64
DA_V_DIM = 2 * DA_HEAD_DIM
DA_WIDTH = DA_HEADS * DA_V_DIM
ROPE_BASE = 10000.0
Q_BLOCK = 128

E_IN = 2 * LRU_WIDTH + 4 * DA_WIDTH
E_MIX = LRU_WIDTH + DA_WIDTH

SSD_INNER = 2 * D_MODEL
SSD_HEAD_DIM = 64
SSD_HEADS = SSD_INNER // SSD_HEAD_DIM
SSD_STATE = 128
SSD_GROUPS = 4
SSD_REP = SSD_HEADS // SSD_GROUPS
SSD_CONV = 4
SSD_CHUNK = 128
SSD_CONV_DIM = SSD_INNER + 2 * SSD_GROUPS * SSD_STATE
O_IN = SSD_INNER + SSD_CONV_DIM + 2 * SSD_HEADS

N_EVEN = (DEPTH + 1) // 2
N_ODD = DEPTH // 2

kernel_name = "hybrid_rglru_diffattn_ssd_prefix_dit"


def rms_norm(x, g):
    xf = x.astype(jnp.float32)
    y = xf * lax.rsqrt(jnp.mean(xf * xf, axis=-1, keepdims=True) + EPS)
    return (y * g.astype(jnp.float32)).astype(x.dtype)


def modulation(cvec, w, b):
    m = jax.nn.silu(cvec) @ w + b
    return jnp.split(m, 3, axis=-1)


def dwconv_centred(x, w, b):
    k = w.shape[0]
    y = lax.conv_general_dilated(
        x, w[:, None, :].astype(x.dtype), window_strides=(1,),
        padding=[(k // 2, k - 1 - k // 2)],
        dimension_numbers=("NWC", "WIO", "NWC"),
        feature_group_count=x.shape[-1])
    return y + b


def axial_rope_tables(n_tokens):
    rows = n_tokens // GRID_W
    row = jnp.repeat(jnp.arange(rows, dtype=jnp.float32), GRID_W)
    col = jnp.tile(jnp.arange(GRID_W, dtype=jnp.float32), rows)
    n_freq = DA_HEAD_DIM // 4
    inv = ROPE_BASE ** (-jnp.arange(n_freq, dtype=jnp.float32) / n_freq)
    ang = jnp.concatenate([row[:, None] * inv, col[:, None] * inv], axis=-1)
    return jnp.cos(ang), jnp.sin(ang)


def apply_axial_rope(x, cos, sin):
    n = cos.shape[-1] // 2

    def rot(u, cs, sn):
        u1, u2 = jnp.split(u, 2, axis=-1)
        cs = cs[None, :, None, None, :]
        sn = sn[None, :, None, None, :]
        return jnp.concatenate([u1 * cs - u2 * sn, u1 * sn + u2 * cs], axis=-1)

    xr, xc = jnp.split(x, 2, axis=-1)
    out = jnp.concatenate([rot(xr, cos[:, :n], sin[:, :n]), rot(xc, cos[:, n:], sin[:, n:])], axis=-1)
    return out.astype(x.dtype)


def diff_attention(q, k, v, lam):
    s = jnp.einsum("bqhcd,bkhcd->bhcqk", q, k).astype(jnp.float32) * (DA_HEAD_DIM ** -0.5)
    p = jax.nn.softmax(s, axis=-1)
    w = p[:, :, 0] - lam * p[:, :, 1]
    return jnp.einsum("bhqk,bkhe->bqhe", w.astype(v.dtype), v)


def _combine(e1, e2):
    a1, b1 = e1
    a2, b2 = e2
    return a1 * a2, a2 * b1 + b2


def linear_scan(a, b, h0, reverse):
    if reverse:
        a, b = jnp.flip(a, 1), jnp.flip(b, 1)
    b = b.at[:, 0].add(a[:, 0] * h0)
    _, h = lax.associative_scan(_combine, (a, b), axis=1)
    h_final = h[:, -1]
    if reverse:
        h = jnp.flip(h, 1)
    return h, h_final


def rglru_coeffs(xc, w_r, b_r, w_i, b_i, lam):
    bz, t, _ = xc.shape
    xb = xc.reshape(bz, t, LRU_BLOCKS, LRU_BLOCK)
    r = jax.nn.sigmoid(jnp.einsum("btnc,ncd->btnd", xb, w_r).reshape(bz, t, LRU_WIDTH) + b_r)
    i = jax.nn.sigmoid(jnp.einsum("btnc,ncd->btnd", xb, w_i).reshape(bz, t, LRU_WIDTH) + b_i)
    log_a = -LRU_C * r * jax.nn.softplus(-lam)
    a = jnp.exp(log_a)
    return a, jnp.sqrt(-jnp.expm1(2.0 * log_a)) * (i * xc)


def even_mixer(u_ctx, u_lat, cos, sin, lambda_init, ctx_out, w_in, w_out, conv_w, conv_b,
               w_r, b_r, w_i, b_i, lru_lambda, da_lambda, da_subln):
    bz, s, _ = u_lat.shape
    tc = u_ctx.shape[1]
    xr_c, gr_c, q_c, k_c, v_c, gd_c = jnp.split(u_ctx @ w_in, 6, axis=-1)
    xr_l, gr_l, q_l, k_l, v_l, gd_l = jnp.split(u_lat @ w_in, 6, axis=-1)

    xc_c = dwconv_centred(xr_c, conv_w, conv_b)
    xc_l = dwconv_centred(xr_l, conv_w, conv_b)
    hs_c, hs_l = [], []
    for d, rev in enumerate((False, True)):
        a, b = rglru_coeffs(xc_c, w_r[d], b_r[d], w_i[d], b_i[d], lru_lambda[d])
        h_c, h_fin = linear_scan(a, b, jnp.zeros((bz, LRU_WIDTH), b.dtype), rev)
        a, b = rglru_coeffs(xc_l, w_r[d], b_r[d], w_i[d], b_i[d], lru_lambda[d])
        h_l, _ = linear_scan(a, b, h_fin, rev)
        hs_c.append(h_c)
        hs_l.append(h_l)
    r_l = hs_l[0] + hs_l[1]

    lq1, lk1, lq2, lk2 = da_lambda.astype(jnp.float32)
    lam = jnp.exp(jnp.sum(lq1 * lk1)) - jnp.exp(jnp.sum(lq2 * lk2)) + lambda_init
    q_c = q_c.reshape(bz, tc, DA_HEADS, 2, DA_HEAD_DIM)
    k_c = k_c.reshape(bz, tc, DA_HEADS, 2, DA_HEAD_DIM)
    v_c = v_c.reshape(bz, tc, DA_HEADS, DA_V_DIM)
    q_l = apply_axial_rope(q_l.reshape(bz, s, DA_HEADS, 2, DA_HEAD_DIM), cos, sin)
    k_l = apply_axial_rope(k_l.reshape(bz, s, DA_HEADS, 2, DA_HEAD_DIM), cos, sin)
    v_l = v_l.reshape(bz, s, DA_HEADS, DA_V_DIM)
    k_all = jnp.concatenate([k_c, k_l], axis=1)
    v_all = jnp.concatenate([v_c, v_l], axis=1)
    n_blk = s // Q_BLOCK
    q_blocks = jnp.moveaxis(q_l.reshape(bz, n_blk, Q_BLOCK, DA_HEADS, 2, DA_HEAD_DIM), 1, 0)
    o_l = lax.map(lambda qb: diff_attention(qb, k_all, v_all, lam), q_blocks)
    o_l = jnp.moveaxis(o_l, 0, 1).reshape(bz, s, DA_HEADS, DA_V_DIM)

    def head_norm(o):
        return (rms_norm(o, da_subln) * (1.0 - lambda_init)).reshape(o.shape[0], o.shape[1], DA_WIDTH)

    y_l = jnp.concatenate([r_l * jax.nn.silu(gr_l), head_norm(o_l) * jax.nn.silu(gd_l)], axis=-1) @ w_out
    y_c = None
    if ctx_out:
        o_c = diff_attention(q_c, k_c, v_c, lam)
        r_c = hs_c[0] + hs_c[1]
        y_c = jnp.concatenate([r_c * jax.nn.silu(gr_c), head_norm(o_c) * jax.nn.silu(gd_c)], axis=-1) @ w_out
    return y_c, y_l


def ssd_chunked(X, Adt, Bm, Cm, h0):
    bz, t = X.shape[:2]
    nc, L = t // SSD_CHUNK, SSD_CHUNK
    Xc = X.reshape(bz, nc, L, SSD_GROUPS, SSD_REP, SSD_HEAD_DIM)
    Bc = Bm.reshape(bz, nc, L, SSD_GROUPS, SSD_STATE)
    Cc = Cm.reshape(bz, nc, L, SSD_GROUPS, SSD_STATE)
    A = jnp.transpose(Adt.astype(jnp.float32).reshape(bz, nc, L, SSD_GROUPS, SSD_REP), (0, 3, 4, 1, 2))
    A_cs = jnp.cumsum(A, axis=-1)
    seg = A_cs[..., :, None] - A_cs[..., None, :]
    tri = jnp.tril(jnp.ones((L, L), dtype=bool))
    Lmat = jnp.exp(jnp.where(tri, seg, -jnp.inf))
    CB = jnp.einsum("bclgn,bcsgn->bcgls", Cc, Bc)
    y_diag = jnp.einsum("bcgls,bgrcls,bcsgrp->bclgrp", CB, Lmat, Xc)
    decay_states = jnp.exp(A_cs[..., -1:] - A_cs)
    states = jnp.einsum("bclgn,bgrcl,bclgrp->bcgrpn", Bc, decay_states, Xc)
    states = jnp.concatenate([h0[:, None].astype(states.dtype), states], axis=1)
    chunk_tot = jnp.pad(A_cs[..., -1], ((0, 0), (0, 0), (0, 0), (1, 0)))
    cs = jnp.cumsum(chunk_tot, axis=-1)
    tri_c = jnp.tril(jnp.ones((nc + 1, nc + 1), dtype=bool))
    decay_chunk = jnp.exp(jnp.where(tri_c, cs[..., :, None] - cs[..., None, :], -jnp.inf))
    new_states = jnp.einsum("bgrzc,bcgrpn->bzgrpn", decay_chunk, states)
    states_in, final_state = new_states[:, :-1], new_states[:, -1]
    y_off = jnp.einsum("bclgn,bcgrpn,bgrcl->bclgrp", Cc, states_in, jnp.exp(A_cs))
    y = (y_diag + y_off).reshape(bz, t, SSD_GROUPS, SSD_REP, SSD_HEAD_DIM).astype(X.dtype)
    return y, final_state


def ssd_direction(xs, Bm, Cm, dt_d, A_d, h0, reverse):
    bz, t = xs.shape[:2]
    dt_g = dt_d.reshape(bz, t, SSD_GROUPS, SSD_REP)
    X = xs * dt_g[..., None]
    Adt = dt_g * A_d.reshape(SSD_GROUPS, SSD_REP)
    if reverse:
        X, Adt, Bm, Cm = jnp.flip(X, 1), jnp.flip(Adt, 1), jnp.flip(Bm, 1), jnp.flip(Cm, 1)
    y, h = ssd_chunked(X, Adt, Bm, Cm, h0)
    if reverse:
        y = jnp.flip(y, 1)
    return y, h


def odd_mixer(u_ctx, u_lat, ctx_out, w_in, w_out, conv_w, conv_b, a_log, dt_bias, d_skip, norm_w):
    def project(u):
        bz, t = u.shape[:2]
        z, xbc, dt = jnp.split(u @ w_in, [SSD_INNER, SSD_INNER + SSD_CONV_DIM], axis=-1)
        xbc = jax.nn.silu(dwconv_centred(xbc, conv_w, conv_b))
        xs, Bm, Cm = jnp.split(xbc, [SSD_INNER, SSD_INNER + SSD_GROUPS * SSD_STATE], axis=-1)
        xs = xs.reshape(bz, t, SSD_GROUPS, SSD_REP, SSD_HEAD_DIM)
        Bm = Bm.reshape(bz, t, SSD_GROUPS, SSD_STATE)
        Cm = Cm.reshape(bz, t, SSD_GROUPS, SSD_STATE)
        dt = jax.nn.softplus(dt.reshape(bz, t, 2, SSD_HEADS) + dt_bias)
        return z, xs, Bm, Cm, dt

    def finish(z, xs, ys):
        bz, t = xs.shape[:2]
        y = ys[0] + ys[1] + d_skip.reshape(SSD_GROUPS, SSD_REP)[:, :, None] * xs
        y = y.reshape(bz, t, SSD_INNER) * jax.nn.silu(z)
        y = rms_norm(y.reshape(bz, t, SSD_GROUPS, SSD_INNER // SSD_GROUPS),
                     norm_w.reshape(SSD_GROUPS, SSD_INNER // SSD_GROUPS)).reshape(bz, t, SSD_INNER)
        return y @ w_out

    A = -jnp.exp(a_log)
    z_c, xs_c, B_c, C_c, dt_c = project(u_ctx)
    z_l, xs_l, B_l, C_l, dt_l = project(u_lat)
    bz = u_lat.shape[0]
    ys_c, ys_l = [], []
    for d, rev in enumerate((False, True)):
        h0 = jnp.zeros((bz, SSD_GROUPS, SSD_REP, SSD_HEAD_DIM, SSD_STATE), jnp.float32)
        y_c, h_fin = ssd_direction(xs_c, B_c, C_c, dt_c[:, :, d], A[d], h0, rev)
        y_l, _ = ssd_direction(xs_l, B_l, C_l, dt_l[:, :, d], A[d], h_fin, rev)
        ys_c.append(y_c)
        ys_l.append(y_l)
    y_lat = finish(z_l, xs_l, ys_l)
    y_ctx = finish(z_c, xs_c, ys_c) if ctx_out else None
    return y_ctx, y_lat


def setup_inputs(seed: int = 0) -> dict:
    key = jax.random.key(seed)
    ks = jax.random.split(key, 27)
    f32 = jnp.float32
    nrm = lambda k, shape, s: jax.random.normal(k, shape, f32) * s
    u_lru = jax.random.uniform(ks[16], (N_EVEN, 2, LRU_WIDTH), f32, 0.9, 0.999)
    a_lru = u_lru ** (1.0 / LRU_C)
    dt0 = jnp.exp(jax.random.uniform(ks[24], (N_ODD, 2, SSD_HEADS), f32, math.log(1e-3), math.log(1e-1)))
    return {
        "x": nrm(ks[0], (BATCH, SEQ, D_MODEL), 1.0),
        "c": nrm(ks[1], (BATCH, D_MODEL), 1.0),
        "ctx": nrm(ks[2], (BATCH, CTX_LEN, D_MODEL), 1.0),
        "c_ctx": nrm(ks[3], (D_MODEL,), 1.0),
        "w_mod": nrm(ks[4], (DEPTH, D_MODEL, 3 * D_MODEL), 0.5 * D_MODEL ** -0.5),
        "b_mod": nrm(ks[5], (DEPTH, 3 * D_MODEL), 0.02),
        "g_pre": 1.0 + nrm(ks[6], (DEPTH, D_MODEL), 0.05),
        "g_post": 1.0 + nrm(ks[7], (DEPTH, D_MODEL), 0.05),
        "e_w_in": nrm(ks[8], (N_EVEN, D_MODEL, E_IN), D_MODEL ** -0.5),
        "e_w_out": nrm(ks[9], (N_EVEN, E_MIX, D_MODEL), E_MIX ** -0.5),
        "lru_conv_w": nrm(ks[10], (N_EVEN, LRU_CONV, LRU_WIDTH), LRU_CONV ** -0.5),
        "lru_conv_b": nrm(ks[11], (N_EVEN, LRU_WIDTH), 0.02),
        "lru_w_r": nrm(ks[12], (N_EVEN, 2, LRU_BLOCKS, LRU_BLOCK, LRU_BLOCK), LRU_BLOCK ** -0.5),
        "lru_b_r": nrm(ks[13], (N_EVEN, 2, LRU_WIDTH), 0.02),
        "lru_w_i": nrm(ks[14], (N_EVEN, 2, LRU_BLOCKS, LRU_BLOCK, LRU_BLOCK), LRU_BLOCK ** -0.5),
        "lru_b_i": nrm(ks[15], (N_EVEN, 2, LRU_WIDTH), 0.02),
        "lru_lambda": jnp.log(a_lru) - jnp.log1p(-a_lru),
        "da_lambda": nrm(ks[17], (N_EVEN, 4, DA_HEAD_DIM), 0.1),
        "da_subln": 1.0 + nrm(ks[18], (N_EVEN, DA_V_DIM), 0.05),
        "o_w_in": nrm(ks[19], (N_ODD, D_MODEL, O_IN), D_MODEL ** -0.5),
        "o_w_out": nrm(ks[20], (N_ODD, SSD_INNER, D_MODEL), SSD_INNER ** -0.5),
        "ssd_conv_w": nrm(ks[21], (N_ODD, SSD_CONV, SSD_CONV_DIM), SSD_CONV ** -0.5),
        "ssd_conv_b": nrm(ks[22], (N_ODD, SSD_CONV_DIM), 0.02),
        "ssd_a_log": jnp.log(jax.random.uniform(ks[23], (N_ODD, 2, SSD_HEADS), f32, 1.0, 16.0)),
        "ssd_dt_bias": dt0 + jnp.log(-jnp.expm1(-dt0)),
        "ssd_d": 1.0 + nrm(ks[25], (N_ODD, SSD_HEADS), 0.05),
        "ssd_norm": 1.0 + nrm(ks[26], (N_ODD, SSD_INNER), 0.05),
    }


def reference(x, c, ctx, c_ctx, w_mod, b_mod, g_pre, g_post, e_w_in, e_w_out, lru_conv_w, lru_conv_b,
              lru_w_r, lru_b_r, lru_w_i, lru_b_i, lru_lambda, da_lambda, da_subln, o_w_in, o_w_out,
              ssd_conv_w, ssd_conv_b, ssd_a_log, ssd_dt_bias, ssd_d, ssd_norm):
    cos, sin = axial_rope_tables(x.shape[1])
    h_lat, h_ctx = x, ctx
    for i in range(DEPTH):
        last = i == DEPTH - 1
        sh, sc, gt = modulation(c, w_mod[i], b_mod[i])
        sh_c, sc_c, gt_c = modulation(c_ctx, w_mod[i], b_mod[i])
        u_lat = rms_norm(h_lat, g_pre[i]) * (1.0 + sc[:, None]) + sh[:, None]
        u_ctx = rms_norm(h_ctx, g_pre[i]) * (1.0 + sc_c) + sh_c
        j = i // 2
        if i % 2 == 0:
            lambda_init = 0.8 - 0.6 * math.exp(-0.3 * i)
            y_ctx, y_lat = even_mixer(u_ctx, u_lat, cos, sin, lambda_init, not last,
                                      e_w_in[j], e_w_out[j], lru_conv_w[j], lru_conv_b[j],
                                      lru_w_r[j], lru_b_r[j], lru_w_i[j], lru_b_i[j], lru_lambda[j],
                                      da_lambda[j], da_subln[j])
        else:
            y_ctx, y_lat = odd_mixer(u_ctx, u_lat, not last, o_w_in[j], o_w_out[j], ssd_conv_w[j],
                                     ssd_conv_b[j], ssd_a_log[j], ssd_dt_bias[j], ssd_d[j], ssd_norm[j])
        h_lat = h_lat + gt[:, None] * rms_norm(y_lat, g_post[i])
        if not last:
            h_ctx = h_ctx + gt_c * rms_norm(y_ctx, g_post[i])
    return h_lat
```

```python
import functools
import math

import jax
import jax.numpy as jnp
from jax import lax
from jax.experimental import pallas as pl
from jax.experimental.pallas import tpu as pltpu

F32 = jnp.float32
BF16 = jnp.bfloat16

EPS = 1e-6
GRID_W = 64
ROPE_BASE = 10000.0
LRU_C = 8.0
LRU_BLOCKS = 8
DA_HEADS = 4
DA_HEAD_DIM = 64
DA_V_DIM = 128
SSD_HEAD_DIM = 64
SSD_STATE = 128
SSD_GROUPS = 4
SSD_CHUNK = 128

TM = 256
HALO = 8
LRU_TT = 128
KV_CHUNK = 512
LANES = 128
SUBLANES = 8
VMEM_LIMIT = 56 * 1024 * 1024

HIGHEST = lax.Precision.HIGHEST


def _silu(x):
    return x * jax.nn.sigmoid(x)


def _rms(x):
    return x * lax.rsqrt(jnp.mean(x * x, axis=-1, keepdims=True) + EPS)


def _cparams(sem):
    return pltpu.CompilerParams(dimension_semantics=sem, vmem_limit_bytes=VMEM_LIMIT)


def _mod_kernel(c_ref, w_ref, b_ref, o_ref):
    s = _silu(c_ref[...])
    o_ref[0] = jnp.dot(s, w_ref[0], preferred_element_type=F32, precision=HIGHEST) + b_ref[0]


def _modulation(c_rows, w_mod, b_mod):
    depth, d, n3 = w_mod.shape
    rows = c_rows.shape[0]
    nt = 1024
    return pl.pallas_call(
        _mod_kernel,
        grid=(depth, n3 // nt),
        in_specs=[
            pl.BlockSpec((rows, d), lambda l, n: (0, 0)),
            pl.BlockSpec((1, d, nt), lambda l, n: (l, 0, n)),
            pl.BlockSpec((1, 1, nt), lambda l, n: (l, 0, n)),
        ],
        out_specs=pl.BlockSpec((1, rows, nt), lambda l, n: (l, 0, n)),
        out_shape=jax.ShapeDtypeStruct((depth, rows, n3), F32),
        compiler_params=_cparams(("arbitrary", "arbitrary")),
        name="modulation",
    )(c_rows, w_mod, b_mod.reshape(depth, 1, n3))


def _normed_rows(h_ext, mod_ref, g_ref, d):
    sh = mod_ref[0, :, 0:d]
    sc = mod_ref[0, :, d:2 * d]
    u = (_rms(h_ext) * g_ref[...]) * (1.0 + sc) + sh
    return u.astype(BF16)


def _halo_mask(x_ext, left_ok, right_ok):
    r = lax.broadcasted_iota(jnp.int32, x_ext.shape, 0)
    lo = jnp.where(left_ok, 0, HALO)
    hi = jnp.where(right_ok, TM + 2 * HALO, TM + HALO)
    return jnp.where((r < lo) | (r >= hi), 0.0, x_ext)


def _conv4(ext_ref, w_ref, b_ref):
    acc = b_ref[...] + w_ref[0:1, :] * ext_ref[pl.ds(HALO - 2, TM), :]
    acc = acc + w_ref[1:2, :] * ext_ref[pl.ds(HALO - 1, TM), :]
    acc = acc + w_ref[2:3, :] * ext_ref[pl.ds(HALO, TM), :]
    acc = acc + w_ref[3:4, :] * ext_ref[pl.ds(HALO + 1, TM), :]
    return acc


def _inproj0_kernel(n_tiles, xm_ref, xl_ref, xr_ref, ctx_ref, mod_ref, g_ref, w_ref, cw_ref, cb_ref,
                    rc_ref, rs1_ref, rs2_ref,
                    xc_ref, gr_ref, q_ref, k_ref, v_ref, gd_ref, ext_ref):
    j = pl.program_id(0)
    d = xm_ref.shape[-1]
    w5 = xc_ref.shape[-1]
    is_ctx = j == 0
    h_main = jnp.where(is_ctx, ctx_ref[0], xm_ref[0])
    h_ext = jnp.concatenate([xl_ref[0], h_main, xr_ref[0]], axis=0)
    u_ext = _normed_rows(h_ext, mod_ref, g_ref, d)
    u = u_ext[HALO:HALO + TM]

    x_ext = jnp.dot(u_ext, w_ref[:, 0:w5], preferred_element_type=F32)
    ext_ref[...] = _halo_mask(x_ext, j >= 2, (j >= 1) & (j <= n_tiles - 2))
    xc_ref[0] = _conv4(ext_ref, cw_ref, cb_ref)

    gr_ref[0] = jnp.dot(u, w_ref[:, w5:2 * w5], preferred_element_type=F32).astype(BF16)
    reps = w5 // LANES
    c = jnp.concatenate([rc_ref[...]] * reps, axis=1)
    s1 = jnp.concatenate([rs1_ref[...]] * reps, axis=1)
    s2 = jnp.concatenate([rs2_ref[...]] * reps, axis=1)
    for idx, o_ref in ((2, q_ref), (3, k_ref)):
        t = jnp.dot(u, w_ref[:, idx * w5:(idx + 1) * w5], preferred_element_type=F32)
        t = t * c + pltpu.roll(t, w5 - 16, 1) * s1 + pltpu.roll(t, 16, 1) * s2
        o_ref[0] = t.astype(BF16)
    v_ref[0] = jnp.dot(u, w_ref[:, 4 * w5:5 * w5], preferred_element_type=F32).astype(BF16)
    gd_ref[0] = jnp.dot(u, w_ref[:, 5 * w5:6 * w5], preferred_element_type=F32).astype(BF16)


def _inproj0(x, ctx, mod, g_pre, w_in, conv_w, conv_b, rope_c, rope_s1, rope_s2):
    bsz, s, d = x.shape
    n_lat = s // TM
    n_tiles = n_lat + 1
    t_all = n_tiles * TM
    w5 = conv_w.shape[-1]
    hb = TM // HALO

    def main_map(j, b):
        return (b, jnp.maximum(j - 1, 0), 0)

    def left_map(j, b):
        return (b, jnp.maximum((j - 1) * hb - 1, 0), 0)

    def right_map(j, b):
        return (b, jnp.minimum(jnp.maximum(j, 0) * hb, s // HALO - 1), 0)

    def mod_map(j, b):
        return (jnp.where(j == 0, bsz, b), 0, 0)

    const2 = lambda j, b: (0, 0)
    tile_out = pl.BlockSpec((1, TM, w5), lambda j, b: (b, j, 0))
    rope_spec = pl.BlockSpec((TM, LANES), lambda j, b: (j, 0))
    outs = pl.pallas_call(
        functools.partial(_inproj0_kernel, n_tiles),
        grid=(n_tiles, bsz),
        in_specs=[
            pl.BlockSpec((1, TM, d), main_map),
            pl.BlockSpec((1, HALO, d), left_map),
            pl.BlockSpec((1, HALO, d), right_map),
            pl.BlockSpec((1, TM, d), lambda j, b: (b, 0, 0)),
            pl.BlockSpec((1, 1, 3 * d), mod_map),
            pl.BlockSpec((1, d), const2),
            pl.BlockSpec(w_in.shape, const2),
            pl.BlockSpec(conv_w.shape, const2),
            pl.BlockSpec((1, w5), const2),
            rope_spec, rope_spec, rope_spec,
        ],
        out_specs=[tile_out] * 6,
        out_shape=[jax.ShapeDtypeStruct((bsz, t_all, w5), F32)]
        + [jax.ShapeDtypeStruct((bsz, t_all, w5), BF16)] * 5,
        scratch_shapes=[pltpu.VMEM((TM + 2 * HALO, w5), F32)],
        compiler_params=_cparams(("arbitrary", "arbitrary")),
        name="inproj0",
    )(x, x, x, ctx, mod, g_pre.reshape(1, d), w_in, conv_w, conv_b.reshape(1, w5), rope_c, rope_s1, rope_s2)
    return outs


def _lru_kernel(bsz, xf_ref, xb_ref, wg_ref, bias_ref, sp_ref, of_ref, ob_ref, a_scr, b_scr, hs_scr, h_scr):
    i = pl.program_id(0)
    w5 = xf_ref.shape[-1]
    nslab = w5 // LANES
    tt = LRU_TT

    @pl.when(i == 0)
    def _():
        h_scr[...] = jnp.zeros_like(h_scr)

    for dr, x_ref in enumerate((xf_ref, xb_ref)):
        sp = sp_ref[dr:dr + 1, :]
        for b in range(bsz):
            xc = x_ref[b]
            g = jnp.dot(xc.astype(BF16), wg_ref[dr], preferred_element_type=F32)
            r = jax.nn.sigmoid(g[:, 0:w5] + bias_ref[2 * dr:2 * dr + 1, :])
            ig = jax.nn.sigmoid(g[:, w5:2 * w5] + bias_ref[2 * dr + 1:2 * dr + 2, :])
            log_a = (-LRU_C) * r * sp
            a = jnp.exp(log_a)
            bc = jnp.sqrt(-jnp.tanh(log_a) * (a * a + 1.0)) * (ig * xc)
            for c in range(nslab):
                a_scr[dr, c, pl.ds(b, tt, stride=bsz), :] = a[:, c * LANES:(c + 1) * LANES]
                b_scr[dr, c, pl.ds(b, tt, stride=bsz), :] = bc[:, c * LANES:(c + 1) * LANES]

    def step(t, carry):
        hf, hb = carry
        rf = pl.multiple_of(t * bsz, bsz)
        rb = pl.multiple_of((tt - 1 - t) * bsz, bsz)
        nf, nb = [], []
        for c in range(nslab):
            h = a_scr[0, c, pl.ds(rf, bsz), :] * hf[c] + b_scr[0, c, pl.ds(rf, bsz), :]
            hs_scr[0, c, pl.ds(rf, bsz), :] = h
            nf.append(h)
            h = a_scr[1, c, pl.ds(rb, bsz), :] * hb[c] + b_scr[1, c, pl.ds(rb, bsz), :]
            hs_scr[1, c, pl.ds(rb, bsz), :] = h
            nb.append(h)
        return tuple(nf), tuple(nb)

    h0f = tuple(h_scr[0, c] for c in range(nslab))
    h0b = tuple(h_scr[1, c] for c in range(nslab))
    hf, hb = lax.fori_loop(0, tt, step, (h0f, h0b), unroll=8)
    for c in range(nslab):
        h_scr[0, c] = hf[c]
        h_scr[1, c] = hb[c]

    for dr, o_ref in enumerate((of_ref, ob_ref)):
        for b in range(bsz):
            for c in range(nslab):
                o_ref[b, :, c * LANES:(c + 1) * LANES] = hs_scr[dr, c, pl.ds(b, tt, stride=bsz), :].astype(BF16)


def _bwd_tile(i, n_ctx, n_all):
    return jnp.where(i < n_ctx, n_ctx - 1 - i, n_all - 1 - (i - n_ctx))


def _lru(xc, wg, bias, sp, ctx_len):
    bsz, t_all, w5 = xc.shape
    tt = LRU_TT
    n_all = t_all // tt
    n_ctx = ctx_len // tt
    nslab = w5 // LANES
    fwd_spec = pl.BlockSpec((bsz, tt, w5), lambda i: (0, i, 0))
    bwd_spec = pl.BlockSpec((bsz, tt, w5), lambda i: (0, _bwd_tile(i, n_ctx, n_all), 0))
    const = lambda shape: pl.BlockSpec(shape, lambda i: (0,) * len(shape))
    coef = pltpu.VMEM((2, nslab, tt * bsz, LANES), F32)
    return pl.pallas_call(
        functools.partial(_lru_kernel, bsz),
        grid=(n_all,),
        in_specs=[fwd_spec, bwd_spec, const(wg.shape), const(bias.shape), const(sp.shape)],
        out_specs=[fwd_spec, bwd_spec],
        out_shape=[jax.ShapeDtypeStruct((bsz, t_all, w5), BF16)] * 2,
        scratch_shapes=[coef, coef, coef, pltpu.VMEM((2, nslab, bsz, LANES), F32)],
        compiler_params=_cparams(("arbitrary",)),
        name="rglru",
    )(xc, xc, wg, bias, sp)


def _attn_kernel(lambda_init, n_lat_chunks, q_ref, k_ref, v_ref, lam_ref, sub_ref, o_ref, m_scr, l_scr, acc_scr):
    j = pl.program_id(2)
    q = q_ref[0]
    lane = lax.broadcasted_iota(jnp.int32, q.shape, 1)
    zero = jnp.zeros_like(q)
    qq = jnp.concatenate([jnp.where(lane < DA_HEAD_DIM, q, zero), jnp.where(lane >= DA_HEAD_DIM, q, zero)], axis=0)

    m_scr[...] = jnp.full_like(m_scr, -jnp.inf)
    l_scr[...] = jnp.zeros_like(l_scr)
    acc_scr[...] = jnp.zeros_like(acc_scr)

    def chunk(start, size):
        kc = k_ref[0, pl.ds(start, size), :]
        vc = v_ref[0, pl.ds(start, size), :]
        s = lax.dot_general(qq, kc, (((1,), (1,)), ((), ())), preferred_element_type=F32)
        m_prev = m_scr[...]
        m_next = jnp.maximum(m_prev, jnp.max(s, axis=1, keepdims=True))
        p = jnp.exp(s - pltpu.repeat(m_next, size // LANES, 1))
        alpha = jnp.exp(m_prev - m_next)
        l_scr[...] = alpha * l_scr[...] + jnp.sum(p, axis=1, keepdims=True)
        acc_scr[...] = acc_scr[...] * alpha + jnp.dot(p.astype(BF16), vc, preferred_element_type=F32)
        m_scr[...] = m_next

    chunk(0, TM)

    @pl.when(j > 0)
    def _():
        def body(c, carry):
            chunk(pl.multiple_of(TM + c * KV_CHUNK, TM), KV_CHUNK)
            return carry
        lax.fori_loop(0, n_lat_chunks, body, 0)

    lm = lam_ref[...]
    lam = (jnp.exp(jnp.sum(lm[0:1] * lm[1:2], axis=1, keepdims=True))
           - jnp.exp(jnp.sum(lm[2:3] * lm[3:4], axis=1, keepdims=True)) + lambda_init)
    o = acc_scr[0:TM] / l_scr[0:TM] - lam * (acc_scr[TM:2 * TM] / l_scr[TM:2 * TM])
    o_ref[0] = ((_rms(o) * sub_ref[...]) * (1.0 - lambda_init)).astype(BF16)


def _attention(q, k, v, da_lambda, da_subln, lambda_init):
    bsz, t_all, _ = q.shape
    n_tiles = t_all // TM
    n_lat_chunks = (t_all - TM) // KV_CHUNK
    hd = DA_V_DIM
    q_spec = pl.BlockSpec((1, TM, hd), lambda b, h, j: (b, j, h))
    kv_spec = pl.BlockSpec((1, t_all, hd), lambda b, h, j: (b, 0, h))
    stat = pltpu.VMEM((2 * TM, LANES), F32)
    return pl.pallas_call(
        functools.partial(_attn_kernel, lambda_init, n_lat_chunks),
        grid=(bsz, DA_HEADS, n_tiles),
        in_specs=[q_spec, kv_spec, kv_spec,
                  pl.BlockSpec(da_lambda.shape, lambda b, h, j: (0, 0)),
                  pl.BlockSpec((1, hd), lambda b, h, j: (0, 0))],
        out_specs=q_spec,
        out_shape=jax.ShapeDtypeStruct(q.shape, BF16),
        scratch_shapes=[stat, stat, stat],
        compiler_params=_cparams(("arbitrary", "arbitrary", "arbitrary")),
        name="diffattn",
    )(q, k, v, da_lambda, da_subln.reshape(1, hd))


def _outproj0_kernel(hf_ref, hb_ref, gr_ref, on_ref, gd_ref, xm_ref, ctx_ref, mod_ref, g_ref, w_ref, o_ref):
    j = pl.program_id(0)
    d = o_ref.shape[-1]
    w5 = hf_ref.shape[-1]
    r = hf_ref[0].astype(F32) + hb_ref[0].astype(F32)
    m1 = (r * _silu(gr_ref[0].astype(F32))).astype(BF16)
    m2 = (on_ref[0].astype(F32) * _silu(gd_ref[0].astype(F32))).astype(BF16)
    y = jnp.dot(m1, w_ref[0:w5, :], preferred_element_type=F32)
    y = y + jnp.dot(m2, w_ref[w5:2 * w5, :], preferred_element_type=F32)
    gt = mod_ref[0, :, 2 * d:3 * d]
    h = jnp.where(j == 0, ctx_ref[0], xm_ref[0])
    o_ref[0] = h + gt * (_rms(y) * g_ref[...])


def _outproj0(hf, hb, gr, on, gd, x, ctx, mod, g_post, w_out):
    bsz, t_all, w5 = hf.shape
    d = x.shape[-1]
    n_tiles = t_all // TM
    tile = pl.BlockSpec((1, TM, w5), lambda j, b: (b, j, 0))
    const2 = lambda j, b: (0, 0)
    return pl.pallas_call(
        _outproj0_kernel,
        grid=(n_tiles, bsz),
        in_specs=[tile] * 5 + [
            pl.BlockSpec((1, TM, d), lambda j, b: (b, jnp.maximum(j - 1, 0), 0)),
            pl.BlockSpec((1, TM, d), lambda j, b: (b, 0, 0)),
            pl.BlockSpec((1, 1, 3 * d), lambda j, b: (jnp.where(j == 0, bsz, b), 0, 0)),
            pl.BlockSpec((1, d), const2),
            pl.BlockSpec(w_out.shape, const2),
        ],
        out_specs=pl.BlockSpec((1, TM, d), lambda j, b: (b, j, 0)),
        out_shape=jax.ShapeDtypeStruct((bsz, t_all, d), F32),
        compiler_params=_cparams(("arbitrary", "arbitrary")),
        name="outproj0",
    )(hf, hb, gr, on, gd, x, ctx, mod, g_post.reshape(1, d), w_out)


def _inproj1_kernel(n_tiles, hm_ref, hl_ref, hr_ref, mod_ref, g_ref, wz_ref, wx_ref, wd_ref, cw_ref, cb_ref, db_ref,
                    z_ref, xbc_ref, dt_ref, ext_ref):
    j = pl.program_id(0)
    d = hm_ref.shape[-1]
    h_ext = jnp.concatenate([hl_ref[0], hm_ref[0], hr_ref[0]], axis=0)
    u_ext = _normed_rows(h_ext, mod_ref, g_ref, d)
    u = u_ext[HALO:HALO + TM]
    z_ref[0] = jnp.dot(u, wz_ref[...], preferred_element_type=F32).astype(BF16)
    x_ext = jnp.dot(u_ext, wx_ref[...], preferred_element_type=F32)
    ext_ref[...] = _halo_mask(x_ext, j >= 2, (j >= 1) & (j <= n_tiles - 2))
    xbc_ref[0] = _silu(_conv4(ext_ref, cw_ref, cb_ref)).astype(BF16)
    dt_ref[0] = jax.nn.softplus(jnp.dot(u, wd_ref[...], preferred_element_type=F32) + db_ref[...])


def _inproj1(h, mod, g_pre, wz, wx, wd, conv_w, conv_b, dt_bias, n_mod_rows):
    bsz, t_all, d = h.shape
    n_tiles = t_all // TM
    hb = TM // HALO
    nz, nx, nd = wz.shape[1], wx.shape[1], wd.shape[1]
    const2 = lambda j, b: (0, 0)
    return pl.pallas_call(
        functools.partial(_inproj1_kernel, n_tiles),
        grid=(n_tiles, bsz),
        in_specs=[
            pl.BlockSpec((1, TM, d), lambda j, b: (b, j, 0)),
            pl.BlockSpec((1, HALO, d), lambda j, b: (b, jnp.maximum(j * hb - 1, 0), 0)),
            pl.BlockSpec((1, HALO, d), lambda j, b: (b, jnp.minimum((j + 1) * hb, t_all // HALO - 1), 0)),
            pl.BlockSpec((1, 1, 3 * d), lambda j, b: (jnp.where(j == 0, n_mod_rows, b), 0, 0)),
            pl.BlockSpec((1, d), const2),
            pl.BlockSpec(wz.shape, const2),
            pl.BlockSpec(wx.shape, const2),
            pl.BlockSpec(wd.shape, const2),
            pl.BlockSpec(conv_w.shape, const2),
            pl.BlockSpec((1, nx), const2),
            pl.BlockSpec((1, nd), const2),
        ],
        out_specs=[pl.BlockSpec((1, TM, nz), lambda j, b: (b, j, 0)),
                   pl.BlockSpec((1, TM, nx), lambda j, b: (b, j, 0)),
                   pl.BlockSpec((1, TM, nd), lambda j, b: (b, j, 0))],
        out_shape=[jax.ShapeDtypeStruct((bsz, t_all, nz), BF16),
                   jax.ShapeDtypeStruct((bsz, t_all, nx), BF16),
                   jax.ShapeDtypeStruct((bsz, t_all, nd), F32)],
        scratch_shapes=[pltpu.VMEM((TM + 2 * HALO, nx), F32)],
        compiler_params=_cparams(("arbitrary", "arbitrary")),
        name="inproj1",
    )(h, h, h, mod, g_pre.reshape(1, d), wz, wx, wd, conv_w, conv_b.reshape(1, nx), dt_bias)


def _ssd_direction(reverse, dr, xbc_ref, dt_ref, alog_ref, s_scr, y_ref):
    ch = SSD_CHUNK
    inner = y_ref.shape[-1]
    n_heads = inner // SSD_HEAD_DIM
    gn = SSD_GROUPS * SSD_STATE
    pairs_per_group = n_heads // SSD_GROUPS // 2
    h0 = dr * n_heads

    dt = dt_ref[0]
    a_neg = -jnp.exp(alog_ref[dr:dr + 1, :])
    adt = dt * a_neg
    row = lax.broadcasted_iota(jnp.int32, (ch, ch), 0)
    col = lax.broadcasted_iota(jnp.int32, (ch, ch), 1)
    mask = (row <= col) if reverse else (row >= col)
    cs = jnp.dot(mask.astype(F32), adt, preferred_element_type=F32, precision=HIGHEST)
    last = 0 if reverse else ch - 1
    cs_t = cs.T
    dt_t = dt.T
    w_t = jnp.exp(cs_t[:, last:last + 1] - cs_t) * dt_t
    e_tot = jnp.exp(cs[last:last + 1, :])
    lane = lax.broadcasted_iota(jnp.int32, (ch, LANES), 1)
    left = lane < SSD_HEAD_DIM

    for g in range(SSD_GROUPS):
        bg = xbc_ref[0, :, inner + g * SSD_STATE:inner + (g + 1) * SSD_STATE]
        cg = xbc_ref[0, :, inner + gn + g * SSD_STATE:inner + gn + (g + 1) * SSD_STATE]
        cb = lax.dot_general(cg, bg, (((1,), (1,)), ((), ())), preferred_element_type=F32)
        bg_t = bg.astype(F32).T
        cg32 = cg.astype(F32)
        for pp in range(pairs_per_group):
            p = g * pairs_per_group + pp
            x2 = xbc_ref[0, :, p * LANES:(p + 1) * LANES]
            zero = jnp.zeros_like(x2)
            wx = jnp.concatenate([jnp.where(left, x2, zero), jnp.where(left, zero, x2)], axis=0)
            sp = s_scr[dr, p]
            sp16 = sp.astype(BF16)
            zero = jnp.zeros_like(sp16)
            ws = jnp.concatenate([jnp.where(left, sp16, zero), jnp.where(left, zero, sp16)], axis=0)
            ms, ces, bws = [], [], []
            for h in (h0 + 2 * p, h0 + 2 * p + 1):
                ccol = cs[:, h:h + 1]
                crow = cs_t[h:h + 1, :]
                lmat = jnp.exp(jnp.where(mask, ccol - crow, -jnp.inf))
                ms.append((cb * lmat * dt_t[h:h + 1, :]).astype(BF16))
                ces.append((cg32 * jnp.exp(ccol)).astype(BF16))
                bws.append((bg_t * w_t[h:h + 1, :]).astype(BF16))
            y = jnp.dot(jnp.concatenate(ms + ces, axis=1), jnp.concatenate([wx, ws], axis=0),
                        preferred_element_type=F32)
            y_ref[0, :, p * LANES:(p + 1) * LANES] = y.astype(BF16)
            ds = jnp.dot(jnp.concatenate(bws, axis=1), wx, preferred_element_type=F32)
            dec = jnp.where(left[0:1, :], e_tot[:, h0 + 2 * p:h0 + 2 * p + 1], e_tot[:, h0 + 2 * p + 1:h0 + 2 * p + 2])
            s_scr[dr, p] = sp * dec + ds


def _ssd_kernel(xf_ref, df_ref, xb_ref, db_ref, alog_ref, yf_ref, yb_ref, s_scr):
    @pl.when(pl.program_id(1) == 0)
    def _():
        s_scr[...] = jnp.zeros_like(s_scr)

    _ssd_direction(False, 0, xf_ref, df_ref, alog_ref, s_scr, yf_ref)
    _ssd_direction(True, 1, xb_ref, db_ref, alog_ref, s_scr, yb_ref)


def _ssd(xbc, dt, a_log, ctx_len):
    bsz, t_all, nx = xbc.shape
    nd = dt.shape[-1]
    n_heads = a_log.shape[-1]
    inner = n_heads * SSD_HEAD_DIM
    a_log = jnp.stack([jnp.pad(a_log[dr], (dr * n_heads, nd - (dr + 1) * n_heads)) for dr in range(2)])
    ch = SSD_CHUNK
    n_all = t_all // ch
    n_ctx = ctx_len // ch
    fwd = lambda b, i: (b, i, 0)
    bwd = lambda b, i: (b, _bwd_tile(i, n_ctx, n_all), 0)
    return pl.pallas_call(
        _ssd_kernel,
        grid=(bsz, n_all),
        in_specs=[pl.BlockSpec((1, ch, nx), fwd), pl.BlockSpec((1, ch, nd), fwd),
                  pl.BlockSpec((1, ch, nx), bwd), pl.BlockSpec((1, ch, nd), bwd),
                  pl.BlockSpec(a_log.shape, lambda b, i: (0, 0))],
        out_specs=[pl.BlockSpec((1, ch, inner), fwd), pl.BlockSpec((1, ch, inner), bwd)],
        out_shape=[jax.ShapeDtypeStruct((bsz, t_all, inner), BF16)] * 2,
        scratch_shapes=[pltpu.VMEM((2, n_heads // 2, SSD_STATE, LANES), F32)],
        compiler_params=_cparams(("arbitrary", "arbitrary")),
        name="ssd",
    )(xbc, dt, xbc, dt, a_log)


def _finish1_kernel(yf_ref, yb_ref, xs_ref, z_ref, h_ref, mod_ref, dsk_ref, nw_ref, w_ref, g_ref, o_ref):
    d = o_ref.shape[-1]
    inner = yf_ref.shape[-1]
    gw = inner // SSD_GROUPS
    y = yf_ref[0].astype(F32) + yb_ref[0].astype(F32) + dsk_ref[...] * xs_ref[0].astype(F32)
    y = y * _silu(z_ref[0].astype(F32))
    parts = []
    for g in range(SSD_GROUPS):
        yg = y[:, g * gw:(g + 1) * gw]
        parts.append((_rms(yg) * nw_ref[:, g * gw:(g + 1) * gw]).astype(BF16))
    out = jnp.dot(jnp.concatenate(parts, axis=1), w_ref[...], preferred_element_type=F32)
    gt = mod_ref[0, :, 2 * d:3 * d]
    o_ref[0] = h_ref[0] + gt * (_rms(out) * g_ref[...])


def _finish1(yf, yb, xbc, z, h, mod, d_skip, norm_w, w_out, g_post, ctx_len):
    bsz, t_all, inner = yf.shape
    d = h.shape[-1]
    off = ctx_len // TM
    n_lat = t_all // TM - off
    lat = lambda j, b: (b, j + off, 0)
    const2 = lambda j, b: (0, 0)
    return pl.pallas_call(
        _finish1_kernel,
        grid=(n_lat, bsz),
        in_specs=[pl.BlockSpec((1, TM, inner), lat)] * 4 + [
            pl.BlockSpec((1, TM, d), lat),
            pl.BlockSpec((1, 1, 3 * d), lambda j, b: (b, 0, 0)),
            pl.BlockSpec((1, inner), const2),
            pl.BlockSpec((1, inner), const2),
            pl.BlockSpec(w_out.shape, const2),
            pl.BlockSpec((1, d), const2),
        ],
        out_specs=pl.BlockSpec((1, TM, d), lambda j, b: (b, j, 0)),
        out_shape=jax.ShapeDtypeStruct((bsz, n_lat * TM, d), F32),
        compiler_params=_cparams(("arbitrary", "arbitrary")),
        name="finish1",
    )(yf, yb, xbc, z, h, mod, d_skip, norm_w, w_out, g_post.reshape(1, d))


def _rope_tables(n_tokens, ctx_len):
    rows = n_tokens // GRID_W
    row = jnp.repeat(jnp.arange(rows, dtype=F32), GRID_W)
    col = jnp.tile(jnp.arange(GRID_W, dtype=F32), rows)
    n_freq = DA_HEAD_DIM // 4
    inv = ROPE_BASE ** (-jnp.arange(n_freq, dtype=F32) / n_freq)
    ang = jnp.concatenate([row[:, None] * inv, col[:, None] * inv], axis=-1)
    cos, sin = jnp.cos(ang), jnp.sin(ang)
    cr, cc, sr, sc = cos[:, :n_freq], cos[:, n_freq:], sin[:, :n_freq], sin[:, n_freq:]
    zr = jnp.zeros_like(sr)
    c64 = jnp.concatenate([cr, cr, cc, cc], axis=1)
    s1 = jnp.concatenate([-sr, zr, -sc, zr], axis=1)
    s2 = jnp.concatenate([zr, sr, zr, sc], axis=1)
    reps = LANES // DA_HEAD_DIM
    pad = lambda t, v: jnp.concatenate([jnp.full((ctx_len, LANES), v, F32), jnp.tile(t, (1, reps))], axis=0)
    return pad(c64, 1.0), pad(s1, 0.0), pad(s2, 0.0)


def _block_diag(w):
    n, c, d = w.shape
    eye = jnp.eye(n, dtype=w.dtype)
    return (eye[:, None, :, None] * w[:, :, None, :]).reshape(n * c, n * d)


def kernel(x, c, ctx, c_ctx, w_mod, b_mod, g_pre, g_post, e_w_in, e_w_out, lru_conv_w, lru_conv_b, lru_w_r, lru_b_r, lru_w_i, lru_b_i, lru_lambda, da_lambda, da_subln, o_w_in, o_w_out, ssd_conv_w, ssd_conv_b, ssd_a_log, ssd_dt_bias, ssd_d, ssd_norm):
    bsz, s, d = x.shape
    ctx_len = ctx.shape[1]
    assert bsz == SUBLANES and ctx_len == TM and s % KV_CHUNK == 0 and w_mod.shape[0] == 2
    assert e_w_in.shape[0] == 1 and o_w_in.shape[0] == 1

    n_rows = 2 * SUBLANES
    c_rows = jnp.concatenate([c, c_ctx[None, :], jnp.zeros((n_rows - bsz - 1, d), F32)], axis=0)
    mod = _modulation(c_rows, w_mod, b_mod)
    mod0 = mod[0].reshape(n_rows, 1, 3 * d)
    mod1 = mod[1].reshape(n_rows, 1, 3 * d)

    w5 = lru_conv_w.shape[-1]
    scale = jnp.concatenate([jnp.ones((2 * w5,), F32), jnp.full((w5,), DA_HEAD_DIM ** -0.5, F32),
                             jnp.ones((3 * w5,), F32)])
    w_in0 = (e_w_in[0] * scale).astype(BF16)
    rope_c, rope_s1, rope_s2 = _rope_tables(s, ctx_len)
    xc, gr, q, k, v, gd = _inproj0(x, ctx, mod0, g_pre[0], w_in0, lru_conv_w[0], lru_conv_b[0],
                                   rope_c, rope_s1, rope_s2)

    wg = jnp.stack([jnp.concatenate([_block_diag(lru_w_r[0, dr]), _block_diag(lru_w_i[0, dr])], axis=1)
                    for dr in range(2)]).astype(BF16)
    bias = jnp.stack([lru_b_r[0, 0], lru_b_i[0, 0], lru_b_r[0, 1], lru_b_i[0, 1]])
    sp = jax.nn.softplus(-lru_lambda[0])
    hf, hb = _lru(xc, wg, bias, sp, ctx_len)

    lambda_init = 0.8 - 0.6 * math.exp(-0.3 * 0)
    on = _attention(q, k, v, da_lambda[0], da_subln[0], lambda_init)
    h1 = _outproj0(hf, hb, gr, on, gd, x, ctx, mod0, g_post[0], e_w_out[0].astype(BF16))

    n_heads = ssd_a_log.shape[-1]
    inner = n_heads * SSD_HEAD_DIM
    nx = ssd_conv_w.shape[-1]
    w1 = o_w_in[0]
    wz = w1[:, :inner].astype(BF16)
    wx = w1[:, inner:inner + nx].astype(BF16)
    nd = 2 * n_heads
    wd = jnp.pad(w1[:, inner + nx:], ((0, 0), (0, LANES - nd))).astype(BF16)
    dt_bias = jnp.pad(ssd_dt_bias[0].reshape(1, nd), ((0, 0), (0, LANES - nd)))
    z, xbc, dt = _inproj1(h1, mod1, g_pre[1], wz, wx, wd, ssd_conv_w[0], ssd_conv_b[0], dt_bias, bsz)
    yf, yb = _ssd(xbc, dt, ssd_a_log[0], ctx_len)
    d_skip = jnp.repeat(ssd_d[0], SSD_HEAD_DIM).reshape(1, inner)
    return _finish1(yf, yb, xbc, z, h1, mod1, d_skip, ssd_norm[0].reshape(1, inner),
                    o_w_out[0].astype(BF16), g_post[1], ctx_len)
```

```python
import functools
import math

import jax
import jax.numpy as jnp
from jax import lax
from jax.experimental import pallas as pl
from jax.experimental.pallas import tpu as pltpu

F32 = jnp.float32
BF16 = jnp.bfloat16

EPS = 1e-6
GRID_W = 64
ROPE_BASE = 10000.0
LRU_C = 8.0
LRU_BLOCKS = 8
DA_HEADS = 4
DA_HEAD_DIM = 64
DA_V_DIM = 128
SSD_HEAD_DIM = 64
SSD_STATE = 128
SSD_GROUPS = 4
SSD_CHUNK = 128

TM = 256
HALO = 8
LRU_TT = 128
KV_CHUNK = 512
LANES = 128
SUBLANES = 8
VMEM_LIMIT = 56 * 1024 * 1024

HIGHEST = lax.Precision.HIGHEST
Q_SCALE = DA_HEAD_DIM ** -0.5 * math.log2(math.e)


def _sigmoid(x):
    return 0.5 * (1.0 + jnp.tanh(0.5 * x))


def _silu(x):
    h = 0.5 * x
    return h * (1.0 + jnp.tanh(h))


def _rms(x):
    return x * lax.rsqrt(jnp.mean(x * x, axis=-1, keepdims=True) + EPS)


def _cparams(sem):
    return pltpu.CompilerParams(dimension_semantics=sem, vmem_limit_bytes=VMEM_LIMIT)


def _mod_kernel(c_ref, w_ref, b_ref, o_ref):
    s = _silu(c_ref[...])
    o_ref[0] = jnp.dot(s, w_ref[0], preferred_element_type=F32, precision=HIGHEST) + b_ref[0]


def _modulation(c_rows, w_mod, b_mod):
    depth, d, n3 = w_mod.shape
    rows = c_rows.shape[0]
    nt = 1024
    return pl.pallas_call(
        _mod_kernel,
        grid=(depth, n3 // nt),
        in_specs=[
            pl.BlockSpec((rows, d), lambda l, n: (0, 0)),
            pl.BlockSpec((1, d, nt), lambda l, n: (l, 0, n)),
            pl.BlockSpec((1, 1, nt), lambda l, n: (l, 0, n)),
        ],
        out_specs=pl.BlockSpec((1, rows, nt), lambda l, n: (l, 0, n)),
        out_shape=jax.ShapeDtypeStruct((depth, rows, n3), F32),
        compiler_params=_cparams(("arbitrary", "arbitrary")),
        name="modulation",
    )(c_rows, w_mod, b_mod.reshape(depth, 1, n3))


def _normed_rows(h_ext, mod_ref, g_ref, d):
    sh = mod_ref[0, :, 0:d]
    sc = mod_ref[0, :, d:2 * d]
    u = (_rms(h_ext) * g_ref[...]) * (1.0 + sc) + sh
    return u.astype(BF16)


def _halo_mask(x_ext, left_ok, right_ok):
    r = lax.broadcasted_iota(jnp.int32, x_ext.shape, 0)
    lo = jnp.where(left_ok, 0, HALO)
    hi = jnp.where(right_ok, TM + 2 * HALO, TM + HALO)
    return jnp.where((r < lo) | (r >= hi), 0.0, x_ext)


def _conv4(ext_ref, w_ref, b_ref):
    acc = b_ref[...] + w_ref[0:1, :] * ext_ref[pl.ds(HALO - 2, TM), :]
    acc = acc + w_ref[1:2, :] * ext_ref[pl.ds(HALO - 1, TM), :]
    acc = acc + w_ref[2:3, :] * ext_ref[pl.ds(HALO, TM), :]
    acc = acc + w_ref[3:4, :] * ext_ref[pl.ds(HALO + 1, TM), :]
    return acc


def _inproj0_kernel(n_tiles, xm_ref, xl_ref, xr_ref, ctx_ref, mod_ref, g_ref, w_ref, cw_ref, cb_ref,
                    rc_ref, rs1_ref, rs2_ref,
                    xc_ref, gr_ref, q_ref, k_ref, v_ref, gd_ref, ext_ref):
    j = pl.program_id(0)
    d = xm_ref.shape[-1]
    w5 = xc_ref.shape[-1]
    is_ctx = j == 0
    h_main = jnp.where(is_ctx, ctx_ref[0], xm_ref[0])
    h_ext = jnp.concatenate([xl_ref[0], h_main, xr_ref[0]], axis=0)
    u_ext = _normed_rows(h_ext, mod_ref, g_ref, d)
    u = u_ext[HALO:HALO + TM]

    x_ext = jnp.dot(u_ext, w_ref[:, 0:w5], preferred_element_type=F32)
    ext_ref[...] = _halo_mask(x_ext, j >= 2, (j >= 1) & (j <= n_tiles - 2))
    xc_ref[0] = _conv4(ext_ref, cw_ref, cb_ref)

    gr_ref[0] = jnp.dot(u, w_ref[:, w5:2 * w5], preferred_element_type=F32).astype(BF16)
    reps = w5 // LANES
    c = jnp.concatenate([rc_ref[...]] * reps, axis=1)
    s1 = jnp.concatenate([rs1_ref[...]] * reps, axis=1)
    s2 = jnp.concatenate([rs2_ref[...]] * reps, axis=1)
    for idx, o_ref, post in ((2, q_ref, Q_SCALE), (3, k_ref, None)):
        t = jnp.dot(u, w_ref[:, idx * w5:(idx + 1) * w5], preferred_element_type=F32)
        t = t * c + pltpu.roll(t, w5 - 16, 1) * s1 + pltpu.roll(t, 16, 1) * s2
        o_ref[0] = (t if post is None else t * post).astype(BF16)
    v_ref[0] = jnp.dot(u, w_ref[:, 4 * w5:5 * w5], preferred_element_type=F32).astype(BF16)
    gd_ref[0] = jnp.dot(u, w_ref[:, 5 * w5:6 * w5], preferred_element_type=F32).astype(BF16)


def _inproj0(x, ctx, mod, g_pre, w_in, conv_w, conv_b, rope_c, rope_s1, rope_s2):
    bsz, s, d = x.shape
    n_lat = s // TM
    n_tiles = n_lat + 1
    t_all = n_tiles * TM
    w5 = conv_w.shape[-1]
    hb = TM // HALO

    def main_map(j, b):
        return (b, jnp.maximum(j - 1, 0), 0)

    def left_map(j, b):
        return (b, jnp.maximum((j - 1) * hb - 1, 0), 0)

    def right_map(j, b):
        return (b, jnp.minimum(jnp.maximum(j, 0) * hb, s // HALO - 1), 0)

    def mod_map(j, b):
        return (jnp.where(j == 0, bsz, b), 0, 0)

    const2 = lambda j, b: (0, 0)
    tile_out = pl.BlockSpec((1, TM, w5), lambda j, b: (b, j, 0))
    rope_spec = pl.BlockSpec((TM, LANES), lambda j, b: (j, 0))
    outs = pl.pallas_call(
        functools.partial(_inproj0_kernel, n_tiles),
        grid=(n_tiles, bsz),
        in_specs=[
            pl.BlockSpec((1, TM, d), main_map),
            pl.BlockSpec((1, HALO, d), left_map),
            pl.BlockSpec((1, HALO, d), right_map),
            pl.BlockSpec((1, TM, d), lambda j, b: (b, 0, 0)),
            pl.BlockSpec((1, 1, 3 * d), mod_map),
            pl.BlockSpec((1, d), const2),
            pl.BlockSpec(w_in.shape, const2),
            pl.BlockSpec(conv_w.shape, const2),
            pl.BlockSpec((1, w5), const2),
            rope_spec, rope_spec, rope_spec,
        ],
        out_specs=[tile_out] * 6,
        out_shape=[jax.ShapeDtypeStruct((bsz, t_all, w5), F32)]
        + [jax.ShapeDtypeStruct((bsz, t_all, w5), BF16)] * 5,
        scratch_shapes=[pltpu.VMEM((TM + 2 * HALO, w5), F32)],
        compiler_params=_cparams(("arbitrary", "arbitrary")),
        name="inproj0",
    )(x, x, x, ctx, mod, g_pre.reshape(1, d), w_in, conv_w, conv_b.reshape(1, w5), rope_c, rope_s1, rope_s2)
    return outs


def _lru_kernel(bsz, xf_ref, xb_ref, wg_ref, bias_ref, sp_ref, of_ref, ob_ref, a_scr, b_scr, hs_scr, h_scr):
    i = pl.program_id(0)
    w5 = xf_ref.shape[-1]
    nslab = w5 // LANES
    tt = LRU_TT

    @pl.when(i == 0)
    def _():
        h_scr[...] = jnp.zeros_like(h_scr)

    for dr, x_ref in enumerate((xf_ref, xb_ref)):
        sp = sp_ref[dr:dr + 1, :]
        for b in range(bsz):
            xc = x_ref[b]
            g = jnp.dot(xc.astype(BF16), wg_ref[dr], preferred_element_type=F32)
            r = _sigmoid(g[:, 0:w5] + bias_ref[2 * dr:2 * dr + 1, :])
            ig = _sigmoid(g[:, w5:2 * w5] + bias_ref[2 * dr + 1:2 * dr + 2, :])
            log_a = (-LRU_C) * r * sp
            a = jnp.exp(log_a)
            bc = jnp.sqrt(-jnp.tanh(log_a) * (a * a + 1.0)) * (ig * xc)
            for c in range(nslab):
                a_scr[dr, c, pl.ds(b, tt, stride=bsz), :] = a[:, c * LANES:(c + 1) * LANES]
                b_scr[dr, c, pl.ds(b, tt, stride=bsz), :] = bc[:, c * LANES:(c + 1) * LANES]

    def step(t, carry):
        hf, hb = carry
        rf = pl.multiple_of(t * bsz, bsz)
        rb = pl.multiple_of((tt - 1 - t) * bsz, bsz)
        nf, nb = [], []
        for c in range(nslab):
            h = a_scr[0, c, pl.ds(rf, bsz), :] * hf[c] + b_scr[0, c, pl.ds(rf, bsz), :]
            hs_scr[0, c, pl.ds(rf, bsz), :] = h
            nf.append(h)
            h = a_scr[1, c, pl.ds(rb, bsz), :] * hb[c] + b_scr[1, c, pl.ds(rb, bsz), :]
            hs_scr[1, c, pl.ds(rb, bsz), :] = h
            nb.append(h)
        return tuple(nf), tuple(nb)

    h0f = tuple(h_scr[0, c] for c in range(nslab))
    h0b = tuple(h_scr[1, c] for c in range(nslab))
    hf, hb = lax.fori_loop(0, tt, step, (h0f, h0b), unroll=8)
    for c in range(nslab):
        h_scr[0, c] = hf[c]
        h_scr[1, c] = hb[c]

    for dr, o_ref in enumerate((of_ref, ob_ref)):
        for b in range(bsz):
            for c in range(nslab):
                o_ref[b, :, c * LANES:(c + 1) * LANES] = hs_scr[dr, c, pl.ds(b, tt, stride=bsz), :].astype(BF16)


def _bwd_tile(i, n_ctx, n_all):
    return jnp.where(i < n_ctx, n_ctx - 1 - i, n_all - 1 - (i - n_ctx))


def _lru(xc, wg, bias, sp, ctx_len):
    bsz, t_all, w5 = xc.shape
    tt = LRU_TT
    n_all = t_all // tt
    n_ctx = ctx_len // tt
    nslab = w5 // LANES
    fwd_spec = pl.BlockSpec((bsz, tt, w5), lambda i: (0, i, 0))
    bwd_spec = pl.BlockSpec((bsz, tt, w5), lambda i: (0, _bwd_tile(i, n_ctx, n_all), 0))
    const = lambda shape: pl.BlockSpec(shape, lambda i: (0,) * len(shape))
    coef = pltpu.VMEM((2, nslab, tt * bsz, LANES), F32)
    return pl.pallas_call(
        functools.partial(_lru_kernel, bsz),
        grid=(n_all,),
        in_specs=[fwd_spec, bwd_spec, const(wg.shape), const(bias.shape), const(sp.shape)],
        out_specs=[fwd_spec, bwd_spec],
        out_shape=[jax.ShapeDtypeStruct((bsz, t_all, w5), BF16)] * 2,
        scratch_shapes=[coef, coef, coef, pltpu.VMEM((2, nslab, bsz, LANES), F32)],
        compiler_params=_cparams(("arbitrary",)),
        name="rglru",
    )(xc, xc, wg, bias, sp)


def _attn_kernel(lambda_init, n_lat_chunks, q_ref, k_ref, v_ref, lam_ref, sub_ref, o_ref, m_scr, acc_scr):
    j = pl.program_id(2)
    q = q_ref[0]
    lane = lax.broadcasted_iota(jnp.int32, q.shape, 1)
    zero = jnp.zeros_like(q)
    qq = jnp.concatenate([jnp.where(lane < DA_HEAD_DIM, q, zero), jnp.where(lane >= DA_HEAD_DIM, q, zero)], axis=0)

    m_scr[...] = jnp.full_like(m_scr, -jnp.inf)
    acc_scr[...] = jnp.zeros_like(acc_scr)

    def chunk(start, size):
        kc = k_ref[0, pl.ds(start, size), :]
        va = jnp.concatenate([v_ref[0, pl.ds(start, size), :], jnp.ones((size, LANES), BF16)], axis=1)
        s = lax.dot_general(qq, kc, (((1,), (1,)), ((), ())), preferred_element_type=F32)
        m_prev = m_scr[...]
        m_next = jnp.maximum(m_prev, jnp.max(s, axis=1, keepdims=True))
        p = jnp.exp2(s - pltpu.repeat(m_next, size // LANES, 1))
        alpha = jnp.exp2(m_prev - m_next)
        acc_scr[...] = acc_scr[...] * pltpu.repeat(alpha, 2, 1) + jnp.dot(p.astype(BF16), va,
                                                                         preferred_element_type=F32)
        m_scr[...] = m_next

    chunk(0, TM)

    @pl.when(j > 0)
    def _():
        def body(c, carry):
            chunk(pl.multiple_of(TM + c * KV_CHUNK, TM), KV_CHUNK)
            return carry
        lax.fori_loop(0, n_lat_chunks, body, 0, unroll=True)

    lm = lam_ref[...]
    lam = (jnp.exp(jnp.sum(lm[0:1] * lm[1:2], axis=1, keepdims=True))
           - jnp.exp(jnp.sum(lm[2:3] * lm[3:4], axis=1, keepdims=True)) + lambda_init)
    o = (acc_scr[0:TM, 0:LANES] / acc_scr[0:TM, LANES:2 * LANES]
         - lam * (acc_scr[TM:2 * TM, 0:LANES] / acc_scr[TM:2 * TM, LANES:2 * LANES]))
    o_ref[0] = ((_rms(o) * sub_ref[...]) * (1.0 - lambda_init)).astype(BF16)


def _attention(q, k, v, da_lambda, da_subln, lambda_init):
    bsz, t_all, _ = q.shape
    n_tiles = t_all // TM
    n_lat_chunks = (t_all - TM) // KV_CHUNK
    hd = DA_V_DIM
    q_spec = pl.BlockSpec((1, TM, hd), lambda b, h, j: (b, j, h))
    kv_spec = pl.BlockSpec((1, t_all, hd), lambda b, h, j: (b, 0, h))
    return pl.pallas_call(
        functools.partial(_attn_kernel, lambda_init, n_lat_chunks),
        grid=(bsz, DA_HEADS, n_tiles),
        in_specs=[q_spec, kv_spec, kv_spec,
                  pl.BlockSpec(da_lambda.shape, lambda b, h, j: (0, 0)),
                  pl.BlockSpec((1, hd), lambda b, h, j: (0, 0))],
        out_specs=q_spec,
        out_shape=jax.ShapeDtypeStruct(q.shape, BF16),
        scratch_shapes=[pltpu.VMEM((2 * TM, LANES), F32), pltpu.VMEM((2 * TM, 2 * LANES), F32)],
        compiler_params=_cparams(("arbitrary", "arbitrary", "arbitrary")),
        name="diffattn",
    )(q, k, v, da_lambda, da_subln.reshape(1, hd))


def _outproj0_kernel(hf_ref, hb_ref, gr_ref, on_ref, gd_ref, xm_ref, ctx_ref, mod_ref, g_ref, w_ref, o_ref):
    j = pl.program_id(0)
    d = o_ref.shape[-1]
    w5 = hf_ref.shape[-1]
    r = hf_ref[0].astype(F32) + hb_ref[0].astype(F32)
    m1 = (r * _silu(gr_ref[0].astype(F32))).astype(BF16)
    m2 = (on_ref[0].astype(F32) * _silu(gd_ref[0].astype(F32))).astype(BF16)
    y = jnp.dot(m1, w_ref[0:w5, :], preferred_element_type=F32)
    y = y + jnp.dot(m2, w_ref[w5:2 * w5, :], preferred_element_type=F32)
    gt = mod_ref[0, :, 2 * d:3 * d]
    h = jnp.where(j == 0, ctx_ref[0], xm_ref[0])
    o_ref[0] = h + gt * (_rms(y) * g_ref[...])


def _outproj0(hf, hb, gr, on, gd, x, ctx, mod, g_post, w_out):
    bsz, t_all, w5 = hf.shape
    d = x.shape[-1]
    n_tiles = t_all // TM
    tile = pl.BlockSpec((1, TM, w5), lambda j, b: (b, j, 0))
    const2 = lambda j, b: (0, 0)
    return pl.pallas_call(
        _outproj0_kernel,
        grid=(n_tiles, bsz),
        in_specs=[tile] * 5 + [
            pl.BlockSpec((1, TM, d), lambda j, b: (b, jnp.maximum(j - 1, 0), 0)),
            pl.BlockSpec((1, TM, d), lambda j, b: (b, 0, 0)),
            pl.BlockSpec((1, 1, 3 * d), lambda j, b: (jnp.where(j == 0, bsz, b), 0, 0)),
            pl.BlockSpec((1, d), const2),
            pl.BlockSpec(w_out.shape, const2),
        ],
        out_specs=pl.BlockSpec((1, TM, d), lambda j, b: (b, j, 0)),
        out_shape=jax.ShapeDtypeStruct((bsz, t_all, d), F32),
        compiler_params=_cparams(("arbitrary", "arbitrary")),
        name="outproj0",
    )(hf, hb, gr, on, gd, x, ctx, mod, g_post.reshape(1, d), w_out)


def _inproj1_kernel(n_tiles, hm_ref, hl_ref, hr_ref, mod_ref, g_ref, wz_ref, wx_ref, wd_ref, cw_ref, cb_ref, db_ref,
                    z_ref, xbc_ref, dt_ref, ext_ref):
    j = pl.program_id(0)
    d = hm_ref.shape[-1]
    h_ext = jnp.concatenate([hl_ref[0], hm_ref[0], hr_ref[0]], axis=0)
    u_ext = _normed_rows(h_ext, mod_ref, g_ref, d)
    u = u_ext[HALO:HALO + TM]
    z_ref[0] = jnp.dot(u, wz_ref[...], preferred_element_type=F32).astype(BF16)
    x_ext = jnp.dot(u_ext, wx_ref[...], preferred_element_type=F32)
    ext_ref[...] = _halo_mask(x_ext, j >= 2, (j >= 1) & (j <= n_tiles - 2))
    xbc_ref[0] = _silu(_conv4(ext_ref, cw_ref, cb_ref)).astype(BF16)
    dt_ref[0] = jax.nn.softplus(jnp.dot(u, wd_ref[...], preferred_element_type=F32) + db_ref[...])


def _inproj1(h, mod, g_pre, wz, wx, wd, conv_w, conv_b, dt_bias, n_mod_rows):
    bsz, t_all, d = h.shape
    n_tiles = t_all // TM
    hb = TM // HALO
    nz, nx, nd = wz.shape[1], wx.shape[1], wd.shape[1]
    const2 = lambda j, b: (0, 0)
    return pl.pallas_call(
        functools.partial(_inproj1_kernel, n_tiles),
        grid=(n_tiles, bsz),
        in_specs=[
            pl.BlockSpec((1, TM, d), lambda j, b: (b, j, 0)),
            pl.BlockSpec((1, HALO, d), lambda j, b: (b, jnp.maximum(j * hb - 1, 0), 0)),
            pl.BlockSpec((1, HALO, d), lambda j, b: (b, jnp.minimum((j + 1) * hb, t_all // HALO - 1), 0)),
            pl.BlockSpec((1, 1, 3 * d), lambda j, b: (jnp.where(j == 0, n_mod_rows, b), 0, 0)),
            pl.BlockSpec((1, d), const2),
            pl.BlockSpec(wz.shape, const2),
            pl.BlockSpec(wx.shape, const2),
            pl.BlockSpec(wd.shape, const2),
            pl.BlockSpec(conv_w.shape, const2),
            pl.BlockSpec((1, nx), const2),
            pl.BlockSpec((1, nd), const2),
        ],
        out_specs=[pl.BlockSpec((1, TM, nz), lambda j, b: (b, j, 0)),
                   pl.BlockSpec((1, TM, nx), lambda j, b: (b, j, 0)),
                   pl.BlockSpec((1, TM, nd), lambda j, b: (b, j, 0))],
        out_shape=[jax.ShapeDtypeStruct((bsz, t_all, nz), BF16),
                   jax.ShapeDtypeStruct((bsz, t_all, nx), BF16),
                   jax.ShapeDtypeStruct((bsz, t_all, nd), F32)],
        scratch_shapes=[pltpu.VMEM((TM + 2 * HALO, nx), F32)],
        compiler_params=_cparams(("arbitrary", "arbitrary")),
        name="inproj1",
    )(h, h, h, mod, g_pre.reshape(1, d), wz, wx, wd, conv_w, conv_b.reshape(1, nx), dt_bias)


def _ssd_direction(reverse, dr, xbc_ref, dt_ref, alog_ref, s_scr, y_ref):
    ch = SSD_CHUNK
    inner = y_ref.shape[-1]
    n_heads = inner // SSD_HEAD_DIM
    gn = SSD_GROUPS * SSD_STATE
    pairs_per_group = n_heads // SSD_GROUPS // 2
    h0 = dr * n_heads

    dt = dt_ref[0]
    a_neg = -jnp.exp(alog_ref[dr:dr + 1, :])
    adt = dt * a_neg
    row = lax.broadcasted_iota(jnp.int32, (ch, ch), 0)
    col = lax.broadcasted_iota(jnp.int32, (ch, ch), 1)
    mask = (row <= col) if reverse else (row >= col)
    cs = jnp.dot(mask.astype(F32), adt, preferred_element_type=F32, precision=HIGHEST)
    last = 0 if reverse else ch - 1
    cs_t = cs.T
    dt_t = dt.T
    w_t = jnp.exp(cs_t[:, last:last + 1] - cs_t) * dt_t
    e_tot = jnp.exp(cs[last:last + 1, :])
    lane = lax.broadcasted_iota(jnp.int32, (ch, LANES), 1)
    left = lane < SSD_HEAD_DIM

    for g in range(SSD_GROUPS):
        bg = xbc_ref[0, :, inner + g * SSD_STATE:inner + (g + 1) * SSD_STATE]
        cg = xbc_ref[0, :, inner + gn + g * SSD_STATE:inner + gn + (g + 1) * SSD_STATE]
        cb = lax.dot_general(cg, bg, (((1,), (1,)), ((), ())), preferred_element_type=F32)
        bg_t = bg.astype(F32).T
        cg32 = cg.astype(F32)
        for pp in range(pairs_per_group):
            p = g * pairs_per_group + pp
            x2 = xbc_ref[0, :, p * LANES:(p + 1) * LANES]
            zero = jnp.zeros_like(x2)
            wx = jnp.concatenate([jnp.where(left, x2, zero), jnp.where(left, zero, x2)], axis=0)
            sp = s_scr[dr, p]
            sp16 = sp.astype(BF16)
            zero = jnp.zeros_like(sp16)
            ws = jnp.concatenate([jnp.where(left, sp16, zero), jnp.where(left, zero, sp16)], axis=0)
            ms, ces, bws = [], [], []
            for h in (h0 + 2 * p, h0 + 2 * p + 1):
                ccol = jnp.broadcast_to(cs[:, h:h + 1], (ch, ch))
                crow = cs_t[h:h + 1, :]
                lmat = jnp.exp(jnp.where(mask, ccol - crow, -jnp.inf))
                ms.append((cb * lmat * dt_t[h:h + 1, :]).astype(BF16))
                ces.append((cg32 * jnp.exp(ccol)).astype(BF16))
                bws.append((bg_t * w_t[h:h + 1, :]).astype(BF16))
            y = jnp.dot(jnp.concatenate(ms + ces, axis=1), jnp.concatenate([wx, ws], axis=0),
                        preferred_element_type=F32)
            y_ref[0, :, p * LANES:(p + 1) * LANES] = y.astype(BF16)
            ds = jnp.dot(jnp.concatenate(bws, axis=1), wx, preferred_element_type=F32)
            dec = jnp.where(left[0:1, :], e_tot[:, h0 + 2 * p:h0 + 2 * p + 1], e_tot[:, h0 + 2 * p + 1:h0 + 2 * p + 2])
            s_scr[dr, p] = sp * dec + ds


def _ssd_kernel(xf_ref, df_ref, xb_ref, db_ref, alog_ref, yf_ref, yb_ref, s_scr):
    @pl.when(pl.program_id(1) == 0)
    def _():
        s_scr[...] = jnp.zeros_like(s_scr)

    _ssd_direction(False, 0, xf_ref, df_ref, alog_ref, s_scr, yf_ref)
    _ssd_direction(True, 1, xb_ref, db_ref, alog_ref, s_scr, yb_ref)


def _ssd(xbc, dt, a_log, ctx_len):
    bsz, t_all, nx = xbc.shape
    nd = dt.shape[-1]
    n_heads = a_log.shape[-1]
    inner = n_heads * SSD_HEAD_DIM
    a_log = jnp.stack([jnp.pad(a_log[dr], (dr * n_heads, nd - (dr + 1) * n_heads)) for dr in range(2)])
    ch = SSD_CHUNK
    n_all = t_all // ch
    n_ctx = ctx_len // ch
    fwd = lambda b, i: (b, i, 0)
    bwd = lambda b, i: (b, _bwd_tile(i, n_ctx, n_all), 0)
    return pl.pallas_call(
        _ssd_kernel,
        grid=(bsz, n_all),
        in_specs=[pl.BlockSpec((1, ch, nx), fwd), pl.BlockSpec((1, ch, nd), fwd),
                  pl.BlockSpec((1, ch, nx), bwd), pl.BlockSpec((1, ch, nd), bwd),
                  pl.BlockSpec(a_log.shape, lambda b, i: (0, 0))],
        out_specs=[pl.BlockSpec((1, ch, inner), fwd), pl.BlockSpec((1, ch, inner), bwd)],
        out_shape=[jax.ShapeDtypeStruct((bsz, t_all, inner), BF16)] * 2,
        scratch_shapes=[pltpu.VMEM((2, n_heads // 2, SSD_STATE, LANES), F32)],
        compiler_params=_cparams(("arbitrary", "arbitrary")),
        name="ssd",
    )(xbc, dt, xbc, dt, a_log)


def _finish1_kernel(yf_ref, yb_ref, xs_ref, z_ref, h_ref, mod_ref, dsk_ref, nw_ref, w_ref, g_ref, o_ref):
    d = o_ref.shape[-1]
    inner = yf_ref.shape[-1]
    gw = inner // SSD_GROUPS
    y = yf_ref[0].astype(F32) + yb_ref[0].astype(F32) + dsk_ref[...] * xs_ref[0].astype(F32)
    y = y * _silu(z_ref[0].astype(F32))
    parts = []
    for g in range(SSD_GROUPS):
        yg = y[:, g * gw:(g + 1) * gw]
        parts.append((_rms(yg) * nw_ref[:, g * gw:(g + 1) * gw]).astype(BF16))
    out = jnp.dot(jnp.concatenate(parts, axis=1), w_ref[...], preferred_element_type=F32)
    gt = mod_ref[0, :, 2 * d:3 * d]
    o_ref[0] = h_ref[0] + gt * (_rms(out) * g_ref[...])


def _finish1(yf, yb, xbc, z, h, mod, d_skip, norm_w, w_out, g_post, ctx_len):
    bsz, t_all, inner = yf.shape
    d = h.shape[-1]
    off = ctx_len // TM
    n_lat = t_all // TM - off
    lat = lambda j, b: (b, j + off, 0)
    const2 = lambda j, b: (0, 0)
    return pl.pallas_call(
        _finish1_kernel,
        grid=(n_lat, bsz),
        in_specs=[pl.BlockSpec((1, TM, inner), lat)] * 4 + [
            pl.BlockSpec((1, TM, d), lat),
            pl.BlockSpec((1, 1, 3 * d), lambda j, b: (b, 0, 0)),
            pl.BlockSpec((1, inner), const2),
            pl.BlockSpec((1, inner), const2),
            pl.BlockSpec(w_out.shape, const2),
            pl.BlockSpec((1, d), const2),
        ],
        out_specs=pl.BlockSpec((1, TM, d), lambda j, b: (b, j, 0)),
        out_shape=jax.ShapeDtypeStruct((bsz, n_lat * TM, d), F32),
        compiler_params=_cparams(("arbitrary", "arbitrary")),
        name="finish1",
    )(yf, yb, xbc, z, h, mod, d_skip, norm_w, w_out, g_post.reshape(1, d))


def _rope_tables(n_tokens, ctx_len):
    rows = n_tokens // GRID_W
    row = jnp.repeat(jnp.arange(rows, dtype=F32), GRID_W)
    col = jnp.tile(jnp.arange(GRID_W, dtype=F32), rows)
    n_freq = DA_HEAD_DIM // 4
    inv = ROPE_BASE ** (-jnp.arange(n_freq, dtype=F32) / n_freq)
    ang = jnp.concatenate([row[:, None] * inv, col[:, None] * inv], axis=-1)
    cos, sin = jnp.cos(ang), jnp.sin(ang)
    cr, cc, sr, sc = cos[:, :n_freq], cos[:, n_freq:], sin[:, :n_freq], sin[:, n_freq:]
    zr = jnp.zeros_like(sr)
    c64 = jnp.concatenate([cr, cr, cc, cc], axis=1)
    s1 = jnp.concatenate([-sr, zr, -sc, zr], axis=1)
    s2 = jnp.concatenate([zr, sr, zr, sc], axis=1)
    reps = LANES // DA_HEAD_DIM
    pad = lambda t, v: jnp.concatenate([jnp.full((ctx_len, LANES), v, F32), jnp.tile(t, (1, reps))], axis=0)
    return pad(c64, 1.0), pad(s1, 0.0), pad(s2, 0.0)


def _block_diag(w):
    n, c, d = w.shape
    eye = jnp.eye(n, dtype=w.dtype)
    return (eye[:, None, :, None] * w[:, :, None, :]).reshape(n * c, n * d)


def kernel(x, c, ctx, c_ctx, w_mod, b_mod, g_pre, g_post, e_w_in, e_w_out, lru_conv_w, lru_conv_b, lru_w_r, lru_b_r, lru_w_i, lru_b_i, lru_lambda, da_lambda, da_subln, o_w_in, o_w_out, ssd_conv_w, ssd_conv_b, ssd_a_log, ssd_dt_bias, ssd_d, ssd_norm):
    bsz, s, d = x.shape
    ctx_len = ctx.shape[1]
    assert bsz == SUBLANES and ctx_len == TM and s % KV_CHUNK == 0 and w_mod.shape[0] == 2
    assert e_w_in.shape[0] == 1 and o_w_in.shape[0] == 1

    n_rows = 2 * SUBLANES
    c_rows = jnp.concatenate([c, c_ctx[None, :], jnp.zeros((n_rows - bsz - 1, d), F32)], axis=0)
    mod = _modulation(c_rows, w_mod, b_mod)
    mod0 = mod[0].reshape(n_rows, 1, 3 * d)
    mod1 = mod[1].reshape(n_rows, 1, 3 * d)

    w5 = lru_conv_w.shape[-1]
    w_in0 = e_w_in[0].astype(BF16)
    rope_c, rope_s1, rope_s2 = _rope_tables(s, ctx_len)
    xc, gr, q, k, v, gd = _inproj0(x, ctx, mod0, g_pre[0], w_in0, lru_conv_w[0], lru_conv_b[0],
                                   rope_c, rope_s1, rope_s2)

    wg = jnp.stack([jnp.concatenate([_block_diag(lru_w_r[0, dr]), _block_diag(lru_w_i[0, dr])], axis=1)
                    for dr in range(2)]).astype(BF16)
    bias = jnp.stack([lru_b_r[0, 0], lru_b_i[0, 0], lru_b_r[0, 1], lru_b_i[0, 1]])
    sp = jax.nn.softplus(-lru_lambda[0])
    hf, hb = _lru(xc, wg, bias, sp, ctx_len)

    lambda_init = 0.8 - 0.6 * math.exp(-0.3 * 0)
    on = _attention(q, k, v, da_lambda[0], da_subln[0], lambda_init)
    h1 = _outproj0(hf, hb, gr, on, gd, x, ctx, mod0, g_post[0], e_w_out[0].astype(BF16))

    n_heads = ssd_a_log.shape[-1]
    inner = n_heads * SSD_HEAD_DIM
    nx = ssd_conv_w.shape[-1]
    w1 = o_w_in[0]
    wz = w1[:, :inner].astype(BF16)
    wx = w1[:, inner:inner + nx].astype(BF16)
    nd = 2 * n_heads
    wd = jnp.pad(w1[:, inner + nx:], ((0, 0), (0, LANES - nd))).astype(BF16)
    dt_bias = jnp.pad(ssd_dt_bias[0].reshape(1, nd), ((0, 0), (0, LANES - nd)))
    z, xbc, dt = _inproj1(h1, mod1, g_pre[1], wz, wx, wd, ssd_conv_w[0], ssd_conv_b[0], dt_bias, bsz)
    yf, yb = _ssd(xbc, dt, ssd_a_log[0], ctx_len)
    d_skip = jnp.repeat(ssd_d[0], SSD_HEAD_DIM).reshape(1, inner)
    return _finish1(yf, yb, xbc, z, h1, mod1, d_skip, ssd_norm[0].reshape(1, inner),
                    o_w_out[0].astype(BF16), g_post[1], ctx_len)
```

```python
import functools
import math

import jax
import jax.numpy as jnp
from jax import lax
from jax.experimental import pallas as pl
from jax.experimental.pallas import tpu as pltpu

F32 = jnp.float32
BF16 = jnp.bfloat16

EPS = 1e-6
GRID_W = 64
ROPE_BASE = 10000.0
LRU_C = 8.0
LRU_BLOCKS = 8
DA_HEADS = 4
DA_HEAD_DIM = 64
DA_V_DIM = 128
SSD_HEAD_DIM = 64
SSD_STATE = 128
SSD_GROUPS = 4
SSD_CHUNK = 128

TM = 256
HALO = 8
LRU_TT = 128
KV_CHUNK = 512
INPROJ_COLS = 512
LANES = 128
SUBLANES = 8
VMEM_LIMIT = 56 * 1024 * 1024

HIGHEST = lax.Precision.HIGHEST
LOG2E = math.log2(math.e)
Q_SCALE = DA_HEAD_DIM ** -0.5 * LOG2E


def _sigmoid(x):
    return 0.5 * (1.0 + jnp.tanh(0.5 * x))


def _silu(x):
    h = 0.5 * x
    return h * (1.0 + jnp.tanh(h))


def _rms(x):
    return x * lax.rsqrt(jnp.mean(x * x, axis=-1, keepdims=True) + EPS)


def _cparams(sem):
    return pltpu.CompilerParams(dimension_semantics=sem, vmem_limit_bytes=VMEM_LIMIT)


def _mod_kernel(c_ref, w_ref, b_ref, o_ref):
    s = _silu(c_ref[...])
    o_ref[0] = jnp.dot(s, w_ref[0], preferred_element_type=F32, precision=HIGHEST) + b_ref[0]


def _modulation(c_rows, w_mod, b_mod):
    depth, d, n3 = w_mod.shape
    rows = c_rows.shape[0]
    nt = 1024
    return pl.pallas_call(
        _mod_kernel,
        grid=(depth, n3 // nt),
        in_specs=[
            pl.BlockSpec((rows, d), lambda l, n: (0, 0)),
            pl.BlockSpec((1, d, nt), lambda l, n: (l, 0, n)),
            pl.BlockSpec((1, 1, nt), lambda l, n: (l, 0, n)),
        ],
        out_specs=pl.BlockSpec((1, rows, nt), lambda l, n: (l, 0, n)),
        out_shape=jax.ShapeDtypeStruct((depth, rows, n3), F32),
        compiler_params=_cparams(("arbitrary", "arbitrary")),
        name="modulation",
    )(c_rows, w_mod, b_mod.reshape(depth, 1, n3))


def _normed_rows(h_ext, mod_ref, g_ref, d):
    sh = mod_ref[0, :, 0:d]
    sc = mod_ref[0, :, d:2 * d]
    u = (_rms(h_ext) * g_ref[...]) * (1.0 + sc) + sh
    return u.astype(BF16)


def _store_ext(ext_ref, slab0, x_ext, left_ok, right_ok):
    n = x_ext.shape[1] // LANES
    zero = jnp.zeros((HALO, LANES), F32)
    for c in range(n):
        blk = x_ext[:, c * LANES:(c + 1) * LANES]
        ext_ref[slab0 + c, 0:HALO, :] = jnp.where(left_ok, blk[0:HALO], zero)
        ext_ref[slab0 + c, HALO:HALO + TM, :] = blk[HALO:HALO + TM]
        ext_ref[slab0 + c, HALO + TM:, :] = jnp.where(right_ok, blk[HALO + TM:], zero)


def _conv4(ext_ref, c, w_ref, b_ref):
    ls = slice(c * LANES, (c + 1) * LANES)
    acc = b_ref[:, ls] + w_ref[0:1, ls] * ext_ref[c, pl.ds(HALO - 2, TM), :]
    acc = acc + w_ref[1:2, ls] * ext_ref[c, pl.ds(HALO - 1, TM), :]
    acc = acc + w_ref[2:3, ls] * ext_ref[c, pl.ds(HALO, TM), :]
    acc = acc + w_ref[3:4, ls] * ext_ref[c, pl.ds(HALO + 1, TM), :]
    return acc


def _inproj0_kernel(n_tiles, xm_ref, xl_ref, xr_ref, ctx_ref, mod_ref, g_ref, w_ref, cw_ref, cb_ref,
                    rc_ref, rs1_ref, rs2_ref,
                    xc_ref, gr_ref, q_ref, k_ref, v_ref, gd_ref, ext_ref):
    j = pl.program_id(0)
    d = xm_ref.shape[-1]
    w5 = xc_ref.shape[-1]
    is_ctx = j == 0
    h_main = jnp.where(is_ctx, ctx_ref[0], xm_ref[0])
    h_ext = jnp.concatenate([xl_ref[0], h_main, xr_ref[0]], axis=0)
    u_ext = _normed_rows(h_ext, mod_ref, g_ref, d)
    u = u_ext[HALO:HALO + TM]

    x_ext = jnp.dot(u_ext, w_ref[:, 0:w5], preferred_element_type=F32)
    _store_ext(ext_ref, 0, x_ext, j >= 2, (j >= 1) & (j <= n_tiles - 2))
    for cs in range(w5 // LANES):
        xc_ref[0, :, cs * LANES:(cs + 1) * LANES] = _conv4(ext_ref, cs, cw_ref, cb_ref)

    gr_ref[0] = jnp.dot(u, w_ref[:, w5:2 * w5], preferred_element_type=F32).astype(BF16)
    reps = w5 // LANES
    c = jnp.concatenate([rc_ref[...]] * reps, axis=1)
    s1 = jnp.concatenate([rs1_ref[...]] * reps, axis=1)
    s2 = jnp.concatenate([rs2_ref[...]] * reps, axis=1)
    for idx, o_ref, post in ((2, q_ref, Q_SCALE), (3, k_ref, None)):
        t = jnp.dot(u, w_ref[:, idx * w5:(idx + 1) * w5], preferred_element_type=F32)
        t = t * c + pltpu.roll(t, w5 - 16, 1) * s1 + pltpu.roll(t, 16, 1) * s2
        o_ref[0] = (t if post is None else t * post).astype(BF16)
    v_ref[0] = jnp.dot(u, w_ref[:, 4 * w5:5 * w5], preferred_element_type=F32).astype(BF16)
    gd_ref[0] = jnp.dot(u, w_ref[:, 5 * w5:6 * w5], preferred_element_type=F32).astype(BF16)


def _inproj0(x, ctx, mod, g_pre, w_in, conv_w, conv_b, rope_c, rope_s1, rope_s2):
    bsz, s, d = x.shape
    n_lat = s // TM
    n_tiles = n_lat + 1
    t_all = n_tiles * TM
    w5 = conv_w.shape[-1]
    hb = TM // HALO

    def main_map(j, b):
        return (b, jnp.maximum(j - 1, 0), 0)

    def left_map(j, b):
        return (b, jnp.maximum((j - 1) * hb - 1, 0), 0)

    def right_map(j, b):
        return (b, jnp.minimum(jnp.maximum(j, 0) * hb, s // HALO - 1), 0)

    def mod_map(j, b):
        return (jnp.where(j == 0, bsz, b), 0, 0)

    const2 = lambda j, b: (0, 0)
    tile_out = pl.BlockSpec((1, TM, w5), lambda j, b: (b, j, 0))
    rope_spec = pl.BlockSpec((TM, LANES), lambda j, b: (j, 0))
    outs = pl.pallas_call(
        functools.partial(_inproj0_kernel, n_tiles),
        grid=(n_tiles, bsz),
        in_specs=[
            pl.BlockSpec((1, TM, d), main_map),
            pl.BlockSpec((1, HALO, d), left_map),
            pl.BlockSpec((1, HALO, d), right_map),
            pl.BlockSpec((1, TM, d), lambda j, b: (b, 0, 0)),
            pl.BlockSpec((1, 1, 3 * d), mod_map),
            pl.BlockSpec((1, d), const2),
            pl.BlockSpec(w_in.shape, const2),
            pl.BlockSpec(conv_w.shape, const2),
            pl.BlockSpec((1, w5), const2),
            rope_spec, rope_spec, rope_spec,
        ],
        out_specs=[tile_out] * 6,
        out_shape=[jax.ShapeDtypeStruct((bsz, t_all, w5), F32)]
        + [jax.ShapeDtypeStruct((bsz, t_all, w5), BF16)] * 5,
        scratch_shapes=[pltpu.VMEM((w5 // LANES, TM + 2 * HALO, LANES), F32)],
        compiler_params=_cparams(("arbitrary", "arbitrary")),
        name="inproj0",
    )(x, x, x, ctx, mod, g_pre.reshape(1, d), w_in, conv_w, conv_b.reshape(1, w5), rope_c, rope_s1, rope_s2)
    return outs


def _lru_kernel(bsz, xf_ref, xb_ref, wg_ref, bias_ref, sp_ref, of_ref, ob_ref, a_scr, b_scr, hs_scr, h_scr):
    i = pl.program_id(0)
    w5 = xf_ref.shape[-1]
    nslab = w5 // LANES
    tt = LRU_TT

    @pl.when(i == 0)
    def _():
        h_scr[...] = jnp.zeros_like(h_scr)

    for dr, x_ref in enumerate((xf_ref, xb_ref)):
        sp = sp_ref[dr:dr + 1, :]
        for b in range(bsz):
            xc = x_ref[b]
            g = jnp.dot(xc.astype(BF16), wg_ref[dr], preferred_element_type=F32)
            r = _sigmoid(g[:, 0:w5] + bias_ref[2 * dr:2 * dr + 1, :])
            ig = _sigmoid(g[:, w5:2 * w5] + bias_ref[2 * dr + 1:2 * dr + 2, :])
            log_a = (-LRU_C) * r * sp
            a = jnp.exp(log_a)
            bc = jnp.sqrt(-jnp.tanh(log_a) * (a * a + 1.0)) * (ig * xc)
            for c in range(nslab):
                a_scr[dr, c, pl.ds(b, tt, stride=bsz), :] = a[:, c * LANES:(c + 1) * LANES]
                b_scr[dr, c, pl.ds(b, tt, stride=bsz), :] = bc[:, c * LANES:(c + 1) * LANES]

    def step(t, carry):
        hf, hb = carry
        rf = pl.multiple_of(t * bsz, bsz)
        rb = pl.multiple_of((tt - 1 - t) * bsz, bsz)
        nf, nb = [], []
        for c in range(nslab):
            h = a_scr[0, c, pl.ds(rf, bsz), :] * hf[c] + b_scr[0, c, pl.ds(rf, bsz), :]
            hs_scr[0, c, pl.ds(rf, bsz), :] = h
            nf.append(h)
            h = a_scr[1, c, pl.ds(rb, bsz), :] * hb[c] + b_scr[1, c, pl.ds(rb, bsz), :]
            hs_scr[1, c, pl.ds(rb, bsz), :] = h
            nb.append(h)
        return tuple(nf), tuple(nb)

    h0f = tuple(h_scr[0, c] for c in range(nslab))
    h0b = tuple(h_scr[1, c] for c in range(nslab))
    hf, hb = lax.fori_loop(0, tt, step, (h0f, h0b), unroll=8)
    for c in range(nslab):
        h_scr[0, c] = hf[c]
        h_scr[1, c] = hb[c]

    for dr, o_ref in enumerate((of_ref, ob_ref)):
        for b in range(bsz):
            for c in range(nslab):
                o_ref[b, :, c * LANES:(c + 1) * LANES] = hs_scr[dr, c, pl.ds(b, tt, stride=bsz), :].astype(BF16)


def _bwd_tile(i, n_ctx, n_all):
    return jnp.where(i < n_ctx, n_ctx - 1 - i, n_all - 1 - (i - n_ctx))


def _lru(xc, wg, bias, sp, ctx_len):
    bsz, t_all, w5 = xc.shape
    tt = LRU_TT
    n_all = t_all // tt
    n_ctx = ctx_len // tt
    nslab = w5 // LANES
    fwd_spec = pl.BlockSpec((bsz, tt, w5), lambda i: (0, i, 0))
    bwd_spec = pl.BlockSpec((bsz, tt, w5), lambda i: (0, _bwd_tile(i, n_ctx, n_all), 0))
    const = lambda shape: pl.BlockSpec(shape, lambda i: (0,) * len(shape))
    coef = pltpu.VMEM((2, nslab, tt * bsz, LANES), F32)
    return pl.pallas_call(
        functools.partial(_lru_kernel, bsz),
        grid=(n_all,),
        in_specs=[fwd_spec, bwd_spec, const(wg.shape), const(bias.shape), const(sp.shape)],
        out_specs=[fwd_spec, bwd_spec],
        out_shape=[jax.ShapeDtypeStruct((bsz, t_all, w5), BF16)] * 2,
        scratch_shapes=[coef, coef, coef, pltpu.VMEM((2, nslab, bsz, LANES), F32)],
        compiler_params=_cparams(("arbitrary",)),
        name="rglru",
    )(xc, xc, wg, bias, sp)


def _attn_kernel(lambda_init, n_lat_chunks, q_ref, k_ref, v_ref, lam_ref, sub_ref, o_ref, m_scr, acc_scr):
    j = pl.program_id(2)
    q = q_ref[0]
    lane = lax.broadcasted_iota(jnp.int32, q.shape, 1)
    zero = jnp.zeros_like(q)
    qq = jnp.concatenate([jnp.where(lane < DA_HEAD_DIM, q, zero), jnp.where(lane >= DA_HEAD_DIM, q, zero)], axis=0)

    m_scr[...] = jnp.full_like(m_scr, -jnp.inf)
    acc_scr[...] = jnp.zeros_like(acc_scr)

    def chunk(start, size):
        kc = k_ref[0, pl.ds(start, size), :]
        va = jnp.concatenate([v_ref[0, pl.ds(start, size), :], jnp.ones((size, LANES), BF16)], axis=1)
        s = lax.dot_general(qq, kc, (((1,), (1,)), ((), ())), preferred_element_type=F32)
        m_prev = m_scr[...]
        m_next = jnp.maximum(m_prev, jnp.max(s, axis=1, keepdims=True))
        p = jnp.exp2(s - jnp.concatenate([m_next] * (size // LANES), axis=1))
        alpha = jnp.exp2(m_prev - m_next)
        acc_scr[...] = (acc_scr[...] * jnp.concatenate([alpha, alpha], axis=1)
                        + jnp.dot(p.astype(BF16), va, preferred_element_type=F32))
        m_scr[...] = m_next

    chunk(0, TM)

    @pl.when(j > 0)
    def _():
        def body(c, carry):
            chunk(pl.multiple_of(TM + c * KV_CHUNK, TM), KV_CHUNK)
            return carry
        lax.fori_loop(0, n_lat_chunks, body, 0, unroll=True)

    lm = lam_ref[...]
    lam = (jnp.exp(jnp.sum(lm[0:1] * lm[1:2], axis=1, keepdims=True))
           - jnp.exp(jnp.sum(lm[2:3] * lm[3:4], axis=1, keepdims=True)) + lambda_init)
    o = (acc_scr[0:TM, 0:LANES] / acc_scr[0:TM, LANES:2 * LANES]
         - lam * (acc_scr[TM:2 * TM, 0:LANES] / acc_scr[TM:2 * TM, LANES:2 * LANES]))
    o_ref[0] = ((_rms(o) * sub_ref[...]) * (1.0 - lambda_init)).astype(BF16)


def _attention(q, k, v, da_lambda, da_subln, lambda_init):
    bsz, t_all, _ = q.shape
    n_tiles = t_all // TM
    n_lat_chunks = (t_all - TM) // KV_CHUNK
    hd = DA_V_DIM
    q_spec = pl.BlockSpec((1, TM, hd), lambda b, h, j: (b, j, h))
    kv_spec = pl.BlockSpec((1, t_all, hd), lambda b, h, j: (b, 0, h))
    return pl.pallas_call(
        functools.partial(_attn_kernel, lambda_init, n_lat_chunks),
        grid=(bsz, DA_HEADS, n_tiles),
        in_specs=[q_spec, kv_spec, kv_spec,
                  pl.BlockSpec(da_lambda.shape, lambda b, h, j: (0, 0)),
                  pl.BlockSpec((1, hd), lambda b, h, j: (0, 0))],
        out_specs=q_spec,
        out_shape=jax.ShapeDtypeStruct(q.shape, BF16),
        scratch_shapes=[pltpu.VMEM((2 * TM, LANES), F32), pltpu.VMEM((2 * TM, 2 * LANES), F32)],
        compiler_params=_cparams(("arbitrary", "arbitrary", "arbitrary")),
        name="diffattn",
    )(q, k, v, da_lambda, da_subln.reshape(1, hd))


def _outproj0_kernel(hf_ref, hb_ref, gr_ref, on_ref, gd_ref, xm_ref, ctx_ref, mod_ref, g_ref, w_ref, o_ref):
    j = pl.program_id(0)
    d = o_ref.shape[-1]
    w5 = hf_ref.shape[-1]
    r = hf_ref[0].astype(F32) + hb_ref[0].astype(F32)
    m1 = (r * _silu(gr_ref[0].astype(F32))).astype(BF16)
    m2 = (on_ref[0].astype(F32) * _silu(gd_ref[0].astype(F32))).astype(BF16)
    y = jnp.dot(m1, w_ref[0:w5, :], preferred_element_type=F32)
    y = y + jnp.dot(m2, w_ref[w5:2 * w5, :], preferred_element_type=F32)
    gt = mod_ref[0, :, 2 * d:3 * d]
    h = jnp.where(j == 0, ctx_ref[0], xm_ref[0])
    o_ref[0] = h + gt * (_rms(y) * g_ref[...])


def _outproj0(hf, hb, gr, on, gd, x, ctx, mod, g_post, w_out):
    bsz, t_all, w5 = hf.shape
    d = x.shape[-1]
    n_tiles = t_all // TM
    tile = pl.BlockSpec((1, TM, w5), lambda j, b: (b, j, 0))
    const2 = lambda j, b: (0, 0)
    return pl.pallas_call(
        _outproj0_kernel,
        grid=(n_tiles, bsz),
        in_specs=[tile] * 5 + [
            pl.BlockSpec((1, TM, d), lambda j, b: (b, jnp.maximum(j - 1, 0), 0)),
            pl.BlockSpec((1, TM, d), lambda j, b: (b, 0, 0)),
            pl.BlockSpec((1, 1, 3 * d), lambda j, b: (jnp.where(j == 0, bsz, b), 0, 0)),
            pl.BlockSpec((1, d), const2),
            pl.BlockSpec(w_out.shape, const2),
        ],
        out_specs=pl.BlockSpec((1, TM, d), lambda j, b: (b, j, 0)),
        out_shape=jax.ShapeDtypeStruct((bsz, t_all, d), F32),
        compiler_params=_cparams(("arbitrary", "arbitrary")),
        name="outproj0",
    )(hf, hb, gr, on, gd, x, ctx, mod, g_post.reshape(1, d), w_out)


def _inproj1_kernel(n_tiles, hm_ref, hl_ref, hr_ref, mod_ref, g_ref, wz_ref, wx_ref, wd_ref, cw_ref, cb_ref, db_ref,
                    z_ref, xbc_ref, dt_ref, ext_ref):
    j = pl.program_id(0)
    d = hm_ref.shape[-1]
    h_ext = jnp.concatenate([hl_ref[0], hm_ref[0], hr_ref[0]], axis=0)
    u_ext = _normed_rows(h_ext, mod_ref, g_ref, d)
    u = u_ext[HALO:HALO + TM]
    left_ok = j >= 2
    right_ok = (j >= 1) & (j <= n_tiles - 2)
    nz, nx = wz_ref.shape[1], wx_ref.shape[1]
    cw = INPROJ_COLS
    spc = cw // LANES
    n_x, n_z = nx // cw, nz // cw
    for i in range(max(n_x, n_z)):
        if i < n_x:
            x_ext = jnp.dot(u_ext, wx_ref[:, i * cw:(i + 1) * cw], preferred_element_type=F32)
            _store_ext(ext_ref, i * spc, x_ext, left_ok, right_ok)
            for cs in range(i * spc, (i + 1) * spc):
                xbc_ref[0, :, cs * LANES:(cs + 1) * LANES] = _silu(_conv4(ext_ref, cs, cw_ref, cb_ref)).astype(BF16)
        if i < n_z:
            z_ref[0, :, i * cw:(i + 1) * cw] = jnp.dot(u, wz_ref[:, i * cw:(i + 1) * cw],
                                                       preferred_element_type=F32).astype(BF16)
    dt_ref[0] = jax.nn.softplus(jnp.dot(u, wd_ref[...], preferred_element_type=F32) + db_ref[...])


def _inproj1(h, mod, g_pre, wz, wx, wd, conv_w, conv_b, dt_bias, n_mod_rows):
    bsz, t_all, d = h.shape
    n_tiles = t_all // TM
    hb = TM // HALO
    nz, nx, nd = wz.shape[1], wx.shape[1], wd.shape[1]
    const2 = lambda j, b: (0, 0)
    return pl.pallas_call(
        functools.partial(_inproj1_kernel, n_tiles),
        grid=(n_tiles, bsz),
        in_specs=[
            pl.BlockSpec((1, TM, d), lambda j, b: (b, j, 0)),
            pl.BlockSpec((1, HALO, d), lambda j, b: (b, jnp.maximum(j * hb - 1, 0), 0)),
            pl.BlockSpec((1, HALO, d), lambda j, b: (b, jnp.minimum((j + 1) * hb, t_all // HALO - 1), 0)),
            pl.BlockSpec((1, 1, 3 * d), lambda j, b: (jnp.where(j == 0, n_mod_rows, b), 0, 0)),
            pl.BlockSpec((1, d), const2),
            pl.BlockSpec(wz.shape, const2),
            pl.BlockSpec(wx.shape, const2),
            pl.BlockSpec(wd.shape, const2),
            pl.BlockSpec(conv_w.shape, const2),
            pl.BlockSpec((1, nx), const2),
            pl.BlockSpec((1, nd), const2),
        ],
        out_specs=[pl.BlockSpec((1, TM, nz), lambda j, b: (b, j, 0)),
                   pl.BlockSpec((1, TM, nx), lambda j, b: (b, j, 0)),
                   pl.BlockSpec((1, TM, nd), lambda j, b: (b, j, 0))],
        out_shape=[jax.ShapeDtypeStruct((bsz, t_all, nz), BF16),
                   jax.ShapeDtypeStruct((bsz, t_all, nx), BF16),
                   jax.ShapeDtypeStruct((bsz, t_all, nd), F32)],
        scratch_shapes=[pltpu.VMEM((nx // LANES, TM + 2 * HALO, LANES), F32)],
        compiler_params=_cparams(("arbitrary", "arbitrary")),
        name="inproj1",
    )(h, h, h, mod, g_pre.reshape(1, d), wz, wx, wd, conv_w, conv_b.reshape(1, nx), dt_bias)


def _ssd_direction(reverse, dr, xbc_ref, dt_ref, alog_ref, s_scr, y_ref):
    ch = SSD_CHUNK
    inner = y_ref.shape[-1]
    n_heads = inner // SSD_HEAD_DIM
    gn = SSD_GROUPS * SSD_STATE
    pairs_per_group = n_heads // SSD_GROUPS // 2
    h0 = dr * n_heads

    dt = dt_ref[0]
    adt = dt * (-jnp.exp(alog_ref[dr:dr + 1, :]) * LOG2E)
    row = lax.broadcasted_iota(jnp.int32, (ch, ch), 0)
    col = lax.broadcasted_iota(jnp.int32, (ch, ch), 1)
    mask = (row <= col) if reverse else (row >= col)
    cs = jnp.dot(mask.astype(F32), adt, preferred_element_type=F32, precision=HIGHEST)
    last = 0 if reverse else ch - 1
    cs_t = cs.T
    dt_t = dt.T
    w_t = jnp.exp2(cs_t[:, last:last + 1] - cs_t) * dt_t
    crow = cs_t - jnp.log2(dt_t)
    e_tot = jnp.exp2(cs[last:last + 1, :])
    lane = lax.broadcasted_iota(jnp.int32, (ch, LANES), 1)
    left = lane < SSD_HEAD_DIM

    for g in range(SSD_GROUPS):
        bg = xbc_ref[0, :, inner + g * SSD_STATE:inner + (g + 1) * SSD_STATE]
        cg = xbc_ref[0, :, inner + gn + g * SSD_STATE:inner + gn + (g + 1) * SSD_STATE]
        cb = lax.dot_general(cg, bg, (((1,), (1,)), ((), ())), preferred_element_type=F32)
        bg_t = bg.astype(F32).T
        s_g = s_scr[dr, g]
        y_off = jnp.dot(cg, s_g.astype(BF16), preferred_element_type=F32)
        for pp in range(pairs_per_group):
            p = g * pairs_per_group + pp
            ls = slice(pp * LANES, (pp + 1) * LANES)
            x2 = xbc_ref[0, :, p * LANES:(p + 1) * LANES]
            zero = jnp.zeros_like(x2)
            wx = jnp.concatenate([jnp.where(left, x2, zero), jnp.where(left, zero, x2)], axis=0)
            ms, bws, cols = [], [], []
            for h in (h0 + 2 * p, h0 + 2 * p + 1):
                ccol = jnp.broadcast_to(cs[:, h:h + 1], (ch, ch))
                cols.append(ccol)
                ms.append((cb * jnp.exp2(jnp.where(mask, ccol - crow[h:h + 1, :], -jnp.inf))).astype(BF16))
                bws.append((bg_t * w_t[h:h + 1, :]).astype(BF16))
            lhs = jnp.concatenate([jnp.concatenate(ms, axis=1), jnp.concatenate(bws, axis=1)], axis=0)
            r = jnp.dot(lhs, wx, preferred_element_type=F32)
            y = r[0:ch] + jnp.exp2(jnp.where(left, cols[0], cols[1])) * y_off[:, ls]
            y_ref[0, :, p * LANES:(p + 1) * LANES] = y.astype(BF16)
            h1 = h0 + 2 * p
            dec = jnp.where(left[0:1, :], e_tot[:, h1:h1 + 1], e_tot[:, h1 + 1:h1 + 2])
            s_scr[dr, g, :, ls] = s_g[:, ls] * dec + r[ch:2 * ch]


def _ssd_kernel(xf_ref, df_ref, xb_ref, db_ref, alog_ref, yf_ref, yb_ref, s_scr):
    @pl.when(pl.program_id(1) == 0)
    def _():
        s_scr[...] = jnp.zeros_like(s_scr)

    _ssd_direction(False, 0, xf_ref, df_ref, alog_ref, s_scr, yf_ref)
    _ssd_direction(True, 1, xb_ref, db_ref, alog_ref, s_scr, yb_ref)


def _ssd(xbc, dt, a_log, ctx_len):
    bsz, t_all, nx = xbc.shape
    nd = dt.shape[-1]
    n_heads = a_log.shape[-1]
    inner = n_heads * SSD_HEAD_DIM
    a_log = jnp.stack([jnp.pad(a_log[dr], (dr * n_heads, nd - (dr + 1) * n_heads)) for dr in range(2)])
    ch = SSD_CHUNK
    n_all = t_all // ch
    n_ctx = ctx_len // ch
    fwd = lambda b, i: (b, i, 0)
    bwd = lambda b, i: (b, _bwd_tile(i, n_ctx, n_all), 0)
    return pl.pallas_call(
        _ssd_kernel,
        grid=(bsz, n_all),
        in_specs=[pl.BlockSpec((1, ch, nx), fwd), pl.BlockSpec((1, ch, nd), fwd),
                  pl.BlockSpec((1, ch, nx), bwd), pl.BlockSpec((1, ch, nd), bwd),
                  pl.BlockSpec(a_log.shape, lambda b, i: (0, 0))],
        out_specs=[pl.BlockSpec((1, ch, inner), fwd), pl.BlockSpec((1, ch, inner), bwd)],
        out_shape=[jax.ShapeDtypeStruct((bsz, t_all, inner), BF16)] * 2,
        scratch_shapes=[pltpu.VMEM((2, SSD_GROUPS, SSD_STATE, inner // SSD_GROUPS), F32)],
        compiler_params=_cparams(("arbitrary", "arbitrary")),
        name="ssd",
    )(xbc, dt, xbc, dt, a_log)


def _finish1_kernel(yf_ref, yb_ref, xs_ref, z_ref, h_ref, mod_ref, dsk_ref, nw_ref, w_ref, g_ref, o_ref):
    d = o_ref.shape[-1]
    inner = yf_ref.shape[-1]
    gw = inner // SSD_GROUPS
    y = yf_ref[0].astype(F32) + yb_ref[0].astype(F32) + dsk_ref[...] * xs_ref[0].astype(F32)
    y = y * _silu(z_ref[0].astype(F32))
    parts = []
    for g in range(SSD_GROUPS):
        yg = y[:, g * gw:(g + 1) * gw]
        parts.append((_rms(yg) * nw_ref[:, g * gw:(g + 1) * gw]).astype(BF16))
    out = jnp.dot(jnp.concatenate(parts, axis=1), w_ref[...], preferred_element_type=F32)
    gt = mod_ref[0, :, 2 * d:3 * d]
    o_ref[0] = h_ref[0] + gt * (_rms(out) * g_ref[...])


def _finish1(yf, yb, xbc, z, h, mod, d_skip, norm_w, w_out, g_post, ctx_len):
    bsz, t_all, inner = yf.shape
    d = h.shape[-1]
    off = ctx_len // TM
    n_lat = t_all // TM - off
    lat = lambda j, b: (b, j + off, 0)
    const2 = lambda j, b: (0, 0)
    return pl.pallas_call(
        _finish1_kernel,
        grid=(n_lat, bsz),
        in_specs=[pl.BlockSpec((1, TM, inner), lat)] * 4 + [
            pl.BlockSpec((1, TM, d), lat),
            pl.BlockSpec((1, 1, 3 * d), lambda j, b: (b, 0, 0)),
            pl.BlockSpec((1, inner), const2),
            pl.BlockSpec((1, inner), const2),
            pl.BlockSpec(w_out.shape, const2),
            pl.BlockSpec((1, d), const2),
        ],
        out_specs=pl.BlockSpec((1, TM, d), lambda j, b: (b, j, 0)),
        out_shape=jax.ShapeDtypeStruct((bsz, n_lat * TM, d), F32),
        compiler_params=_cparams(("arbitrary", "arbitrary")),
        name="finish1",
    )(yf, yb, xbc, z, h, mod, d_skip, norm_w, w_out, g_post.reshape(1, d))


def _rope_tables(n_tokens, ctx_len):
    rows = n_tokens // GRID_W
    row = jnp.repeat(jnp.arange(rows, dtype=F32), GRID_W)
    col = jnp.tile(jnp.arange(GRID_W, dtype=F32), rows)
    n_freq = DA_HEAD_DIM // 4
    inv = ROPE_BASE ** (-jnp.arange(n_freq, dtype=F32) / n_freq)
    ang = jnp.concatenate([row[:, None] * inv, col[:, None] * inv], axis=-1)
    cos, sin = jnp.cos(ang), jnp.sin(ang)
    cr, cc, sr, sc = cos[:, :n_freq], cos[:, n_freq:], sin[:, :n_freq], sin[:, n_freq:]
    zr = jnp.zeros_like(sr)
    c64 = jnp.concatenate([cr, cr, cc, cc], axis=1)
    s1 = jnp.concatenate([-sr, zr, -sc, zr], axis=1)
    s2 = jnp.concatenate([zr, sr, zr, sc], axis=1)
    reps = LANES // DA_HEAD_DIM
    pad = lambda t, v: jnp.concatenate([jnp.full((ctx_len, LANES), v, F32), jnp.tile(t, (1, reps))], axis=0)
    return pad(c64, 1.0), pad(s1, 0.0), pad(s2, 0.0)


def _block_diag(w):
    n, c, d = w.shape
    eye = jnp.eye(n, dtype=w.dtype)
    return (eye[:, None, :, None] * w[:, :, None, :]).reshape(n * c, n * d)


def kernel(x, c, ctx, c_ctx, w_mod, b_mod, g_pre, g_post, e_w_in, e_w_out, lru_conv_w, lru_conv_b, lru_w_r, lru_b_r, lru_w_i, lru_b_i, lru_lambda, da_lambda, da_subln, o_w_in, o_w_out, ssd_conv_w, ssd_conv_b, ssd_a_log, ssd_dt_bias, ssd_d, ssd_norm):
    bsz, s, d = x.shape
    ctx_len = ctx.shape[1]
    assert bsz == SUBLANES and ctx_len == TM and s % KV_CHUNK == 0 and w_mod.shape[0] == 2
    assert e_w_in.shape[0] == 1 and o_w_in.shape[0] == 1

    n_rows = 2 * SUBLANES
    c_rows = jnp.concatenate([c, c_ctx[None, :], jnp.zeros((n_rows - bsz - 1, d), F32)], axis=0)
    mod = _modulation(c_rows, w_mod, b_mod)
    mod0 = mod[0].reshape(n_rows, 1, 3 * d)
    mod1 = mod[1].reshape(n_rows, 1, 3 * d)

    w5 = lru_conv_w.shape[-1]
    w_in0 = e_w_in[0].astype(BF16)
    rope_c, rope_s1, rope_s2 = _rope_tables(s, ctx_len)
    xc, gr, q, k, v, gd = _inproj0(x, ctx, mod0, g_pre[0], w_in0, lru_conv_w[0], lru_conv_b[0],
                                   rope_c, rope_s1, rope_s2)

    wg = jnp.stack([jnp.concatenate([_block_diag(lru_w_r[0, dr]), _block_diag(lru_w_i[0, dr])], axis=1)
                    for dr in range(2)]).astype(BF16)
    bias = jnp.stack([lru_b_r[0, 0], lru_b_i[0, 0], lru_b_r[0, 1], lru_b_i[0, 1]])
    sp = jax.nn.softplus(-lru_lambda[0])
    hf, hb = _lru(xc, wg, bias, sp, ctx_len)

    lambda_init = 0.8 - 0.6 * math.exp(-0.3 * 0)
    on = _attention(q, k, v, da_lambda[0], da_subln[0], lambda_init)
    h1 = _outproj0(hf, hb, gr, on, gd, x, ctx, mod0, g_post[0], e_w_out[0].astype(BF16))

    n_heads = ssd_a_log.shape[-1]
    inner = n_heads * SSD_HEAD_DIM
    nx = ssd_conv_w.shape[-1]
    w1 = o_w_in[0]
    wz = w1[:, :inner].astype(BF16)
    wx = w1[:, inner:inner + nx].astype(BF16)
    nd = 2 * n_heads
    wd = jnp.pad(w1[:, inner + nx:], ((0, 0), (0, LANES - nd))).astype(BF16)
    dt_bias = jnp.pad(ssd_dt_bias[0].reshape(1, nd), ((0, 0), (0, LANES - nd)))
    z, xbc, dt = _inproj1(h1, mod1, g_pre[1], wz, wx, wd, ssd_conv_w[0], ssd_conv_b[0], dt_bias, bsz)
    yf, yb = _ssd(xbc, dt, ssd_a_log[0], ctx_len)
    d_skip = jnp.repeat(ssd_d[0], SSD_HEAD_DIM).reshape(1, inner)
    return _finish1(yf, yb, xbc, z, h1, mod1, d_skip, ssd_norm[0].reshape(1, inner),
                    o_w_out[0].astype(BF16), g_post[1], ctx_len)
```

```python
import functools
import math

import jax
import jax.numpy as jnp
from jax import lax
from jax.experimental import pallas as pl
from jax.experimental.pallas import tpu as pltpu

F32 = jnp.float32
BF16 = jnp.bfloat16

EPS = 1e-6
GRID_W = 64
ROPE_BASE = 10000.0
LRU_C = 8.0
LRU_BLOCKS = 8
DA_HEADS = 4
DA_HEAD_DIM = 64
DA_V_DIM = 128
SSD_HEAD_DIM = 64
SSD_STATE = 128
SSD_GROUPS = 4
SSD_CHUNK = 128

TM = 256
HALO = 8
LRU_TT = 128
KV_CHUNK = 512
INPROJ_COLS = 512
OUTPROJ_K = 256
LANES = 128
SUBLANES = 8
VMEM_LIMIT = 56 * 1024 * 1024

HIGHEST = lax.Precision.HIGHEST
LOG2E = math.log2(math.e)
Q_SCALE = DA_HEAD_DIM ** -0.5 * LOG2E


def _sigmoid(x):
    return 0.5 * (1.0 + jnp.tanh(0.5 * x))


def _silu(x):
    h = 0.5 * x
    return h * (1.0 + jnp.tanh(h))


def _rms(x):
    return x * lax.rsqrt(jnp.mean(x * x, axis=-1, keepdims=True) + EPS)


def _cparams(sem):
    return pltpu.CompilerParams(dimension_semantics=sem, vmem_limit_bytes=VMEM_LIMIT)


def _mod_kernel(c_ref, w_ref, b_ref, o_ref):
    s = _silu(c_ref[...])
    o_ref[0] = jnp.dot(s, w_ref[0], preferred_element_type=F32, precision=HIGHEST) + b_ref[0]


def _modulation(c_rows, w_mod, b_mod):
    depth, d, n3 = w_mod.shape
    rows = c_rows.shape[0]
    nt = 1024
    return pl.pallas_call(
        _mod_kernel,
        grid=(depth, n3 // nt),
        in_specs=[
            pl.BlockSpec((rows, d), lambda l, n: (0, 0)),
            pl.BlockSpec((1, d, nt), lambda l, n: (l, 0, n)),
            pl.BlockSpec((1, 1, nt), lambda l, n: (l, 0, n)),
        ],
        out_specs=pl.BlockSpec((1, rows, nt), lambda l, n: (l, 0, n)),
        out_shape=jax.ShapeDtypeStruct((depth, rows, n3), F32),
        compiler_params=_cparams(("arbitrary", "arbitrary")),
        name="modulation",
    )(c_rows, w_mod, b_mod.reshape(depth, 1, n3))


def _normed_rows(h_ext, mod_ref, g_ref, d):
    sh = mod_ref[0, :, 0:d]
    sc = mod_ref[0, :, d:2 * d]
    u = (_rms(h_ext) * g_ref[...]) * (1.0 + sc) + sh
    return u.astype(BF16)


def _store_ext(ext_ref, slab0, x_ext, left_ok, right_ok):
    n = x_ext.shape[1] // LANES
    zero = jnp.zeros((HALO, LANES), F32)
    for c in range(n):
        blk = x_ext[:, c * LANES:(c + 1) * LANES]
        ext_ref[slab0 + c, 0:HALO, :] = jnp.where(left_ok, blk[0:HALO], zero)
        ext_ref[slab0 + c, HALO:HALO + TM, :] = blk[HALO:HALO + TM]
        ext_ref[slab0 + c, HALO + TM:, :] = jnp.where(right_ok, blk[HALO + TM:], zero)


def _conv4(ext_ref, c, w_ref, b_ref):
    ls = slice(c * LANES, (c + 1) * LANES)
    acc = b_ref[:, ls] + w_ref[0:1, ls] * ext_ref[c, pl.ds(HALO - 2, TM), :]
    acc = acc + w_ref[1:2, ls] * ext_ref[c, pl.ds(HALO - 1, TM), :]
    acc = acc + w_ref[2:3, ls] * ext_ref[c, pl.ds(HALO, TM), :]
    acc = acc + w_ref[3:4, ls] * ext_ref[c, pl.ds(HALO + 1, TM), :]
    return acc


def _inproj0_kernel(n_tiles, xm_ref, xl_ref, xr_ref, ctx_ref, mod_ref, g_ref, w_ref, cw_ref, cb_ref,
                    rc_ref, rs1_ref, rs2_ref,
                    xc_ref, gr_ref, q_ref, k_ref, v_ref, gd_ref, ext_ref):
    j = pl.program_id(0)
    d = xm_ref.shape[-1]
    w5 = xc_ref.shape[-1]
    is_ctx = j == 0
    h_main = jnp.where(is_ctx, ctx_ref[0], xm_ref[0])
    h_ext = jnp.concatenate([xl_ref[0], h_main, xr_ref[0]], axis=0)
    u_ext = _normed_rows(h_ext, mod_ref, g_ref, d)
    u = u_ext[HALO:HALO + TM]

    x_ext = jnp.dot(u_ext, w_ref[:, 0:w5], preferred_element_type=F32)
    _store_ext(ext_ref, 0, x_ext, j >= 2, (j >= 1) & (j <= n_tiles - 2))
    for cs in range(w5 // LANES):
        xc_ref[0, :, cs * LANES:(cs + 1) * LANES] = _conv4(ext_ref, cs, cw_ref, cb_ref)

    gr_ref[0] = jnp.dot(u, w_ref[:, w5:2 * w5], preferred_element_type=F32).astype(BF16)
    reps = w5 // LANES
    c = jnp.concatenate([rc_ref[...]] * reps, axis=1)
    s1 = jnp.concatenate([rs1_ref[...]] * reps, axis=1)
    s2 = jnp.concatenate([rs2_ref[...]] * reps, axis=1)
    for idx, o_ref, post in ((2, q_ref, Q_SCALE), (3, k_ref, None)):
        t = jnp.dot(u, w_ref[:, idx * w5:(idx + 1) * w5], preferred_element_type=F32)
        t = t * c + pltpu.roll(t, w5 - 16, 1) * s1 + pltpu.roll(t, 16, 1) * s2
        o_ref[0] = (t if post is None else t * post).astype(BF16)
    v_ref[0] = jnp.dot(u, w_ref[:, 4 * w5:5 * w5], preferred_element_type=F32).astype(BF16)
    gd_ref[0] = jnp.dot(u, w_ref[:, 5 * w5:6 * w5], preferred_element_type=F32).astype(BF16)


def _inproj0(x, ctx, mod, g_pre, w_in, conv_w, conv_b, rope_c, rope_s1, rope_s2):
    bsz, s, d = x.shape
    n_lat = s // TM
    n_tiles = n_lat + 1
    t_all = n_tiles * TM
    w5 = conv_w.shape[-1]
    hb = TM // HALO

    def main_map(j, b):
        return (b, jnp.maximum(j - 1, 0), 0)

    def left_map(j, b):
        return (b, jnp.maximum((j - 1) * hb - 1, 0), 0)

    def right_map(j, b):
        return (b, jnp.minimum(jnp.maximum(j, 0) * hb, s // HALO - 1), 0)

    def mod_map(j, b):
        return (jnp.where(j == 0, bsz, b), 0, 0)

    const2 = lambda j, b: (0, 0)
    tile_out = pl.BlockSpec((1, TM, w5), lambda j, b: (b, j, 0))
    rope_spec = pl.BlockSpec((TM, LANES), lambda j, b: (j, 0))
    outs = pl.pallas_call(
        functools.partial(_inproj0_kernel, n_tiles),
        grid=(n_tiles, bsz),
        in_specs=[
            pl.BlockSpec((1, TM, d), main_map),
            pl.BlockSpec((1, HALO, d), left_map),
            pl.BlockSpec((1, HALO, d), right_map),
            pl.BlockSpec((1, TM, d), lambda j, b: (b, 0, 0)),
            pl.BlockSpec((1, 1, 3 * d), mod_map),
            pl.BlockSpec((1, d), const2),
            pl.BlockSpec(w_in.shape, const2),
            pl.BlockSpec(conv_w.shape, const2),
            pl.BlockSpec((1, w5), const2),
            rope_spec, rope_spec, rope_spec,
        ],
        out_specs=[tile_out] * 6,
        out_shape=[jax.ShapeDtypeStruct((bsz, t_all, w5), F32)]
        + [jax.ShapeDtypeStruct((bsz, t_all, w5), BF16)] * 5,
        scratch_shapes=[pltpu.VMEM((w5 // LANES, TM + 2 * HALO, LANES), F32)],
        compiler_params=_cparams(("arbitrary", "arbitrary")),
        name="inproj0",
    )(x, x, x, ctx, mod, g_pre.reshape(1, d), w_in, conv_w, conv_b.reshape(1, w5), rope_c, rope_s1, rope_s2)
    return outs


def _sqrt_pos(y):
    return jnp.where(y > 0.0, y * lax.rsqrt(y), 0.0)


def _lru_kernel(bsz, xf_ref, xb_ref, wg_ref, bias_ref, k4_ref, of_ref, ob_ref, g_scr, a_scr, b_scr, hs_scr, h_scr):
    i = pl.program_id(0)
    w5 = xf_ref.shape[-1]
    nslab = w5 // LANES
    tt = LRU_TT

    @pl.when(i == 0)
    def _():
        h_scr[...] = jnp.zeros_like(h_scr)

    for dr, x_ref in enumerate((xf_ref, xb_ref)):
        k4 = k4_ref[dr:dr + 1, :]
        g_scr[...] = jnp.dot(x_ref[...].reshape(bsz * tt, w5).astype(BF16), wg_ref[dr], preferred_element_type=F32)
        for b in range(bsz):
            xc = x_ref[b]
            tr = jnp.tanh(g_scr[b * tt:(b + 1) * tt, 0:w5] + bias_ref[2 * dr:2 * dr + 1, :])
            ti = jnp.tanh(g_scr[b * tt:(b + 1) * tt, w5:2 * w5] + bias_ref[2 * dr + 1:2 * dr + 2, :])
            nla = k4 * tr + k4
            a = jnp.exp2(nla * (-LOG2E))
            bc = _sqrt_pos(jnp.tanh(nla) * (a * a + 1.0)) * ((0.5 * xc) * (1.0 + ti))
            for c in range(nslab):
                a_scr[dr, c, pl.ds(b, tt, stride=bsz), :] = a[:, c * LANES:(c + 1) * LANES]
                b_scr[dr, c, pl.ds(b, tt, stride=bsz), :] = bc[:, c * LANES:(c + 1) * LANES]

    def step(t, carry):
        hf, hb = carry
        rf = pl.multiple_of(t * bsz, bsz)
        rb = pl.multiple_of((tt - 1 - t) * bsz, bsz)
        nf, nb = [], []
        for c in range(nslab):
            h = a_scr[0, c, pl.ds(rf, bsz), :] * hf[c] + b_scr[0, c, pl.ds(rf, bsz), :]
            hs_scr[0, c, pl.ds(rf, bsz), :] = h
            nf.append(h)
            h = a_scr[1, c, pl.ds(rb, bsz), :] * hb[c] + b_scr[1, c, pl.ds(rb, bsz), :]
            hs_scr[1, c, pl.ds(rb, bsz), :] = h
            nb.append(h)
        return tuple(nf), tuple(nb)

    h0f = tuple(h_scr[0, c] for c in range(nslab))
    h0b = tuple(h_scr[1, c] for c in range(nslab))
    hf, hb = lax.fori_loop(0, tt, step, (h0f, h0b), unroll=8)
    for c in range(nslab):
        h_scr[0, c] = hf[c]
        h_scr[1, c] = hb[c]

    for dr, o_ref in enumerate((of_ref, ob_ref)):
        for b in range(bsz):
            for c in range(nslab):
                o_ref[b, :, c * LANES:(c + 1) * LANES] = hs_scr[dr, c, pl.ds(b, tt, stride=bsz), :].astype(BF16)


def _bwd_tile(i, n_ctx, n_all):
    return jnp.where(i < n_ctx, n_ctx - 1 - i, n_all - 1 - (i - n_ctx))


def _lru(xc, wg, bias, k4, ctx_len):
    bsz, t_all, w5 = xc.shape
    tt = LRU_TT
    n_all = t_all // tt
    n_ctx = ctx_len // tt
    nslab = w5 // LANES
    fwd_spec = pl.BlockSpec((bsz, tt, w5), lambda i: (0, i, 0))
    bwd_spec = pl.BlockSpec((bsz, tt, w5), lambda i: (0, _bwd_tile(i, n_ctx, n_all), 0))
    const = lambda shape: pl.BlockSpec(shape, lambda i: (0,) * len(shape))
    coef = pltpu.VMEM((2, nslab, tt * bsz, LANES), F32)
    return pl.pallas_call(
        functools.partial(_lru_kernel, bsz),
        grid=(n_all,),
        in_specs=[fwd_spec, bwd_spec, const(wg.shape), const(bias.shape), const(k4.shape)],
        out_specs=[fwd_spec, bwd_spec],
        out_shape=[jax.ShapeDtypeStruct((bsz, t_all, w5), BF16)] * 2,
        scratch_shapes=[pltpu.VMEM((bsz * tt, 2 * w5), F32), coef, coef, coef,
                        pltpu.VMEM((2, nslab, bsz, LANES), F32)],
        compiler_params=_cparams(("arbitrary",)),
        name="rglru",
    )(xc, xc, wg, bias, k4)


def _attn_kernel(lambda_init, n_lat_chunks, q_ref, k_ref, v_ref, lam_ref, sub_ref, o_ref, m_scr, acc_scr):
    j = pl.program_id(2)
    q = q_ref[0]
    lane = lax.broadcasted_iota(jnp.int32, q.shape, 1)
    zero = jnp.zeros_like(q)
    qq = jnp.concatenate([jnp.where(lane < DA_HEAD_DIM, q, zero), jnp.where(lane >= DA_HEAD_DIM, q, zero)], axis=0)

    m_scr[...] = jnp.full_like(m_scr, -jnp.inf)
    acc_scr[...] = jnp.zeros_like(acc_scr)

    def chunk(start, size):
        kc = k_ref[0, pl.ds(start, size), :]
        va = jnp.concatenate([v_ref[0, pl.ds(start, size), :], jnp.ones((size, LANES), BF16)], axis=1)
        s = lax.dot_general(qq, kc, (((1,), (1,)), ((), ())), preferred_element_type=F32)
        m_prev = m_scr[...]
        m_next = jnp.maximum(m_prev, jnp.max(s, axis=1, keepdims=True))
        p = jnp.exp2(s - jnp.concatenate([m_next] * (size // LANES), axis=1))
        alpha = jnp.exp2(m_prev - m_next)
        acc_scr[...] = (acc_scr[...] * jnp.concatenate([alpha, alpha], axis=1)
                        + jnp.dot(p.astype(BF16), va, preferred_element_type=F32))
        m_scr[...] = m_next

    chunk(0, TM)

    @pl.when(j > 0)
    def _():
        def body(c, carry):
            chunk(pl.multiple_of(TM + c * KV_CHUNK, TM), KV_CHUNK)
            return carry
        lax.fori_loop(0, n_lat_chunks, body, 0, unroll=True)

    lm = lam_ref[...]
    lam = (jnp.exp(jnp.sum(lm[0:1] * lm[1:2], axis=1, keepdims=True))
           - jnp.exp(jnp.sum(lm[2:3] * lm[3:4], axis=1, keepdims=True)) + lambda_init)
    o = (acc_scr[0:TM, 0:LANES] / acc_scr[0:TM, LANES:2 * LANES]
         - lam * (acc_scr[TM:2 * TM, 0:LANES] / acc_scr[TM:2 * TM, LANES:2 * LANES]))
    o_ref[0] = ((_rms(o) * sub_ref[...]) * (1.0 - lambda_init)).astype(BF16)


def _attention(q, k, v, da_lambda, da_subln, lambda_init):
    bsz, t_all, _ = q.shape
    n_tiles = t_all // TM
    n_lat_chunks = (t_all - TM) // KV_CHUNK
    hd = DA_V_DIM
    q_spec = pl.BlockSpec((1, TM, hd), lambda b, h, j: (b, j, h))
    kv_spec = pl.BlockSpec((1, t_all, hd), lambda b, h, j: (b, 0, h))
    return pl.pallas_call(
        functools.partial(_attn_kernel, lambda_init, n_lat_chunks),
        grid=(bsz, DA_HEADS, n_tiles),
        in_specs=[q_spec, kv_spec, kv_spec,
                  pl.BlockSpec(da_lambda.shape, lambda b, h, j: (0, 0)),
                  pl.BlockSpec((1, hd), lambda b, h, j: (0, 0))],
        out_specs=q_spec,
        out_shape=jax.ShapeDtypeStruct(q.shape, BF16),
        scratch_shapes=[pltpu.VMEM((2 * TM, LANES), F32), pltpu.VMEM((2 * TM, 2 * LANES), F32)],
        compiler_params=_cparams(("arbitrary", "arbitrary", "arbitrary")),
        name="diffattn",
    )(q, k, v, da_lambda, da_subln.reshape(1, hd))


def _outproj0_kernel(hf_ref, hb_ref, gr_ref, on_ref, gd_ref, xm_ref, ctx_ref, mod_ref, g_ref, w_ref, o_ref):
    j = pl.program_id(0)
    d = o_ref.shape[-1]
    w5 = hf_ref.shape[-1]
    kc = OUTPROJ_K
    y = None
    for c in range(w5 // kc):
        sl = slice(c * kc, (c + 1) * kc)
        r = hf_ref[0, :, sl].astype(F32) + hb_ref[0, :, sl].astype(F32)
        m1 = (r * _silu(gr_ref[0, :, sl].astype(F32))).astype(BF16)
        m2 = (on_ref[0, :, sl].astype(F32) * _silu(gd_ref[0, :, sl].astype(F32))).astype(BF16)
        t = jnp.dot(m1, w_ref[c * kc:(c + 1) * kc, :], preferred_element_type=F32)
        t = t + jnp.dot(m2, w_ref[w5 + c * kc:w5 + (c + 1) * kc, :], preferred_element_type=F32)
        y = t if y is None else y + t
    gt = mod_ref[0, :, 2 * d:3 * d]
    h = jnp.where(j == 0, ctx_ref[0], xm_ref[0])
    o_ref[0] = h + gt * (_rms(y) * g_ref[...])


def _outproj0(hf, hb, gr, on, gd, x, ctx, mod, g_post, w_out):
    bsz, t_all, w5 = hf.shape
    d = x.shape[-1]
    n_tiles = t_all // TM
    tile = pl.BlockSpec((1, TM, w5), lambda j, b: (b, j, 0))
    const2 = lambda j, b: (0, 0)
    return pl.pallas_call(
        _outproj0_kernel,
        grid=(n_tiles, bsz),
        in_specs=[tile] * 5 + [
            pl.BlockSpec((1, TM, d), lambda j, b: (b, jnp.maximum(j - 1, 0), 0)),
            pl.BlockSpec((1, TM, d), lambda j, b: (b, 0, 0)),
            pl.BlockSpec((1, 1, 3 * d), lambda j, b: (jnp.where(j == 0, bsz, b), 0, 0)),
            pl.BlockSpec((1, d), const2),
            pl.BlockSpec(w_out.shape, const2),
        ],
        out_specs=pl.BlockSpec((1, TM, d), lambda j, b: (b, j, 0)),
        out_shape=jax.ShapeDtypeStruct((bsz, t_all, d), F32),
        compiler_params=_cparams(("arbitrary", "arbitrary")),
        name="outproj0",
    )(hf, hb, gr, on, gd, x, ctx, mod, g_post.reshape(1, d), w_out)


def _inproj1_kernel(n_tiles, hm_ref, hl_ref, hr_ref, mod_ref, g_ref, wz_ref, wx_ref, wd_ref, cw_ref, cb_ref, db_ref,
                    z_ref, xbc_ref, dt_ref, ext_ref):
    j = pl.program_id(0)
    d = hm_ref.shape[-1]
    h_ext = jnp.concatenate([hl_ref[0], hm_ref[0], hr_ref[0]], axis=0)
    u_ext = _normed_rows(h_ext, mod_ref, g_ref, d)
    u = u_ext[HALO:HALO + TM]
    left_ok = j >= 2
    right_ok = (j >= 1) & (j <= n_tiles - 2)
    nz, nx = wz_ref.shape[1], wx_ref.shape[1]
    cw = INPROJ_COLS
    spc = cw // LANES
    n_x, n_z = nx // cw, nz // cw
    for i in range(max(n_x, n_z)):
        if i < n_x:
            x_ext = jnp.dot(u_ext, wx_ref[:, i * cw:(i + 1) * cw], preferred_element_type=F32)
            _store_ext(ext_ref, i * spc, x_ext, left_ok, right_ok)
            for cs in range(i * spc, (i + 1) * spc):
                xbc_ref[0, :, cs * LANES:(cs + 1) * LANES] = _silu(_conv4(ext_ref, cs, cw_ref, cb_ref)).astype(BF16)
        if i < n_z:
            z_ref[0, :, i * cw:(i + 1) * cw] = jnp.dot(u, wz_ref[:, i * cw:(i + 1) * cw],
                                                       preferred_element_type=F32).astype(BF16)
    dt_ref[0] = jax.nn.softplus(jnp.dot(u, wd_ref[...], preferred_element_type=F32) + db_ref[...])


def _inproj1(h, mod, g_pre, wz, wx, wd, conv_w, conv_b, dt_bias, n_mod_rows):
    bsz, t_all, d = h.shape
    n_tiles = t_all // TM
    hb = TM // HALO
    nz, nx, nd = wz.shape[1], wx.shape[1], wd.shape[1]
    const2 = lambda j, b: (0, 0)
    return pl.pallas_call(
        functools.partial(_inproj1_kernel, n_tiles),
        grid=(n_tiles, bsz),
        in_specs=[
            pl.BlockSpec((1, TM, d), lambda j, b: (b, j, 0)),
            pl.BlockSpec((1, HALO, d), lambda j, b: (b, jnp.maximum(j * hb - 1, 0), 0)),
            pl.BlockSpec((1, HALO, d), lambda j, b: (b, jnp.minimum((j + 1) * hb, t_all // HALO - 1), 0)),
            pl.BlockSpec((1, 1, 3 * d), lambda j, b: (jnp.where(j == 0, n_mod_rows, b), 0, 0)),
            pl.BlockSpec((1, d), const2),
            pl.BlockSpec(wz.shape, const2),
            pl.BlockSpec(wx.shape, const2),
            pl.BlockSpec(wd.shape, const2),
            pl.BlockSpec(conv_w.shape, const2),
            pl.BlockSpec((1, nx), const2),
            pl.BlockSpec((1, nd), const2),
        ],
        out_specs=[pl.BlockSpec((1, TM, nz), lambda j, b: (b, j, 0)),
                   pl.BlockSpec((1, TM, nx), lambda j, b: (b, j, 0)),
                   pl.BlockSpec((1, TM, nd), lambda j, b: (b, j, 0))],
        out_shape=[jax.ShapeDtypeStruct((bsz, t_all, nz), BF16),
                   jax.ShapeDtypeStruct((bsz, t_all, nx), BF16),
                   jax.ShapeDtypeStruct((bsz, t_all, nd), F32)],
        scratch_shapes=[pltpu.VMEM((nx // LANES, TM + 2 * HALO, LANES), F32)],
        compiler_params=_cparams(("arbitrary", "arbitrary")),
        name="inproj1",
    )(h, h, h, mod, g_pre.reshape(1, d), wz, wx, wd, conv_w, conv_b.reshape(1, nx), dt_bias)


def _ssd_prologue(reverse, dr, dt_ref, alog_ref):
    ch = SSD_CHUNK
    dt = dt_ref[0]
    adt = dt * (-jnp.exp(alog_ref[dr:dr + 1, :]) * LOG2E)
    row = lax.broadcasted_iota(jnp.int32, (ch, ch), 0)
    col = lax.broadcasted_iota(jnp.int32, (ch, ch), 1)
    mask = (row <= col) if reverse else (row >= col)
    cs = jnp.dot(mask.astype(F32), adt, preferred_element_type=F32, precision=HIGHEST)
    last = 0 if reverse else ch - 1
    cs_t = cs.T
    dt_t = dt.T
    w_t = jnp.exp2(cs_t[:, last:last + 1] - cs_t) * dt_t
    crow = cs_t - jnp.log2(dt_t)
    e_tot = jnp.exp2(cs[last:last + 1, :])
    return mask, cs, w_t, crow, e_tot


def _ssd_group(dr, g, xbc_ref, s_scr, inner):
    gn = SSD_GROUPS * SSD_STATE
    bg = xbc_ref[0, :, inner + g * SSD_STATE:inner + (g + 1) * SSD_STATE]
    cg = xbc_ref[0, :, inner + gn + g * SSD_STATE:inner + gn + (g + 1) * SSD_STATE]
    cb = lax.dot_general(cg, bg, (((1,), (1,)), ((), ())), preferred_element_type=F32)
    bg_t = bg.astype(F32).T
    s_g = s_scr[dr, g]
    y_off = jnp.dot(cg, s_g.astype(BF16), preferred_element_type=F32)
    return cb, bg_t, s_g, y_off


def _ssd_pair(dr, g, pp, n_heads, pro, grp, xbc_ref, s_scr, y_ref):
    ch = SSD_CHUNK
    mask, cs, w_t, crow, e_tot = pro
    cb, bg_t, s_g, y_off = grp
    pairs_per_group = n_heads // SSD_GROUPS // 2
    p = g * pairs_per_group + pp
    h1 = dr * n_heads + 2 * p
    ls = slice(pp * LANES, (pp + 1) * LANES)
    left = lax.broadcasted_iota(jnp.int32, (ch, LANES), 1) < SSD_HEAD_DIM
    x2 = xbc_ref[0, :, p * LANES:(p + 1) * LANES]
    zero = jnp.zeros_like(x2)
    wx = jnp.concatenate([jnp.where(left, x2, zero), jnp.where(left, zero, x2)], axis=0)
    ms, bws, cols = [], [], []
    for h in (h1, h1 + 1):
        ccol = jnp.broadcast_to(cs[:, h:h + 1], (ch, ch))
        cols.append(ccol)
        ms.append((cb * jnp.exp2(jnp.where(mask, ccol - crow[h:h + 1, :], -jnp.inf))).astype(BF16))
        bws.append((bg_t * w_t[h:h + 1, :]).astype(BF16))
    lhs = jnp.concatenate([jnp.concatenate(ms, axis=1), jnp.concatenate(bws, axis=1)], axis=0)
    r = jnp.dot(lhs, wx, preferred_element_type=F32)
    y = r[0:ch] + jnp.exp2(jnp.where(left, cols[0], cols[1])) * y_off[:, ls]
    y_ref[0, :, p * LANES:(p + 1) * LANES] = y.astype(BF16)
    dec = jnp.where(left[0:1, :], e_tot[:, h1:h1 + 1], e_tot[:, h1 + 1:h1 + 2])
    s_scr[dr, g, :, ls] = s_g[:, ls] * dec + r[ch:2 * ch]


def _ssd_kernel(xf_ref, df_ref, xb_ref, db_ref, alog_ref, yf_ref, yb_ref, s_scr):
    @pl.when(pl.program_id(1) == 0)
    def _():
        s_scr[...] = jnp.zeros_like(s_scr)

    inner = yf_ref.shape[-1]
    n_heads = inner // SSD_HEAD_DIM
    dirs = ((xf_ref, df_ref, yf_ref, False), (xb_ref, db_ref, yb_ref, True))
    pros = [_ssd_prologue(rev, dr, d_ref, alog_ref) for dr, (_, d_ref, _, rev) in enumerate(dirs)]
    for g in range(SSD_GROUPS):
        grps = [_ssd_group(dr, g, x_ref, s_scr, inner) for dr, (x_ref, _, _, _) in enumerate(dirs)]
        for pp in range(n_heads // SSD_GROUPS // 2):
            for dr, (x_ref, _, y_ref, _) in enumerate(dirs):
                _ssd_pair(dr, g, pp, n_heads, pros[dr], grps[dr], x_ref, s_scr, y_ref)


def _ssd(xbc, dt, a_log, ctx_len):
    bsz, t_all, nx = xbc.shape
    nd = dt.shape[-1]
    n_heads = a_log.shape[-1]
    inner = n_heads * SSD_HEAD_DIM
    a_log = jnp.stack([jnp.pad(a_log[dr], (dr * n_heads, nd - (dr + 1) * n_heads)) for dr in range(2)])
    ch = SSD_CHUNK
    n_all = t_all // ch
    n_ctx = ctx_len // ch
    fwd = lambda b, i: (b, i, 0)
    bwd = lambda b, i: (b, _bwd_tile(i, n_ctx, n_all), 0)
    return pl.pallas_call(
        _ssd_kernel,
        grid=(bsz, n_all),
        in_specs=[pl.BlockSpec((1, ch, nx), fwd), pl.BlockSpec((1, ch, nd), fwd),
                  pl.BlockSpec((1, ch, nx), bwd), pl.BlockSpec((1, ch, nd), bwd),
                  pl.BlockSpec(a_log.shape, lambda b, i: (0, 0))],
        out_specs=[pl.BlockSpec((1, ch, inner), fwd), pl.BlockSpec((1, ch, inner), bwd)],
        out_shape=[jax.ShapeDtypeStruct((bsz, t_all, inner), BF16)] * 2,
        scratch_shapes=[pltpu.VMEM((2, SSD_GROUPS, SSD_STATE, inner // SSD_GROUPS), F32)],
        compiler_params=_cparams(("arbitrary", "arbitrary")),
        name="ssd",
    )(xbc, dt, xbc, dt, a_log)


def _finish1_kernel(yf_ref, yb_ref, xs_ref, z_ref, h_ref, mod_ref, dsk_ref, nw_ref, w_ref, g_ref, o_ref):
    d = o_ref.shape[-1]
    inner = yf_ref.shape[-1]
    gw = inner // SSD_GROUPS
    out = None
    for g in range(SSD_GROUPS):
        sl = slice(g * gw, (g + 1) * gw)
        y = yf_ref[0, :, sl].astype(F32) + yb_ref[0, :, sl].astype(F32) + dsk_ref[:, sl] * xs_ref[0, :, sl].astype(F32)
        y = y * _silu(z_ref[0, :, sl].astype(F32))
        t = jnp.dot((_rms(y) * nw_ref[:, sl]).astype(BF16), w_ref[sl, :], preferred_element_type=F32)
        out = t if out is None else out + t
    gt = mod_ref[0, :, 2 * d:3 * d]
    o_ref[0] = h_ref[0] + gt * (_rms(out) * g_ref[...])


def _finish1(yf, yb, xbc, z, h, mod, d_skip, norm_w, w_out, g_post, ctx_len):
    bsz, t_all, inner = yf.shape
    d = h.shape[-1]
    off = ctx_len // TM
    n_lat = t_all // TM - off
    lat = lambda j, b: (b, j + off, 0)
    const2 = lambda j, b: (0, 0)
    return pl.pallas_call(
        _finish1_kernel,
        grid=(n_lat, bsz),
        in_specs=[pl.BlockSpec((1, TM, inner), lat)] * 4 + [
            pl.BlockSpec((1, TM, d), lat),
            pl.BlockSpec((1, 1, 3 * d), lambda j, b: (b, 0, 0)),
            pl.BlockSpec((1, inner), const2),
            pl.BlockSpec((1, inner), const2),
            pl.BlockSpec(w_out.shape, const2),
            pl.BlockSpec((1, d), const2),
        ],
        out_specs=pl.BlockSpec((1, TM, d), lambda j, b: (b, j, 0)),
        out_shape=jax.ShapeDtypeStruct((bsz, n_lat * TM, d), F32),
        compiler_params=_cparams(("arbitrary", "arbitrary")),
        name="finish1",
    )(yf, yb, xbc, z, h, mod, d_skip, norm_w, w_out, g_post.reshape(1, d))


def _rope_tables(n_tokens, ctx_len):
    rows = n_tokens // GRID_W
    row = jnp.repeat(jnp.arange(rows, dtype=F32), GRID_W)
    col = jnp.tile(jnp.arange(GRID_W, dtype=F32), rows)
    n_freq = DA_HEAD_DIM // 4
    inv = ROPE_BASE ** (-jnp.arange(n_freq, dtype=F32) / n_freq)
    ang = jnp.concatenate([row[:, None] * inv, col[:, None] * inv], axis=-1)
    cos, sin = jnp.cos(ang), jnp.sin(ang)
    cr, cc, sr, sc = cos[:, :n_freq], cos[:, n_freq:], sin[:, :n_freq], sin[:, n_freq:]
    zr = jnp.zeros_like(sr)
    c64 = jnp.concatenate([cr, cr, cc, cc], axis=1)
    s1 = jnp.concatenate([-sr, zr, -sc, zr], axis=1)
    s2 = jnp.concatenate([zr, sr, zr, sc], axis=1)
    reps = LANES // DA_HEAD_DIM
    pad = lambda t, v: jnp.concatenate([jnp.full((ctx_len, LANES), v, F32), jnp.tile(t, (1, reps))], axis=0)
    return pad(c64, 1.0), pad(s1, 0.0), pad(s2, 0.0)


def _block_diag(w):
    n, c, d = w.shape
    eye = jnp.eye(n, dtype=w.dtype)
    return (eye[:, None, :, None] * w[:, :, None, :]).reshape(n * c, n * d)


def kernel(x, c, ctx, c_ctx, w_mod, b_mod, g_pre, g_post, e_w_in, e_w_out, lru_conv_w, lru_conv_b, lru_w_r, lru_b_r, lru_w_i, lru_b_i, lru_lambda, da_lambda, da_subln, o_w_in, o_w_out, ssd_conv_w, ssd_conv_b, ssd_a_log, ssd_dt_bias, ssd_d, ssd_norm):
    bsz, s, d = x.shape
    ctx_len = ctx.shape[1]
    assert bsz == SUBLANES and ctx_len == TM and s % KV_CHUNK == 0 and w_mod.shape[0] == 2
    assert e_w_in.shape[0] == 1 and o_w_in.shape[0] == 1

    n_rows = 2 * SUBLANES
    c_rows = jnp.concatenate([c, c_ctx[None, :], jnp.zeros((n_rows - bsz - 1, d), F32)], axis=0)
    mod = _modulation(c_rows, w_mod, b_mod)
    mod0 = mod[0].reshape(n_rows, 1, 3 * d)
    mod1 = mod[1].reshape(n_rows, 1, 3 * d)

    w5 = lru_conv_w.shape[-1]
    w_in0 = e_w_in[0].astype(BF16)
    rope_c, rope_s1, rope_s2 = _rope_tables(s, ctx_len)
    xc, gr, q, k, v, gd = _inproj0(x, ctx, mod0, g_pre[0], w_in0, lru_conv_w[0], lru_conv_b[0],
                                   rope_c, rope_s1, rope_s2)

    wg = (0.5 * jnp.stack([jnp.concatenate([_block_diag(lru_w_r[0, dr]), _block_diag(lru_w_i[0, dr])], axis=1)
                           for dr in range(2)])).astype(BF16)
    bias = 0.5 * jnp.stack([lru_b_r[0, 0], lru_b_i[0, 0], lru_b_r[0, 1], lru_b_i[0, 1]])
    k4 = (0.5 * LRU_C) * jax.nn.softplus(-lru_lambda[0])
    hf, hb = _lru(xc, wg, bias, k4, ctx_len)

    lambda_init = 0.8 - 0.6 * math.exp(-0.3 * 0)
    on = _attention(q, k, v, da_lambda[0], da_subln[0], lambda_init)
    h1 = _outproj0(hf, hb, gr, on, gd, x, ctx, mod0, g_post[0], e_w_out[0].astype(BF16))

    n_heads = ssd_a_log.shape[-1]
    inner = n_heads * SSD_HEAD_DIM
    nx = ssd_conv_w.shape[-1]
    w1 = o_w_in[0]
    wz = w1[:, :inner].astype(BF16)
    wx = w1[:, inner:inner + nx].astype(BF16)
    nd = 2 * n_heads
    wd = jnp.pad(w1[:, inner + nx:], ((0, 0), (0, LANES - nd))).astype(BF16)
    dt_bias = jnp.pad(ssd_dt_bias[0].reshape(1, nd), ((0, 0), (0, LANES - nd)))
    z, xbc, dt = _inproj1(h1, mod1, g_pre[1], wz, wx, wd, ssd_conv_w[0], ssd_conv_b[0], dt_bias, bsz)
    yf, yb = _ssd(xbc, dt, ssd_a_log[0], ctx_len)
    d_skip = jnp.repeat(ssd_d[0], SSD_HEAD_DIM).reshape(1, inner)
    return _finish1(yf, yb, xbc, z, h1, mod1, d_skip, ssd_norm[0].reshape(1, inner),
                    o_w_out[0].astype(BF16), g_post[1], ctx_len)
```

```python
import functools
import math

import jax
import jax.numpy as jnp
from jax import lax
from jax.experimental import pallas as pl
from jax.experimental.pallas import tpu as pltpu

F32 = jnp.float32
BF16 = jnp.bfloat16

EPS = 1e-6
GRID_W = 64
ROPE_BASE = 10000.0
LRU_C = 8.0
LRU_BLOCKS = 8
DA_HEADS = 4
DA_HEAD_DIM = 64
DA_V_DIM = 128
SSD_HEAD_DIM = 64
SSD_STATE = 128
SSD_GROUPS = 4
SSD_CHUNK = 128

TM = 256
HALO = 8
LRU_TT = 128
KV_CHUNK = 512
ATTN_TQ = 1024
INPROJ_COLS = 512
OUTPROJ_K = 256
FINISH_TM = 512
OUTPROJ_TM = 512
LANES = 128
SUBLANES = 8
VMEM_LIMIT = 56 * 1024 * 1024

HIGHEST = lax.Precision.HIGHEST
LOG2E = math.log2(math.e)
Q_SCALE = DA_HEAD_DIM ** -0.5 * LOG2E


def _sigmoid(x):
    return 0.5 * (1.0 + jnp.tanh(0.5 * x))


def _silu(x):
    h = 0.5 * x
    return h * (1.0 + jnp.tanh(h))


def _rms(x):
    return x * lax.rsqrt(jnp.mean(x * x, axis=-1, keepdims=True) + EPS)


def _cparams(sem):
    return pltpu.CompilerParams(dimension_semantics=sem, vmem_limit_bytes=VMEM_LIMIT)


def _mod_kernel(c_ref, w_ref, b_ref, o_ref):
    s = _silu(c_ref[...])
    o_ref[0] = jnp.dot(s, w_ref[0], preferred_element_type=F32, precision=HIGHEST) + b_ref[0]


def _modulation(c_rows, w_mod, b_mod):
    depth, d, n3 = w_mod.shape
    rows = c_rows.shape[0]
    nt = 1024
    return pl.pallas_call(
        _mod_kernel,
        grid=(depth, n3 // nt),
        in_specs=[
            pl.BlockSpec((rows, d), lambda l, n: (0, 0)),
            pl.BlockSpec((1, d, nt), lambda l, n: (l, 0, n)),
            pl.BlockSpec((1, 1, nt), lambda l, n: (l, 0, n)),
        ],
        out_specs=pl.BlockSpec((1, rows, nt), lambda l, n: (l, 0, n)),
        out_shape=jax.ShapeDtypeStruct((depth, rows, n3), F32),
        compiler_params=_cparams(("arbitrary", "arbitrary")),
        name="modulation",
    )(c_rows, w_mod, b_mod.reshape(depth, 1, n3))


def _normed_rows(h_ext, mod_ref, g_ref, d):
    sh = mod_ref[0, :, 0:d]
    sc = mod_ref[0, :, d:2 * d]
    u = (_rms(h_ext) * g_ref[...]) * (1.0 + sc) + sh
    return u.astype(BF16)


def _store_ext(ext_ref, slab0, x_ext, left_ok, right_ok):
    n = x_ext.shape[1] // LANES
    zero = jnp.zeros((HALO, LANES), F32)
    for c in range(n):
        blk = x_ext[:, c * LANES:(c + 1) * LANES]
        ext_ref[slab0 + c, 0:HALO, :] = jnp.where(left_ok, blk[0:HALO], zero)
        ext_ref[slab0 + c, HALO:HALO + TM, :] = blk[HALO:HALO + TM]
        ext_ref[slab0 + c, HALO + TM:, :] = jnp.where(right_ok, blk[HALO + TM:], zero)


def _conv4(ext_ref, c, w_ref, b_ref):
    ls = slice(c * LANES, (c + 1) * LANES)
    acc = b_ref[:, ls] + w_ref[0:1, ls] * ext_ref[c, pl.ds(HALO - 2, TM), :]
    acc = acc + w_ref[1:2, ls] * ext_ref[c, pl.ds(HALO - 1, TM), :]
    acc = acc + w_ref[2:3, ls] * ext_ref[c, pl.ds(HALO, TM), :]
    acc = acc + w_ref[3:4, ls] * ext_ref[c, pl.ds(HALO + 1, TM), :]
    return acc


def _inproj0_kernel(n_tiles, xm_ref, xl_ref, xr_ref, ctx_ref, mod_ref, g_ref, w_ref, cw_ref, cb_ref,
                    rc_ref, rs1_ref, rs2_ref,
                    xc_ref, gr_ref, q_ref, k_ref, v_ref, gd_ref, ext_ref):
    j = pl.program_id(0)
    d = xm_ref.shape[-1]
    w5 = xc_ref.shape[-1]
    is_ctx = j == 0
    h_main = jnp.where(is_ctx, ctx_ref[0], xm_ref[0])
    h_ext = jnp.concatenate([xl_ref[0], h_main, xr_ref[0]], axis=0)
    u_ext = _normed_rows(h_ext, mod_ref, g_ref, d)
    u = u_ext[HALO:HALO + TM]

    x_ext = jnp.dot(u_ext, w_ref[:, 0:w5], preferred_element_type=F32)
    _store_ext(ext_ref, 0, x_ext, j >= 2, (j >= 1) & (j <= n_tiles - 2))
    for cs in range(w5 // LANES):
        xc_ref[0, :, cs * LANES:(cs + 1) * LANES] = _conv4(ext_ref, cs, cw_ref, cb_ref)

    gr_ref[0] = jnp.dot(u, w_ref[:, w5:2 * w5], preferred_element_type=F32).astype(BF16)
    reps = w5 // LANES
    c = jnp.concatenate([rc_ref[...]] * reps, axis=1)
    s1 = jnp.concatenate([rs1_ref[...]] * reps, axis=1)
    s2 = jnp.concatenate([rs2_ref[...]] * reps, axis=1)
    for idx, o_ref, post in ((2, q_ref, Q_SCALE), (3, k_ref, None)):
        t = jnp.dot(u, w_ref[:, idx * w5:(idx + 1) * w5], preferred_element_type=F32)
        t = t * c + pltpu.roll(t, w5 - 16, 1) * s1 + pltpu.roll(t, 16, 1) * s2
        o_ref[0] = (t if post is None else t * post).astype(BF16)
    v_ref[0] = jnp.dot(u, w_ref[:, 4 * w5:5 * w5], preferred_element_type=F32).astype(BF16)
    gd_ref[0] = jnp.dot(u, w_ref[:, 5 * w5:6 * w5], preferred_element_type=F32).astype(BF16)


def _inproj0(x, ctx, mod, g_pre, w_in, conv_w, conv_b, rope_c, rope_s1, rope_s2):
    bsz, s, d = x.shape
    n_lat = s // TM
    n_tiles = n_lat + 1
    t_all = n_tiles * TM
    w5 = conv_w.shape[-1]
    hb = TM // HALO

    def main_map(j, b):
        return (b, jnp.maximum(j - 1, 0), 0)

    def left_map(j, b):
        return (b, jnp.maximum((j - 1) * hb - 1, 0), 0)

    def right_map(j, b):
        return (b, jnp.minimum(jnp.maximum(j, 0) * hb, s // HALO - 1), 0)

    def mod_map(j, b):
        return (jnp.where(j == 0, bsz, b), 0, 0)

    const2 = lambda j, b: (0, 0)
    tile_out = pl.BlockSpec((1, TM, w5), lambda j, b: (b, j, 0))
    rope_spec = pl.BlockSpec((TM, LANES), lambda j, b: (j, 0))
    outs = pl.pallas_call(
        functools.partial(_inproj0_kernel, n_tiles),
        grid=(n_tiles, bsz),
        in_specs=[
            pl.BlockSpec((1, TM, d), main_map),
            pl.BlockSpec((1, HALO, d), left_map),
            pl.BlockSpec((1, HALO, d), right_map),
            pl.BlockSpec((1, TM, d), lambda j, b: (b, 0, 0)),
            pl.BlockSpec((1, 1, 3 * d), mod_map),
            pl.BlockSpec((1, d), const2),
            pl.BlockSpec(w_in.shape, const2),
            pl.BlockSpec(conv_w.shape, const2),
            pl.BlockSpec((1, w5), const2),
            rope_spec, rope_spec, rope_spec,
        ],
        out_specs=[tile_out] * 6,
        out_shape=[jax.ShapeDtypeStruct((bsz, t_all, w5), F32)]
        + [jax.ShapeDtypeStruct((bsz, t_all, w5), BF16)] * 5,
        scratch_shapes=[pltpu.VMEM((w5 // LANES, TM + 2 * HALO, LANES), F32)],
        compiler_params=_cparams(("arbitrary", "arbitrary")),
        name="inproj0",
    )(x, x, x, ctx, mod, g_pre.reshape(1, d), w_in, conv_w, conv_b.reshape(1, w5), rope_c, rope_s1, rope_s2)
    return outs


def _sqrt_pos(y):
    return jnp.where(y > 0.0, y * lax.rsqrt(y), 0.0)


def _lru_kernel(bsz, xf_ref, xb_ref, wg_ref, bias_ref, k4_ref, of_ref, ob_ref, g_scr, a_scr, b_scr, hs_scr, h_scr):
    i = pl.program_id(0)
    w5 = xf_ref.shape[-1]
    nslab = w5 // LANES
    tt = LRU_TT

    @pl.when(i == 0)
    def _():
        h_scr[...] = jnp.zeros_like(h_scr)

    for dr, x_ref in enumerate((xf_ref, xb_ref)):
        k4 = k4_ref[dr:dr + 1, :]
        g_scr[...] = jnp.dot(x_ref[...].reshape(bsz * tt, w5).astype(BF16), wg_ref[dr], preferred_element_type=F32)
        for b in range(bsz):
            xc = x_ref[b]
            tr = jnp.tanh(g_scr[b * tt:(b + 1) * tt, 0:w5] + bias_ref[2 * dr:2 * dr + 1, :])
            ti = jnp.tanh(g_scr[b * tt:(b + 1) * tt, w5:2 * w5] + bias_ref[2 * dr + 1:2 * dr + 2, :])
            nla = k4 * tr + k4
            a = jnp.exp2(nla * (-LOG2E))
            bc = _sqrt_pos(jnp.tanh(nla) * (a * a + 1.0)) * ((0.5 * xc) * (1.0 + ti))
            for c in range(nslab):
                a_scr[dr, c, pl.ds(b, tt, stride=bsz), :] = a[:, c * LANES:(c + 1) * LANES]
                b_scr[dr, c, pl.ds(b, tt, stride=bsz), :] = bc[:, c * LANES:(c + 1) * LANES]

    def step(t, carry):
        hf, hb = carry
        rf = pl.multiple_of(t * bsz, bsz)
        rb = pl.multiple_of((tt - 1 - t) * bsz, bsz)
        nf, nb = [], []
        for c in range(nslab):
            h = a_scr[0, c, pl.ds(rf, bsz), :] * hf[c] + b_scr[0, c, pl.ds(rf, bsz), :]
            hs_scr[0, c, pl.ds(rf, bsz), :] = h
            nf.append(h)
            h = a_scr[1, c, pl.ds(rb, bsz), :] * hb[c] + b_scr[1, c, pl.ds(rb, bsz), :]
            hs_scr[1, c, pl.ds(rb, bsz), :] = h
            nb.append(h)
        return tuple(nf), tuple(nb)

    h0f = tuple(h_scr[0, c] for c in range(nslab))
    h0b = tuple(h_scr[1, c] for c in range(nslab))
    hf, hb = lax.fori_loop(0, tt, step, (h0f, h0b), unroll=8)
    for c in range(nslab):
        h_scr[0, c] = hf[c]
        h_scr[1, c] = hb[c]

    for dr, o_ref in enumerate((of_ref, ob_ref)):
        for b in range(bsz):
            for c in range(nslab):
                o_ref[b, :, c * LANES:(c + 1) * LANES] = hs_scr[dr, c, pl.ds(b, tt, stride=bsz), :].astype(BF16)


def _bwd_tile(i, n_ctx, n_all):
    return jnp.where(i < n_ctx, n_ctx - 1 - i, n_all - 1 - (i - n_ctx))


def _lru(xc, wg, bias, k4, ctx_len):
    bsz, t_all, w5 = xc.shape
    tt = LRU_TT
    n_all = t_all // tt
    n_ctx = ctx_len // tt
    nslab = w5 // LANES
    fwd_spec = pl.BlockSpec((bsz, tt, w5), lambda i: (0, i, 0))
    bwd_spec = pl.BlockSpec((bsz, tt, w5), lambda i: (0, _bwd_tile(i, n_ctx, n_all), 0))
    const = lambda shape: pl.BlockSpec(shape, lambda i: (0,) * len(shape))
    coef = pltpu.VMEM((2, nslab, tt * bsz, LANES), F32)
    return pl.pallas_call(
        functools.partial(_lru_kernel, bsz),
        grid=(n_all,),
        in_specs=[fwd_spec, bwd_spec, const(wg.shape), const(bias.shape), const(k4.shape)],
        out_specs=[fwd_spec, bwd_spec],
        out_shape=[jax.ShapeDtypeStruct((bsz, t_all, w5), BF16)] * 2,
        scratch_shapes=[pltpu.VMEM((bsz * tt, 2 * w5), F32), coef, coef, coef,
                        pltpu.VMEM((2, nslab, bsz, LANES), F32)],
        compiler_params=_cparams(("arbitrary",)),
        name="rglru",
    )(xc, xc, wg, bias, k4)


def _attn_kernel(lambda_init, n_streams, key_chunks, q_ref, k_ref, v_ref, lam_ref, sub_ref, o_ref, m_scr, acc_scr):
    qqs = []
    for st in range(n_streams):
        q = q_ref[0, st * TM:(st + 1) * TM, :]
        lane = lax.broadcasted_iota(jnp.int32, q.shape, 1)
        zero = jnp.zeros_like(q)
        qqs.append(jnp.concatenate([jnp.where(lane < DA_HEAD_DIM, q, zero),
                                    jnp.where(lane >= DA_HEAD_DIM, q, zero)], axis=0))
    m_scr[...] = jnp.full_like(m_scr, -jnp.inf)
    acc_scr[...] = jnp.zeros_like(acc_scr)

    for start, size in key_chunks:
        kc = k_ref[0, start:start + size, :]
        va = jnp.concatenate([v_ref[0, start:start + size, :], jnp.ones((size, LANES), BF16)], axis=1)
        for st in range(n_streams):
            s = lax.dot_general(qqs[st], kc, (((1,), (1,)), ((), ())), preferred_element_type=F32)
            m_prev = m_scr[st]
            m_next = jnp.maximum(m_prev, jnp.max(s, axis=1, keepdims=True))
            p = jnp.exp2(s - jnp.concatenate([m_next] * (size // LANES), axis=1))
            alpha = jnp.exp2(m_prev - m_next)
            acc_scr[st] = (acc_scr[st] * jnp.concatenate([alpha, alpha], axis=1)
                           + jnp.dot(p.astype(BF16), va, preferred_element_type=F32))
            m_scr[st] = m_next

    lm = lam_ref[...]
    lam = (jnp.exp(jnp.sum(lm[0:1] * lm[1:2], axis=1, keepdims=True))
           - jnp.exp(jnp.sum(lm[2:3] * lm[3:4], axis=1, keepdims=True)) + lambda_init)
    for st in range(n_streams):
        o = (acc_scr[st, 0:TM, 0:LANES] / acc_scr[st, 0:TM, LANES:2 * LANES]
             - lam * (acc_scr[st, TM:2 * TM, 0:LANES] / acc_scr[st, TM:2 * TM, LANES:2 * LANES]))
        o_ref[0, st * TM:(st + 1) * TM, :] = ((_rms(o) * sub_ref[...]) * (1.0 - lambda_init)).astype(BF16)


def _attention(q, k, v, da_lambda, da_subln, lambda_init, ctx_len):
    bsz, t_all, w = q.shape
    hd = DA_V_DIM
    n_lat = t_all - ctx_len
    small = [pl.BlockSpec(da_lambda.shape, lambda b, h, j: (0, 0)), pl.BlockSpec((1, hd), lambda b, h, j: (0, 0))]
    sub = da_subln.reshape(1, hd)

    def scratch(n_streams):
        return [pltpu.VMEM((n_streams, 2 * TM, LANES), F32), pltpu.VMEM((n_streams, 2 * TM, 2 * LANES), F32)]

    n_streams = ATTN_TQ // TM
    chunks = [(0, ctx_len)] + [(ctx_len + c * KV_CHUNK, KV_CHUNK) for c in range(n_lat // KV_CHUNK)]
    q_lat = pl.BlockSpec((pl.Element(1), pl.Element(ATTN_TQ), pl.Element(hd)),
                         lambda b, h, j: (b, pl.multiple_of(ctx_len + j * ATTN_TQ, TM), pl.multiple_of(h * hd, hd)))
    kv_all = pl.BlockSpec((1, t_all, hd), lambda b, h, j: (b, 0, h))
    on_lat = pl.pallas_call(
        functools.partial(_attn_kernel, lambda_init, n_streams, chunks),
        grid=(bsz, DA_HEADS, n_lat // ATTN_TQ),
        in_specs=[q_lat, kv_all, kv_all] + small,
        out_specs=pl.BlockSpec((1, ATTN_TQ, hd), lambda b, h, j: (b, j, h)),
        out_shape=jax.ShapeDtypeStruct((bsz, n_lat, w), BF16),
        scratch_shapes=scratch(n_streams),
        compiler_params=_cparams(("arbitrary", "arbitrary", "arbitrary")),
        name="diffattn",
    )(q, k, v, da_lambda, sub)

    ctx_spec = pl.BlockSpec((1, ctx_len, hd), lambda b, h, j: (b, 0, h))
    on_ctx = pl.pallas_call(
        functools.partial(_attn_kernel, lambda_init, ctx_len // TM, [(0, ctx_len)]),
        grid=(bsz, DA_HEADS, 1),
        in_specs=[ctx_spec, ctx_spec, ctx_spec] + small,
        out_specs=ctx_spec,
        out_shape=jax.ShapeDtypeStruct((bsz, ctx_len, w), BF16),
        scratch_shapes=scratch(ctx_len // TM),
        compiler_params=_cparams(("arbitrary", "arbitrary", "arbitrary")),
        name="diffattn_ctx",
    )(q, k, v, da_lambda, sub)
    return on_lat, on_ctx


def _outproj0_kernel(hf_ref, hb_ref, gr_ref, on_ref, gd_ref, res_ref, mod_ref, g_ref, w_ref, o_ref):
    d = o_ref.shape[-1]
    w5 = hf_ref.shape[-1]
    kc = OUTPROJ_K
    gt = mod_ref[0, :, 2 * d:3 * d]
    for r0 in range(0, o_ref.shape[1], TM):
        rs = slice(r0, r0 + TM)
        y = None
        for c in range(w5 // kc):
            sl = slice(c * kc, (c + 1) * kc)
            r = hf_ref[0, rs, sl].astype(F32) + hb_ref[0, rs, sl].astype(F32)
            m1 = (r * _silu(gr_ref[0, rs, sl].astype(F32))).astype(BF16)
            m2 = (on_ref[0, rs, sl].astype(F32) * _silu(gd_ref[0, rs, sl].astype(F32))).astype(BF16)
            t = jnp.dot(m1, w_ref[c * kc:(c + 1) * kc, :], preferred_element_type=F32)
            t = t + jnp.dot(m2, w_ref[w5 + c * kc:w5 + (c + 1) * kc, :], preferred_element_type=F32)
            y = t if y is None else y + t
        o_ref[0, rs, :] = res_ref[0, rs, :] + gt * (_rms(y) * g_ref[...])


def _outproj0(hf, hb, gr, on_lat, on_ctx, gd, x, ctx, mod, g_post, w_out):
    bsz, t_all, w5 = hf.shape
    d = x.shape[-1]
    ctx_len = ctx.shape[1]
    const2 = lambda j, b: (0, 0)
    consts = [pl.BlockSpec((1, d), const2), pl.BlockSpec(w_out.shape, const2)]

    def call(rows, n_steps, uni, on, res, mod_row, name):
        own = lambda width: pl.BlockSpec((1, rows, width), lambda j, b: (b, j, 0))
        return pl.pallas_call(
            _outproj0_kernel,
            grid=(n_steps, bsz),
            in_specs=[uni, uni, uni, own(w5), uni, own(d),
                      pl.BlockSpec((1, 1, 3 * d), lambda j, b: (mod_row(b), 0, 0))] + consts,
            out_specs=own(d),
            out_shape=jax.ShapeDtypeStruct((bsz, n_steps * rows, d), F32),
            compiler_params=_cparams(("arbitrary", "arbitrary")),
            name=name,
        )(hf, hb, gr, on, gd, res, mod, g_post.reshape(1, d), w_out)

    rows = OUTPROJ_TM
    lat_uni = pl.BlockSpec((pl.Element(1), pl.Element(rows), pl.Element(w5)),
                           lambda j, b: (b, pl.multiple_of(ctx_len + j * rows, TM), 0))
    h_lat = call(rows, (t_all - ctx_len) // rows, lat_uni, on_lat, x, lambda b: b, "outproj0")
    ctx_uni = pl.BlockSpec((1, ctx_len, w5), lambda j, b: (b, 0, 0))
    h_ctx = call(ctx_len, 1, ctx_uni, on_ctx, ctx, lambda b: bsz, "outproj0_ctx")
    return h_lat, h_ctx


def _inproj1_kernel(n_tiles, hm_ref, hl_ref, hr_ref, hc_ref, mod_ref, g_ref, wz_ref, wx_ref, wd_ref, cw_ref, cb_ref,
                    db_ref, z_ref, xbc_ref, dt_ref, ext_ref):
    j = pl.program_id(0)
    d = hm_ref.shape[-1]
    h_main = jnp.where(j == 0, hc_ref[0], hm_ref[0])
    h_ext = jnp.concatenate([hl_ref[0], h_main, hr_ref[0]], axis=0)
    u_ext = _normed_rows(h_ext, mod_ref, g_ref, d)
    u = u_ext[HALO:HALO + TM]
    left_ok = j >= 2
    right_ok = (j >= 1) & (j <= n_tiles - 2)
    nz, nx = wz_ref.shape[1], wx_ref.shape[1]
    cw = INPROJ_COLS
    spc = cw // LANES
    n_x, n_z = nx // cw, nz // cw
    for i in range(max(n_x, n_z)):
        if i < n_x:
            x_ext = jnp.dot(u_ext, wx_ref[:, i * cw:(i + 1) * cw], preferred_element_type=F32)
            _store_ext(ext_ref, i * spc, x_ext, left_ok, right_ok)
            for cs in range(i * spc, (i + 1) * spc):
                xbc_ref[0, :, cs * LANES:(cs + 1) * LANES] = _silu(_conv4(ext_ref, cs, cw_ref, cb_ref)).astype(BF16)
        if i < n_z:
            z_ref[0, :, i * cw:(i + 1) * cw] = _silu(jnp.dot(u, wz_ref[:, i * cw:(i + 1) * cw],
                                                             preferred_element_type=F32)).astype(BF16)
    dt_ref[0] = jax.nn.softplus(jnp.dot(u, wd_ref[...], preferred_element_type=F32) + db_ref[...])


def _inproj1(h_lat, h_ctx, mod, g_pre, wz, wx, wd, conv_w, conv_b, dt_bias, n_mod_rows):
    bsz, s, d = h_lat.shape
    n_tiles = s // TM + 1
    t_all = n_tiles * TM
    hb = TM // HALO
    nz, nx, nd = wz.shape[1], wx.shape[1], wd.shape[1]
    const2 = lambda j, b: (0, 0)
    return pl.pallas_call(
        functools.partial(_inproj1_kernel, n_tiles),
        grid=(n_tiles, bsz),
        in_specs=[
            pl.BlockSpec((1, TM, d), lambda j, b: (b, jnp.maximum(j - 1, 0), 0)),
            pl.BlockSpec((1, HALO, d), lambda j, b: (b, jnp.maximum((j - 1) * hb - 1, 0), 0)),
            pl.BlockSpec((1, HALO, d), lambda j, b: (b, jnp.minimum(jnp.maximum(j, 0) * hb, s // HALO - 1), 0)),
            pl.BlockSpec((1, TM, d), lambda j, b: (b, 0, 0)),
            pl.BlockSpec((1, 1, 3 * d), lambda j, b: (jnp.where(j == 0, n_mod_rows, b), 0, 0)),
            pl.BlockSpec((1, d), const2),
            pl.BlockSpec(wz.shape, const2),
            pl.BlockSpec(wx.shape, const2),
            pl.BlockSpec(wd.shape, const2),
            pl.BlockSpec(conv_w.shape, const2),
            pl.BlockSpec((1, nx), const2),
            pl.BlockSpec((1, nd), const2),
        ],
        out_specs=[pl.BlockSpec((1, TM, nz), lambda j, b: (b, j, 0)),
                   pl.BlockSpec((1, TM, nx), lambda j, b: (b, j, 0)),
                   pl.BlockSpec((1, TM, nd), lambda j, b: (b, j, 0))],
        out_shape=[jax.ShapeDtypeStruct((bsz, t_all, nz), BF16),
                   jax.ShapeDtypeStruct((bsz, t_all, nx), BF16),
                   jax.ShapeDtypeStruct((bsz, t_all, nd), F32)],
        scratch_shapes=[pltpu.VMEM((nx // LANES, TM + 2 * HALO, LANES), F32)],
        compiler_params=_cparams(("arbitrary", "arbitrary")),
        name="inproj1",
    )(h_lat, h_lat, h_lat, h_ctx, mod, g_pre.reshape(1, d), wz, wx, wd, conv_w, conv_b.reshape(1, nx), dt_bias)


def _ssd_prologue(reverse, dr, dt_ref, alog_ref):
    ch = SSD_CHUNK
    dt = dt_ref[0]
    adt = dt * (-jnp.exp(alog_ref[dr:dr + 1, :]) * LOG2E)
    row = lax.broadcasted_iota(jnp.int32, (ch, ch), 0)
    col = lax.broadcasted_iota(jnp.int32, (ch, ch), 1)
    mask = (row <= col) if reverse else (row >= col)
    cs = jnp.dot(mask.astype(F32), adt, preferred_element_type=F32, precision=HIGHEST)
    last = 0 if reverse else ch - 1
    cs_t = cs.T
    dt_t = dt.T
    w_t = jnp.exp2(cs_t[:, last:last + 1] - cs_t) * dt_t
    crow = cs_t - jnp.log2(dt_t)
    e_tot = jnp.exp2(cs[last:last + 1, :])
    return mask, cs, w_t, crow, e_tot


def _ssd_group(dr, g, xbc_ref, s_scr, inner):
    gn = SSD_GROUPS * SSD_STATE
    bg = xbc_ref[0, :, inner + g * SSD_STATE:inner + (g + 1) * SSD_STATE]
    cg = xbc_ref[0, :, inner + gn + g * SSD_STATE:inner + gn + (g + 1) * SSD_STATE]
    cb = lax.dot_general(cg, bg, (((1,), (1,)), ((), ())), preferred_element_type=F32)
    bg_t = bg.astype(F32).T
    s_g = s_scr[dr, g]
    y_off = jnp.dot(cg, s_g.astype(BF16), preferred_element_type=F32)
    return cb, bg_t, s_g, y_off


def _ssd_pair(dr, g, pp, n_heads, pro, grp, xbc_ref, s_scr, y_ref):
    ch = SSD_CHUNK
    mask, cs, w_t, crow, e_tot = pro
    cb, bg_t, s_g, y_off = grp
    pairs_per_group = n_heads // SSD_GROUPS // 2
    p = g * pairs_per_group + pp
    h1 = dr * n_heads + 2 * p
    ls = slice(pp * LANES, (pp + 1) * LANES)
    left = lax.broadcasted_iota(jnp.int32, (ch, LANES), 1) < SSD_HEAD_DIM
    x2 = xbc_ref[0, :, p * LANES:(p + 1) * LANES]
    zero = jnp.zeros_like(x2)
    wx = jnp.concatenate([jnp.where(left, x2, zero), jnp.where(left, zero, x2)], axis=0)
    ms, bws, cols = [], [], []
    for h in (h1, h1 + 1):
        ccol = jnp.broadcast_to(cs[:, h:h + 1], (ch, ch))
        cols.append(ccol)
        ms.append((cb * jnp.exp2(jnp.where(mask, ccol - crow[h:h + 1, :], -jnp.inf))).astype(BF16))
        bws.append((bg_t * w_t[h:h + 1, :]).astype(BF16))
    lhs = jnp.concatenate([jnp.concatenate(ms, axis=1), jnp.concatenate(bws, axis=1)], axis=0)
    r = jnp.dot(lhs, wx, preferred_element_type=F32)
    y = r[0:ch] + jnp.exp2(jnp.where(left, cols[0], cols[1])) * y_off[:, ls]
    y_ref[0, :, p * LANES:(p + 1) * LANES] = y.astype(BF16)
    dec = jnp.where(left[0:1, :], e_tot[:, h1:h1 + 1], e_tot[:, h1 + 1:h1 + 2])
    s_scr[dr, g, :, ls] = s_g[:, ls] * dec + r[ch:2 * ch]


def _ssd_kernel(xf_ref, df_ref, xb_ref, db_ref, alog_ref, yf_ref, yb_ref, s_scr):
    @pl.when(pl.program_id(1) == 0)
    def _():
        s_scr[...] = jnp.zeros_like(s_scr)

    inner = yf_ref.shape[-1]
    n_heads = inner // SSD_HEAD_DIM
    dirs = ((xf_ref, df_ref, yf_ref, False), (xb_ref, db_ref, yb_ref, True))
    pros = [_ssd_prologue(rev, dr, d_ref, alog_ref) for dr, (_, d_ref, _, rev) in enumerate(dirs)]
    for g in range(SSD_GROUPS):
        grps = [_ssd_group(dr, g, x_ref, s_scr, inner) for dr, (x_ref, _, _, _) in enumerate(dirs)]
        for pp in range(n_heads // SSD_GROUPS // 2):
            for dr, (x_ref, _, y_ref, _) in enumerate(dirs):
                _ssd_pair(dr, g, pp, n_heads, pros[dr], grps[dr], x_ref, s_scr, y_ref)


def _ssd(xbc, dt, a_log, ctx_len):
    bsz, t_all, nx = xbc.shape
    nd = dt.shape[-1]
    n_heads = a_log.shape[-1]
    inner = n_heads * SSD_HEAD_DIM
    a_log = jnp.stack([jnp.pad(a_log[dr], (dr * n_heads, nd - (dr + 1) * n_heads)) for dr in range(2)])
    ch = SSD_CHUNK
    n_all = t_all // ch
    n_ctx = ctx_len // ch
    fwd = lambda b, i: (b, i, 0)
    bwd = lambda b, i: (b, _bwd_tile(i, n_ctx, n_all), 0)
    return pl.pallas_call(
        _ssd_kernel,
        grid=(bsz, n_all),
        in_specs=[pl.BlockSpec((1, ch, nx), fwd), pl.BlockSpec((1, ch, nd), fwd),
                  pl.BlockSpec((1, ch, nx), bwd), pl.BlockSpec((1, ch, nd), bwd),
                  pl.BlockSpec(a_log.shape, lambda b, i: (0, 0))],
        out_specs=[pl.BlockSpec((1, ch, inner), fwd), pl.BlockSpec((1, ch, inner), bwd)],
        out_shape=[jax.ShapeDtypeStruct((bsz, t_all, inner), BF16)] * 2,
        scratch_shapes=[pltpu.VMEM((2, SSD_GROUPS, SSD_STATE, inner // SSD_GROUPS), F32)],
        compiler_params=_cparams(("arbitrary", "arbitrary")),
        name="ssd",
    )(xbc, dt, xbc, dt, a_log)


def _finish1_kernel(yf_ref, yb_ref, xs_ref, gz_ref, h_ref, mod_ref, dsk_ref, nw_ref, w_ref, g_ref, o_ref):
    d = o_ref.shape[-1]
    inner = yf_ref.shape[-1]
    gw = inner // SSD_GROUPS
    gt = mod_ref[0, :, 2 * d:3 * d]
    for r0 in range(0, o_ref.shape[1], TM):
        rs = slice(r0, r0 + TM)
        out = None
        for g in range(SSD_GROUPS):
            sl = slice(g * gw, (g + 1) * gw)
            y = (yf_ref[0, rs, sl] + yb_ref[0, rs, sl]).astype(F32) + dsk_ref[:, sl] * xs_ref[0, rs, sl].astype(F32)
            y = y * gz_ref[0, rs, sl].astype(F32)
            t = jnp.dot((_rms(y) * nw_ref[:, sl]).astype(BF16), w_ref[sl, :], preferred_element_type=F32)
            out = t if out is None else out + t
        o_ref[0, rs, :] = h_ref[0, rs, :] + gt * (_rms(out) * g_ref[...])


def _finish1(yf, yb, xbc, gz, h, mod, d_skip, norm_w, w_out, g_post, ctx_len):
    bsz, t_all, inner = yf.shape
    d = h.shape[-1]
    rows = FINISH_TM
    n_lat = (t_all - ctx_len) // rows

    def lat(width):
        return pl.BlockSpec((pl.Element(1), pl.Element(rows), pl.Element(width)),
                            lambda j, b: (b, pl.multiple_of(ctx_len + j * rows, TM), 0))

    const2 = lambda j, b: (0, 0)
    return pl.pallas_call(
        _finish1_kernel,
        grid=(n_lat, bsz),
        in_specs=[lat(inner)] * 4 + [
            pl.BlockSpec((1, rows, d), lambda j, b: (b, j, 0)),
            pl.BlockSpec((1, 1, 3 * d), lambda j, b: (b, 0, 0)),
            pl.BlockSpec((1, inner), const2),
            pl.BlockSpec((1, inner), const2),
            pl.BlockSpec(w_out.shape, const2),
            pl.BlockSpec((1, d), const2),
        ],
        out_specs=pl.BlockSpec((1, rows, d), lambda j, b: (b, j, 0)),
        out_shape=jax.ShapeDtypeStruct((bsz, n_lat * rows, d), F32),
        compiler_params=_cparams(("arbitrary", "arbitrary")),
        name="finish1",
    )(yf, yb, xbc, gz, h, mod, d_skip, norm_w, w_out, g_post.reshape(1, d))


def _rope_tables(n_tokens, ctx_len):
    rows = n_tokens // GRID_W
    row = jnp.repeat(jnp.arange(rows, dtype=F32), GRID_W)
    col = jnp.tile(jnp.arange(GRID_W, dtype=F32), rows)
    n_freq = DA_HEAD_DIM // 4
    inv = ROPE_BASE ** (-jnp.arange(n_freq, dtype=F32) / n_freq)
    ang = jnp.concatenate([row[:, None] * inv, col[:, None] * inv], axis=-1)
    cos, sin = jnp.cos(ang), jnp.sin(ang)
    cr, cc, sr, sc = cos[:, :n_freq], cos[:, n_freq:], sin[:, :n_freq], sin[:, n_freq:]
    zr = jnp.zeros_like(sr)
    c64 = jnp.concatenate([cr, cr, cc, cc], axis=1)
    s1 = jnp.concatenate([-sr, zr, -sc, zr], axis=1)
    s2 = jnp.concatenate([zr, sr, zr, sc], axis=1)
    reps = LANES // DA_HEAD_DIM
    pad = lambda t, v: jnp.concatenate([jnp.full((ctx_len, LANES), v, F32), jnp.tile(t, (1, reps))], axis=0)
    return pad(c64, 1.0), pad(s1, 0.0), pad(s2, 0.0)


def _block_diag(w):
    n, c, d = w.shape
    eye = jnp.eye(n, dtype=w.dtype)
    return (eye[:, None, :, None] * w[:, :, None, :]).reshape(n * c, n * d)


def kernel(x, c, ctx, c_ctx, w_mod, b_mod, g_pre, g_post, e_w_in, e_w_out, lru_conv_w, lru_conv_b, lru_w_r, lru_b_r, lru_w_i, lru_b_i, lru_lambda, da_lambda, da_subln, o_w_in, o_w_out, ssd_conv_w, ssd_conv_b, ssd_a_log, ssd_dt_bias, ssd_d, ssd_norm):
    bsz, s, d = x.shape
    ctx_len = ctx.shape[1]
    assert bsz == SUBLANES and ctx_len == TM and s % KV_CHUNK == 0 and w_mod.shape[0] == 2
    assert e_w_in.shape[0] == 1 and o_w_in.shape[0] == 1

    n_rows = 2 * SUBLANES
    c_rows = jnp.concatenate([c, c_ctx[None, :], jnp.zeros((n_rows - bsz - 1, d), F32)], axis=0)
    mod = _modulation(c_rows, w_mod, b_mod)
    mod0 = mod[0].reshape(n_rows, 1, 3 * d)
    mod1 = mod[1].reshape(n_rows, 1, 3 * d)

    w5 = lru_conv_w.shape[-1]
    w_in0 = e_w_in[0].astype(BF16)
    rope_c, rope_s1, rope_s2 = _rope_tables(s, ctx_len)
    xc, gr, q, k, v, gd = _inproj0(x, ctx, mod0, g_pre[0], w_in0, lru_conv_w[0], lru_conv_b[0],
                                   rope_c, rope_s1, rope_s2)

    wg = (0.5 * jnp.stack([jnp.concatenate([_block_diag(lru_w_r[0, dr]), _block_diag(lru_w_i[0, dr])], axis=1)
                           for dr in range(2)])).astype(BF16)
    bias = 0.5 * jnp.stack([lru_b_r[0, 0], lru_b_i[0, 0], lru_b_r[0, 1], lru_b_i[0, 1]])
    k4 = (0.5 * LRU_C) * jax.nn.softplus(-lru_lambda[0])
    hf, hb = _lru(xc, wg, bias, k4, ctx_len)

    lambda_init = 0.8 - 0.6 * math.exp(-0.3 * 0)
    on_lat, on_ctx = _attention(q, k, v, da_lambda[0], da_subln[0], lambda_init, ctx_len)
    h1_lat, h1_ctx = _outproj0(hf, hb, gr, on_lat, on_ctx, gd, x, ctx, mod0, g_post[0], e_w_out[0].astype(BF16))

    n_heads = ssd_a_log.shape[-1]
    inner = n_heads * SSD_HEAD_DIM
    nx = ssd_conv_w.shape[-1]
    w1 = o_w_in[0]
    wz = w1[:, :inner].astype(BF16)
    wx = w1[:, inner:inner + nx].astype(BF16)
    nd = 2 * n_heads
    wd = jnp.pad(w1[:, inner + nx:], ((0, 0), (0, LANES - nd))).astype(BF16)
    dt_bias = jnp.pad(ssd_dt_bias[0].reshape(1, nd), ((0, 0), (0, LANES - nd)))
    gz, xbc, dt = _inproj1(h1_lat, h1_ctx, mod1, g_pre[1], wz, wx, wd, ssd_conv_w[0], ssd_conv_b[0], dt_bias, bsz)
    yf, yb = _ssd(xbc, dt, ssd_a_log[0], ctx_len)
    d_skip = jnp.repeat(ssd_d[0], SSD_HEAD_DIM).reshape(1, inner)
    return _finish1(yf, yb, xbc, gz, h1_lat, mod1, d_skip, ssd_norm[0].reshape(1, inner),
                    o_w_out[0].astype(BF16), g_post[1], ctx_len)
```

```python
import functools
import math

import jax
import jax.numpy as jnp
from jax import lax
from jax.experimental import pallas as pl
from jax.experimental.pallas import tpu as pltpu

F32 = jnp.float32
BF16 = jnp.bfloat16

EPS = 1e-6
GRID_W = 64
ROPE_BASE = 10000.0
LRU_C = 8.0
LRU_BLOCKS = 8
DA_HEADS = 4
DA_HEAD_DIM = 64
DA_V_DIM = 128
SSD_HEAD_DIM = 64
SSD_STATE = 128
SSD_GROUPS = 4
SSD_CHUNK = 128

TM = 256
HALO = 8
LRU_TT = 128
KV_CHUNK = 512
ATTN_TQ = 1024
INPROJ_TM = 512
INPROJ_COLS = 512
OUTPROJ_K = 256
FINISH_TM = 512
OUTPROJ_TM = 512
LANES = 128
SUBLANES = 8
VMEM_LIMIT = 56 * 1024 * 1024

HIGHEST = lax.Precision.HIGHEST
LOG2E = math.log2(math.e)
Q_SCALE = DA_HEAD_DIM ** -0.5 * LOG2E


def _sigmoid(x):
    return 0.5 * (1.0 + jnp.tanh(0.5 * x))


def _silu(x):
    h = 0.5 * x
    return h * (1.0 + jnp.tanh(h))


def _rms(x):
    return x * lax.rsqrt(jnp.mean(x * x, axis=-1, keepdims=True) + EPS)


def _cparams(sem):
    return pltpu.CompilerParams(dimension_semantics=sem, vmem_limit_bytes=VMEM_LIMIT)


def _mod_kernel(c_ref, w_ref, b_ref, o_ref):
    s = _silu(c_ref[...])
    o_ref[0] = jnp.dot(s, w_ref[0], preferred_element_type=F32, precision=HIGHEST) + b_ref[0]


def _modulation(c_rows, w_mod, b_mod):
    depth, d, n3 = w_mod.shape
    rows = c_rows.shape[0]
    nt = 1024
    return pl.pallas_call(
        _mod_kernel,
        grid=(depth, n3 // nt),
        in_specs=[
            pl.BlockSpec((rows, d), lambda l, n: (0, 0)),
            pl.BlockSpec((1, d, nt), lambda l, n: (l, 0, n)),
            pl.BlockSpec((1, 1, nt), lambda l, n: (l, 0, n)),
        ],
        out_specs=pl.BlockSpec((1, rows, nt), lambda l, n: (l, 0, n)),
        out_shape=jax.ShapeDtypeStruct((depth, rows, n3), F32),
        compiler_params=_cparams(("arbitrary", "arbitrary")),
        name="modulation",
    )(c_rows, w_mod, b_mod.reshape(depth, 1, n3))


def _normed_rows(h_ext, mod_ref, g_ref, d):
    sh = mod_ref[0, :, 0:d]
    sc = mod_ref[0, :, d:2 * d]
    u = (_rms(h_ext) * g_ref[...]) * (1.0 + sc) + sh
    return u.astype(BF16)


def _store_ext(ext_ref, slab0, x_ext, left_ok, right_ok):
    n = x_ext.shape[1] // LANES
    rows = x_ext.shape[0] - 2 * HALO
    zero = jnp.zeros((HALO, LANES), F32)
    for c in range(n):
        blk = x_ext[:, c * LANES:(c + 1) * LANES]
        ext_ref[slab0 + c, 0:HALO, :] = jnp.where(left_ok, blk[0:HALO], zero)
        ext_ref[slab0 + c, HALO:HALO + rows, :] = blk[HALO:HALO + rows]
        ext_ref[slab0 + c, HALO + rows:, :] = jnp.where(right_ok, blk[HALO + rows:], zero)


def _conv4(ext_ref, c, w_ref, b_ref):
    rows = ext_ref.shape[1] - 2 * HALO
    ls = slice(c * LANES, (c + 1) * LANES)
    acc = b_ref[:, ls] + w_ref[0:1, ls] * ext_ref[c, pl.ds(HALO - 2, rows), :]
    acc = acc + w_ref[1:2, ls] * ext_ref[c, pl.ds(HALO - 1, rows), :]
    acc = acc + w_ref[2:3, ls] * ext_ref[c, pl.ds(HALO, rows), :]
    acc = acc + w_ref[3:4, ls] * ext_ref[c, pl.ds(HALO + 1, rows), :]
    return acc


def _segment_edges(is_ctx, n_steps):
    if is_ctx:
        return jnp.bool_(False), jnp.bool_(False)
    j = pl.program_id(0)
    return j >= 1, j <= n_steps - 2


def _uni_lat_spec(rows, width, ctx_len):
    return pl.BlockSpec((pl.Element(1), pl.Element(rows), pl.Element(width)),
                        lambda j, b: (b, pl.multiple_of(ctx_len + j * rows, TM), 0))


N_INPROJ0_IN = 11
N_INPROJ0_OUT = 6


def _inproj0_kernel(is_ctx, n_steps, *refs):
    (hm_ref, hl_ref, hr_ref, mod_ref, g_ref, w_ref, cw_ref, cb_ref, rc_ref, rs1_ref, rs2_ref) = refs[:N_INPROJ0_IN]
    xc_ref, gr_ref, q_ref, k_ref, v_ref, gd_ref, ext_ref = refs[-(N_INPROJ0_OUT + 1):]
    d = hm_ref.shape[-1]
    rows = hm_ref.shape[1]
    w5 = xc_ref.shape[-1]
    h_ext = jnp.concatenate([hl_ref[0], hm_ref[0], hr_ref[0]], axis=0)
    u_ext = _normed_rows(h_ext, mod_ref, g_ref, d)
    u = u_ext[HALO:HALO + rows]

    x_ext = jnp.dot(u_ext, w_ref[:, 0:w5], preferred_element_type=F32)
    _store_ext(ext_ref, 0, x_ext, *_segment_edges(is_ctx, n_steps))
    for cs in range(w5 // LANES):
        xc_ref[0, :, cs * LANES:(cs + 1) * LANES] = _conv4(ext_ref, cs, cw_ref, cb_ref)

    gr_ref[0] = jnp.dot(u, w_ref[:, w5:2 * w5], preferred_element_type=F32).astype(BF16)
    reps = w5 // LANES
    for idx, o_ref, post in ((2, q_ref, Q_SCALE), (3, k_ref, None)):
        t = jnp.dot(u, w_ref[:, idx * w5:(idx + 1) * w5], preferred_element_type=F32)
        if not is_ctx:
            c = jnp.concatenate([rc_ref[...]] * reps, axis=1)
            s1 = jnp.concatenate([rs1_ref[...]] * reps, axis=1)
            s2 = jnp.concatenate([rs2_ref[...]] * reps, axis=1)
            t = t * c + pltpu.roll(t, w5 - 16, 1) * s1 + pltpu.roll(t, 16, 1) * s2
        o_ref[0] = (t if post is None else t * post).astype(BF16)
    v_ref[0] = jnp.dot(u, w_ref[:, 4 * w5:5 * w5], preferred_element_type=F32).astype(BF16)
    gd_ref[0] = jnp.dot(u, w_ref[:, 5 * w5:6 * w5], preferred_element_type=F32).astype(BF16)


def _halo_specs(rows, d, n_rows):
    hb = rows // HALO
    return [pl.BlockSpec((1, rows, d), lambda j, b: (b, j, 0)),
            pl.BlockSpec((1, HALO, d), lambda j, b: (b, jnp.maximum(j * hb - 1, 0), 0)),
            pl.BlockSpec((1, HALO, d), lambda j, b: (b, jnp.minimum((j + 1) * hb, n_rows // HALO - 1), 0))]


def _inproj0(x, ctx, mod, g_pre, w_in, conv_w, conv_b, rope):
    bsz, s, d = x.shape
    ctx_len = ctx.shape[1]
    t_all = ctx_len + s
    w5 = conv_w.shape[-1]
    const2 = lambda j, b: (0, 0)
    consts = [pl.BlockSpec((1, d), const2), pl.BlockSpec(w_in.shape, const2), pl.BlockSpec(conv_w.shape, const2),
              pl.BlockSpec((1, w5), const2)]
    out_shape = ([jax.ShapeDtypeStruct((bsz, t_all, w5), F32)]
                 + [jax.ShapeDtypeStruct((bsz, t_all, w5), BF16)] * (N_INPROJ0_OUT - 1))

    def call(is_ctx, src, rows, out_spec, mod_row, bufs, name):
        n_steps = src.shape[1] // rows
        rope_spec = pl.BlockSpec((rows, LANES), lambda j, b: (j, 0))
        return pl.pallas_call(
            functools.partial(_inproj0_kernel, is_ctx, n_steps),
            grid=(n_steps, bsz),
            in_specs=_halo_specs(rows, d, src.shape[1])
            + [pl.BlockSpec((1, 1, 3 * d), lambda j, b: (mod_row(b), 0, 0))] + consts + [rope_spec] * 3
            + [pl.BlockSpec(memory_space=pl.ANY)] * len(bufs),
            out_specs=[out_spec] * N_INPROJ0_OUT,
            out_shape=out_shape,
            input_output_aliases={N_INPROJ0_IN + i: i for i in range(len(bufs))},
            scratch_shapes=[pltpu.VMEM((w5 // LANES, rows + 2 * HALO, LANES), F32)],
            compiler_params=_cparams(("arbitrary", "arbitrary")),
            name=name,
        )(src, src, src, mod, g_pre.reshape(1, d), w_in, conv_w, conv_b.reshape(1, w5), *rope, *bufs)

    outs = call(False, x, INPROJ_TM, _uni_lat_spec(INPROJ_TM, w5, ctx_len), lambda b: b, [], "inproj0")
    ctx_spec = pl.BlockSpec((1, ctx_len, w5), lambda j, b: (b, 0, 0))
    return call(True, ctx, ctx_len, ctx_spec, lambda b: bsz, outs, "inproj0_ctx")


def _sqrt_pos(y):
    return jnp.where(y > 0.0, y * lax.rsqrt(y), 0.0)


def _lru_kernel(bsz, xf_ref, xb_ref, wg_ref, bias_ref, k4_ref, of_ref, ob_ref, g_scr, a_scr, b_scr, hs_scr, h_scr):
    i = pl.program_id(0)
    w5 = xf_ref.shape[-1]
    nslab = w5 // LANES
    tt = LRU_TT

    @pl.when(i == 0)
    def _():
        h_scr[...] = jnp.zeros_like(h_scr)

    for dr, x_ref in enumerate((xf_ref, xb_ref)):
        k4 = k4_ref[dr:dr + 1, :]
        g_scr[...] = jnp.dot(x_ref[...].reshape(bsz * tt, w5).astype(BF16), wg_ref[dr], preferred_element_type=F32)
        for b in range(bsz):
            xc = x_ref[b]
            tr = jnp.tanh(g_scr[b * tt:(b + 1) * tt, 0:w5] + bias_ref[2 * dr:2 * dr + 1, :])
            ti = jnp.tanh(g_scr[b * tt:(b + 1) * tt, w5:2 * w5] + bias_ref[2 * dr + 1:2 * dr + 2, :])
            nla = k4 * tr + k4
            a = jnp.exp2(nla * (-LOG2E))
            bc = _sqrt_pos(jnp.tanh(nla) * (a * a + 1.0)) * ((0.5 * xc) * (1.0 + ti))
            for c in range(nslab):
                a_scr[dr, c, pl.ds(b, tt, stride=bsz), :] = a[:, c * LANES:(c + 1) * LANES]
                b_scr[dr, c, pl.ds(b, tt, stride=bsz), :] = bc[:, c * LANES:(c + 1) * LANES]

    def step(t, carry):
        hf, hb = carry
        rf = pl.multiple_of(t * bsz, bsz)
        rb = pl.multiple_of((tt - 1 - t) * bsz, bsz)
        nf, nb = [], []
        for c in range(nslab):
            h = a_scr[0, c, pl.ds(rf, bsz), :] * hf[c] + b_scr[0, c, pl.ds(rf, bsz), :]
            hs_scr[0, c, pl.ds(rf, bsz), :] = h
            nf.append(h)
            h = a_scr[1, c, pl.ds(rb, bsz), :] * hb[c] + b_scr[1, c, pl.ds(rb, bsz), :]
            hs_scr[1, c, pl.ds(rb, bsz), :] = h
            nb.append(h)
        return tuple(nf), tuple(nb)

    h0f = tuple(h_scr[0, c] for c in range(nslab))
    h0b = tuple(h_scr[1, c] for c in range(nslab))
    hf, hb = lax.fori_loop(0, tt, step, (h0f, h0b), unroll=8)
    for c in range(nslab):
        h_scr[0, c] = hf[c]
        h_scr[1, c] = hb[c]

    for dr, o_ref in enumerate((of_ref, ob_ref)):
        for b in range(bsz):
            for c in range(nslab):
                o_ref[b, :, c * LANES:(c + 1) * LANES] = hs_scr[dr, c, pl.ds(b, tt, stride=bsz), :].astype(BF16)


def _bwd_tile(i, n_ctx, n_all):
    return jnp.where(i < n_ctx, n_ctx - 1 - i, n_all - 1 - (i - n_ctx))


def _lru(xc, wg, bias, k4, ctx_len):
    bsz, t_all, w5 = xc.shape
    tt = LRU_TT
    n_all = t_all // tt
    n_ctx = ctx_len // tt
    nslab = w5 // LANES
    fwd_spec = pl.BlockSpec((bsz, tt, w5), lambda i: (0, i, 0))
    bwd_spec = pl.BlockSpec((bsz, tt, w5), lambda i: (0, _bwd_tile(i, n_ctx, n_all), 0))
    const = lambda shape: pl.BlockSpec(shape, lambda i: (0,) * len(shape))
    coef = pltpu.VMEM((2, nslab, tt * bsz, LANES), F32)
    return pl.pallas_call(
        functools.partial(_lru_kernel, bsz),
        grid=(n_all,),
        in_specs=[fwd_spec, bwd_spec, const(wg.shape), const(bias.shape), const(k4.shape)],
        out_specs=[fwd_spec, bwd_spec],
        out_shape=[jax.ShapeDtypeStruct((bsz, t_all, w5), BF16)] * 2,
        scratch_shapes=[pltpu.VMEM((bsz * tt, 2 * w5), F32), coef, coef, coef,
                        pltpu.VMEM((2, nslab, bsz, LANES), F32)],
        compiler_params=_cparams(("arbitrary",)),
        name="rglru",
    )(xc, xc, wg, bias, k4)


def _attn_kernel(lambda_init, n_streams, key_chunks, q_ref, k_ref, v_ref, lam_ref, sub_ref, o_ref, m_scr, acc_scr):
    qqs = []
    for st in range(n_streams):
        q = q_ref[0, st * TM:(st + 1) * TM, :]
        lane = lax.broadcasted_iota(jnp.int32, q.shape, 1)
        zero = jnp.zeros_like(q)
        qqs.append(jnp.concatenate([jnp.where(lane < DA_HEAD_DIM, q, zero),
                                    jnp.where(lane >= DA_HEAD_DIM, q, zero)], axis=0))
    m_scr[...] = jnp.full_like(m_scr, -jnp.inf)
    acc_scr[...] = jnp.zeros_like(acc_scr)

    for start, size in key_chunks:
        kc = k_ref[0, start:start + size, :]
        va = jnp.concatenate([v_ref[0, start:start + size, :], jnp.ones((size, LANES), BF16)], axis=1)
        for st in range(n_streams):
            s = lax.dot_general(qqs[st], kc, (((1,), (1,)), ((), ())), preferred_element_type=F32)
            m_prev = m_scr[st]
            m_next = jnp.maximum(m_prev, jnp.max(s, axis=1, keepdims=True))
            p = jnp.exp2(s - jnp.concatenate([m_next] * (size // LANES), axis=1))
            alpha = jnp.exp2(m_prev - m_next)
            acc_scr[st] = (acc_scr[st] * jnp.concatenate([alpha, alpha], axis=1)
                           + jnp.dot(p.astype(BF16), va, preferred_element_type=F32))
            m_scr[st] = m_next

    lm = lam_ref[...]
    lam = (jnp.exp(jnp.sum(lm[0:1] * lm[1:2], axis=1, keepdims=True))
           - jnp.exp(jnp.sum(lm[2:3] * lm[3:4], axis=1, keepdims=True)) + lambda_init)
    for st in range(n_streams):
        o = (acc_scr[st, 0:TM, 0:LANES] / acc_scr[st, 0:TM, LANES:2 * LANES]
             - lam * (acc_scr[st, TM:2 * TM, 0:LANES] / acc_scr[st, TM:2 * TM, LANES:2 * LANES]))
        o_ref[0, st * TM:(st + 1) * TM, :] = ((_rms(o) * sub_ref[...]) * (1.0 - lambda_init)).astype(BF16)


def _attention(q, k, v, da_lambda, da_subln, lambda_init, ctx_len):
    bsz, t_all, w = q.shape
    hd = DA_V_DIM
    n_lat = t_all - ctx_len
    small = [pl.BlockSpec(da_lambda.shape, lambda b, h, j: (0, 0)), pl.BlockSpec((1, hd), lambda b, h, j: (0, 0))]
    sub = da_subln.reshape(1, hd)

    def scratch(n_streams):
        return [pltpu.VMEM((n_streams, 2 * TM, LANES), F32), pltpu.VMEM((n_streams, 2 * TM, 2 * LANES), F32)]

    n_streams = ATTN_TQ // TM
    chunks = [(0, ctx_len)] + [(ctx_len + c * KV_CHUNK, KV_CHUNK) for c in range(n_lat // KV_CHUNK)]
    q_lat = pl.BlockSpec((pl.Element(1), pl.Element(ATTN_TQ), pl.Element(hd)),
                         lambda b, h, j: (b, pl.multiple_of(ctx_len + j * ATTN_TQ, TM), pl.multiple_of(h * hd, hd)))
    kv_all = pl.BlockSpec((1, t_all, hd), lambda b, h, j: (b, 0, h))
    on_lat = pl.pallas_call(
        functools.partial(_attn_kernel, lambda_init, n_streams, chunks),
        grid=(bsz, DA_HEADS, n_lat // ATTN_TQ),
        in_specs=[q_lat, kv_all, kv_all] + small,
        out_specs=pl.BlockSpec((1, ATTN_TQ, hd), lambda b, h, j: (b, j, h)),
        out_shape=jax.ShapeDtypeStruct((bsz, n_lat, w), BF16),
        scratch_shapes=scratch(n_streams),
        compiler_params=_cparams(("arbitrary", "arbitrary", "arbitrary")),
        name="diffattn",
    )(q, k, v, da_lambda, sub)

    ctx_spec = pl.BlockSpec((1, ctx_len, hd), lambda b, h, j: (b, 0, h))
    on_ctx = pl.pallas_call(
        functools.partial(_attn_kernel, lambda_init, ctx_len // TM, [(0, ctx_len)]),
        grid=(bsz, DA_HEADS, 1),
        in_specs=[ctx_spec, ctx_spec, ctx_spec] + small,
        out_specs=ctx_spec,
        out_shape=jax.ShapeDtypeStruct((bsz, ctx_len, w), BF16),
        scratch_shapes=scratch(ctx_len // TM),
        compiler_params=_cparams(("arbitrary", "arbitrary", "arbitrary")),
        name="diffattn_ctx",
    )(q, k, v, da_lambda, sub)
    return on_lat, on_ctx


def _outproj0_kernel(hf_ref, hb_ref, gr_ref, on_ref, gd_ref, res_ref, mod_ref, g_ref, w_ref, o_ref):
    d = o_ref.shape[-1]
    w5 = hf_ref.shape[-1]
    kc = OUTPROJ_K
    gt = mod_ref[0, :, 2 * d:3 * d]
    for r0 in range(0, o_ref.shape[1], TM):
        rs = slice(r0, r0 + TM)
        y = None
        for c in range(w5 // kc):
            sl = slice(c * kc, (c + 1) * kc)
            r = hf_ref[0, rs, sl].astype(F32) + hb_ref[0, rs, sl].astype(F32)
            m1 = (r * _silu(gr_ref[0, rs, sl].astype(F32))).astype(BF16)
            m2 = (on_ref[0, rs, sl].astype(F32) * _silu(gd_ref[0, rs, sl].astype(F32))).astype(BF16)
            t = jnp.dot(m1, w_ref[c * kc:(c + 1) * kc, :], preferred_element_type=F32)
            t = t + jnp.dot(m2, w_ref[w5 + c * kc:w5 + (c + 1) * kc, :], preferred_element_type=F32)
            y = t if y is None else y + t
        o_ref[0, rs, :] = res_ref[0, rs, :] + gt * (_rms(y) * g_ref[...])


def _outproj0(hf, hb, gr, on_lat, on_ctx, gd, x, ctx, mod, g_post, w_out):
    bsz, t_all, w5 = hf.shape
    d = x.shape[-1]
    ctx_len = ctx.shape[1]
    const2 = lambda j, b: (0, 0)
    consts = [pl.BlockSpec((1, d), const2), pl.BlockSpec(w_out.shape, const2)]

    def call(rows, n_steps, uni, on, res, mod_row, name):
        own = lambda width: pl.BlockSpec((1, rows, width), lambda j, b: (b, j, 0))
        return pl.pallas_call(
            _outproj0_kernel,
            grid=(n_steps, bsz),
            in_specs=[uni, uni, uni, own(w5), uni, own(d),
                      pl.BlockSpec((1, 1, 3 * d), lambda j, b: (mod_row(b), 0, 0))] + consts,
            out_specs=own(d),
            out_shape=jax.ShapeDtypeStruct((bsz, n_steps * rows, d), F32),
            compiler_params=_cparams(("arbitrary", "arbitrary")),
            name=name,
        )(hf, hb, gr, on, gd, res, mod, g_post.reshape(1, d), w_out)

    rows = OUTPROJ_TM
    lat_uni = pl.BlockSpec((pl.Element(1), pl.Element(rows), pl.Element(w5)),
                           lambda j, b: (b, pl.multiple_of(ctx_len + j * rows, TM), 0))
    h_lat = call(rows, (t_all - ctx_len) // rows, lat_uni, on_lat, x, lambda b: b, "outproj0")
    ctx_uni = pl.BlockSpec((1, ctx_len, w5), lambda j, b: (b, 0, 0))
    h_ctx = call(ctx_len, 1, ctx_uni, on_ctx, ctx, lambda b: bsz, "outproj0_ctx")
    return h_lat, h_ctx


N_INPROJ1_IN = 11
N_INPROJ1_OUT = 3


def _inproj1_kernel(is_ctx, n_steps, *refs):
    (hm_ref, hl_ref, hr_ref, mod_ref, g_ref, wz_ref, wx_ref, wd_ref, cw_ref, cb_ref, db_ref) = refs[:N_INPROJ1_IN]
    z_ref, xbc_ref, dt_ref, ext_ref = refs[-(N_INPROJ1_OUT + 1):]
    d = hm_ref.shape[-1]
    rows = hm_ref.shape[1]
    h_ext = jnp.concatenate([hl_ref[0], hm_ref[0], hr_ref[0]], axis=0)
    u_ext = _normed_rows(h_ext, mod_ref, g_ref, d)
    u = u_ext[HALO:HALO + rows]
    left_ok, right_ok = _segment_edges(is_ctx, n_steps)
    nz, nx = wz_ref.shape[1], wx_ref.shape[1]
    cw = INPROJ_COLS
    spc = cw // LANES
    n_x, n_z = nx // cw, nz // cw
    for i in range(max(n_x, n_z)):
        if i < n_x:
            x_ext = jnp.dot(u_ext, wx_ref[:, i * cw:(i + 1) * cw], preferred_element_type=F32)
            _store_ext(ext_ref, i * spc, x_ext, left_ok, right_ok)
            for cs in range(i * spc, (i + 1) * spc):
                xbc_ref[0, :, cs * LANES:(cs + 1) * LANES] = _silu(_conv4(ext_ref, cs, cw_ref, cb_ref)).astype(BF16)
        if i < n_z:
            z_ref[0, :, i * cw:(i + 1) * cw] = _silu(jnp.dot(u, wz_ref[:, i * cw:(i + 1) * cw],
                                                             preferred_element_type=F32)).astype(BF16)
    dt_ref[0] = jax.nn.softplus(jnp.dot(u, wd_ref[...], preferred_element_type=F32) + db_ref[...])


def _inproj1(h_lat, h_ctx, mod, g_pre, wz, wx, wd, conv_w, conv_b, dt_bias):
    bsz, s, d = h_lat.shape
    ctx_len = h_ctx.shape[1]
    t_all = ctx_len + s
    nz, nx, nd = wz.shape[1], wx.shape[1], wd.shape[1]
    const2 = lambda j, b: (0, 0)
    consts = [pl.BlockSpec((1, d), const2), pl.BlockSpec(wz.shape, const2), pl.BlockSpec(wx.shape, const2),
              pl.BlockSpec(wd.shape, const2), pl.BlockSpec(conv_w.shape, const2), pl.BlockSpec((1, nx), const2),
              pl.BlockSpec((1, nd), const2)]
    out_shape = [jax.ShapeDtypeStruct((bsz, t_all, nz), BF16), jax.ShapeDtypeStruct((bsz, t_all, nx), BF16),
                 jax.ShapeDtypeStruct((bsz, t_all, nd), F32)]

    def call(is_ctx, src, rows, out_spec, mod_row, bufs, name):
        n_steps = src.shape[1] // rows
        return pl.pallas_call(
            functools.partial(_inproj1_kernel, is_ctx, n_steps),
            grid=(n_steps, bsz),
            in_specs=_halo_specs(rows, d, src.shape[1])
            + [pl.BlockSpec((1, 1, 3 * d), lambda j, b: (mod_row(b), 0, 0))] + consts
            + [pl.BlockSpec(memory_space=pl.ANY)] * len(bufs),
            out_specs=[out_spec(w) for w in (nz, nx, nd)],
            out_shape=out_shape,
            input_output_aliases={N_INPROJ1_IN + i: i for i in range(len(bufs))},
            scratch_shapes=[pltpu.VMEM((nx // LANES, rows + 2 * HALO, LANES), F32)],
            compiler_params=_cparams(("arbitrary", "arbitrary")),
            name=name,
        )(src, src, src, mod, g_pre.reshape(1, d), wz, wx, wd, conv_w, conv_b.reshape(1, nx), dt_bias, *bufs)

    outs = call(False, h_lat, INPROJ_TM, lambda w: _uni_lat_spec(INPROJ_TM, w, ctx_len), lambda b: b, [], "inproj1")
    return call(True, h_ctx, ctx_len, lambda w: pl.BlockSpec((1, ctx_len, w), lambda j, b: (b, 0, 0)),
                lambda b: bsz, outs, "inproj1_ctx")


def _ssd_prologue(reverse, dr, dt_ref, alog_ref):
    ch = SSD_CHUNK
    dt = dt_ref[0]
    adt = dt * (-jnp.exp(alog_ref[dr:dr + 1, :]) * LOG2E)
    row = lax.broadcasted_iota(jnp.int32, (ch, ch), 0)
    col = lax.broadcasted_iota(jnp.int32, (ch, ch), 1)
    mask = (row <= col) if reverse else (row >= col)
    cs = jnp.dot(mask.astype(F32), adt, preferred_element_type=F32, precision=HIGHEST)
    last = 0 if reverse else ch - 1
    cs_t = cs.T
    dt_t = dt.T
    w_t = jnp.exp2(cs_t[:, last:last + 1] - cs_t) * dt_t
    crow = cs_t - jnp.log2(dt_t)
    e_tot = jnp.exp2(cs[last:last + 1, :])
    return mask, cs, w_t, crow, e_tot


def _ssd_group(dr, g, xbc_ref, s_scr, inner):
    gn = SSD_GROUPS * SSD_STATE
    bg = xbc_ref[0, :, inner + g * SSD_STATE:inner + (g + 1) * SSD_STATE]
    cg = xbc_ref[0, :, inner + gn + g * SSD_STATE:inner + gn + (g + 1) * SSD_STATE]
    cb = lax.dot_general(cg, bg, (((1,), (1,)), ((), ())), preferred_element_type=F32)
    bg_t = bg.astype(F32).T
    s_g = s_scr[dr, g]
    y_off = jnp.dot(cg, s_g.astype(BF16), preferred_element_type=F32)
    return cb, bg_t, s_g, y_off


def _ssd_pair(dr, g, pp, n_heads, pro, grp, xbc_ref, s_scr, y_ref):
    ch = SSD_CHUNK
    mask, cs, w_t, crow, e_tot = pro
    cb, bg_t, s_g, y_off = grp
    pairs_per_group = n_heads // SSD_GROUPS // 2
    p = g * pairs_per_group + pp
    h1 = dr * n_heads + 2 * p
    ls = slice(pp * LANES, (pp + 1) * LANES)
    left = lax.broadcasted_iota(jnp.int32, (ch, LANES), 1) < SSD_HEAD_DIM
    x2 = xbc_ref[0, :, p * LANES:(p + 1) * LANES]
    zero = jnp.zeros_like(x2)
    wx = jnp.concatenate([jnp.where(left, x2, zero), jnp.where(left, zero, x2)], axis=0)
    ms, bws, cols = [], [], []
    for h in (h1, h1 + 1):
        ccol = jnp.broadcast_to(cs[:, h:h + 1], (ch, ch))
        cols.append(ccol)
        ms.append((cb * jnp.exp2(jnp.where(mask, ccol - crow[h:h + 1, :], -jnp.inf))).astype(BF16))
        bws.append((bg_t * w_t[h:h + 1, :]).astype(BF16))
    lhs = jnp.concatenate([jnp.concatenate(ms, axis=1), jnp.concatenate(bws, axis=1)], axis=0)
    r = jnp.dot(lhs, wx, preferred_element_type=F32)
    y = r[0:ch] + jnp.exp2(jnp.where(left, cols[0], cols[1])) * y_off[:, ls]
    y_ref[0, :, p * LANES:(p + 1) * LANES] = y.astype(BF16)
    dec = jnp.where(left[0:1, :], e_tot[:, h1:h1 + 1], e_tot[:, h1 + 1:h1 + 2])
    s_scr[dr, g, :, ls] = s_g[:, ls] * dec + r[ch:2 * ch]


def _ssd_kernel(xf_ref, df_ref, xb_ref, db_ref, alog_ref, yf_ref, yb_ref, s_scr):
    @pl.when(pl.program_id(1) == 0)
    def _():
        s_scr[...] = jnp.zeros_like(s_scr)

    inner = yf_ref.shape[-1]
    n_heads = inner // SSD_HEAD_DIM
    dirs = ((xf_ref, df_ref, yf_ref, False), (xb_ref, db_ref, yb_ref, True))
    pros = [_ssd_prologue(rev, dr, d_ref, alog_ref) for dr, (_, d_ref, _, rev) in enumerate(dirs)]
    for g in range(SSD_GROUPS):
        grps = [_ssd_group(dr, g, x_ref, s_scr, inner) for dr, (x_ref, _, _, _) in enumerate(dirs)]
        for pp in range(n_heads // SSD_GROUPS // 2):
            for dr, (x_ref, _, y_ref, _) in enumerate(dirs):
                _ssd_pair(dr, g, pp, n_heads, pros[dr], grps[dr], x_ref, s_scr, y_ref)


def _ssd(xbc, dt, a_log, ctx_len):
    bsz, t_all, nx = xbc.shape
    nd = dt.shape[-1]
    n_heads = a_log.shape[-1]
    inner = n_heads * SSD_HEAD_DIM
    a_log = jnp.stack([jnp.pad(a_log[dr], (dr * n_heads, nd - (dr + 1) * n_heads)) for dr in range(2)])
    ch = SSD_CHUNK
    n_all = t_all // ch
    n_ctx = ctx_len // ch
    fwd = lambda b, i: (b, i, 0)
    bwd = lambda b, i: (b, _bwd_tile(i, n_ctx, n_all), 0)
    return pl.pallas_call(
        _ssd_kernel,
        grid=(bsz, n_all),
        in_specs=[pl.BlockSpec((1, ch, nx), fwd), pl.BlockSpec((1, ch, nd), fwd),
                  pl.BlockSpec((1, ch, nx), bwd), pl.BlockSpec((1, ch, nd), bwd),
                  pl.BlockSpec(a_log.shape, lambda b, i: (0, 0))],
        out_specs=[pl.BlockSpec((1, ch, inner), fwd), pl.BlockSpec((1, ch, inner), bwd)],
        out_shape=[jax.ShapeDtypeStruct((bsz, t_all, inner), BF16)] * 2,
        scratch_shapes=[pltpu.VMEM((2, SSD_GROUPS, SSD_STATE, inner // SSD_GROUPS), F32)],
        compiler_params=_cparams(("arbitrary", "arbitrary")),
        name="ssd",
    )(xbc, dt, xbc, dt, a_log)


def _finish1_kernel(yf_ref, yb_ref, xs_ref, gz_ref, h_ref, mod_ref, dsk_ref, nw_ref, w_ref, g_ref, o_ref):
    d = o_ref.shape[-1]
    inner = yf_ref.shape[-1]
    gw = inner // SSD_GROUPS
    gt = mod_ref[0, :, 2 * d:3 * d]
    for r0 in range(0, o_ref.shape[1], TM):
        rs = slice(r0, r0 + TM)
        out = None
        for g in range(SSD_GROUPS):
            sl = slice(g * gw, (g + 1) * gw)
            y = (yf_ref[0, rs, sl] + yb_ref[0, rs, sl]).astype(F32) + dsk_ref[:, sl] * xs_ref[0, rs, sl].astype(F32)
            y = y * gz_ref[0, rs, sl].astype(F32)
            t = jnp.dot((_rms(y) * nw_ref[:, sl]).astype(BF16), w_ref[sl, :], preferred_element_type=F32)
            out = t if out is None else out + t
        o_ref[0, rs, :] = h_ref[0, rs, :] + gt * (_rms(out) * g_ref[...])


def _finish1(yf, yb, xbc, gz, h, mod, d_skip, norm_w, w_out, g_post, ctx_len):
    bsz, t_all, inner = yf.shape
    d = h.shape[-1]
    rows = FINISH_TM
    n_lat = (t_all - ctx_len) // rows

    def lat(width):
        return pl.BlockSpec((pl.Element(1), pl.Element(rows), pl.Element(width)),
                            lambda j, b: (b, pl.multiple_of(ctx_len + j * rows, TM), 0))

    const2 = lambda j, b: (0, 0)
    return pl.pallas_call(
        _finish1_kernel,
        grid=(n_lat, bsz),
        in_specs=[lat(inner)] * 4 + [
            pl.BlockSpec((1, rows, d), lambda j, b: (b, j, 0)),
            pl.BlockSpec((1, 1, 3 * d), lambda j, b: (b, 0, 0)),
            pl.BlockSpec((1, inner), const2),
            pl.BlockSpec((1, inner), const2),
            pl.BlockSpec(w_out.shape, const2),
            pl.BlockSpec((1, d), const2),
        ],
        out_specs=pl.BlockSpec((1, rows, d), lambda j, b: (b, j, 0)),
        out_shape=jax.ShapeDtypeStruct((bsz, n_lat * rows, d), F32),
        compiler_params=_cparams(("arbitrary", "arbitrary")),
        name="finish1",
    )(yf, yb, xbc, gz, h, mod, d_skip, norm_w, w_out, g_post.reshape(1, d))


def _rope_tables(n_tokens):
    rows = n_tokens // GRID_W
    row = jnp.repeat(jnp.arange(rows, dtype=F32), GRID_W)
    col = jnp.tile(jnp.arange(GRID_W, dtype=F32), rows)
    n_freq = DA_HEAD_DIM // 4
    inv = ROPE_BASE ** (-jnp.arange(n_freq, dtype=F32) / n_freq)
    ang = jnp.concatenate([row[:, None] * inv, col[:, None] * inv], axis=-1)
    cos, sin = jnp.cos(ang), jnp.sin(ang)
    cr, cc, sr, sc = cos[:, :n_freq], cos[:, n_freq:], sin[:, :n_freq], sin[:, n_freq:]
    zr = jnp.zeros_like(sr)
    c64 = jnp.concatenate([cr, cr, cc, cc], axis=1)
    s1 = jnp.concatenate([-sr, zr, -sc, zr], axis=1)
    s2 = jnp.concatenate([zr, sr, zr, sc], axis=1)
    reps = LANES // DA_HEAD_DIM
    return tuple(jnp.tile(t, (1, reps)) for t in (c64, s1, s2))


def _block_diag(w):
    n, c, d = w.shape
    eye = jnp.eye(n, dtype=w.dtype)
    return (eye[:, None, :, None] * w[:, :, None, :]).reshape(n * c, n * d)


def kernel(x, c, ctx, c_ctx, w_mod, b_mod, g_pre, g_post, e_w_in, e_w_out, lru_conv_w, lru_conv_b, lru_w_r, lru_b_r, lru_w_i, lru_b_i, lru_lambda, da_lambda, da_subln, o_w_in, o_w_out, ssd_conv_w, ssd_conv_b, ssd_a_log, ssd_dt_bias, ssd_d, ssd_norm):
    bsz, s, d = x.shape
    ctx_len = ctx.shape[1]
    assert bsz == SUBLANES and ctx_len == TM and s % KV_CHUNK == 0 and w_mod.shape[0] == 2
    assert e_w_in.shape[0] == 1 and o_w_in.shape[0] == 1

    n_rows = 2 * SUBLANES
    c_rows = jnp.concatenate([c, c_ctx[None, :], jnp.zeros((n_rows - bsz - 1, d), F32)], axis=0)
    mod = _modulation(c_rows, w_mod, b_mod)
    mod0 = mod[0].reshape(n_rows, 1, 3 * d)
    mod1 = mod[1].reshape(n_rows, 1, 3 * d)

    w5 = lru_conv_w.shape[-1]
    w_in0 = e_w_in[0].astype(BF16)
    xc, gr, q, k, v, gd = _inproj0(x, ctx, mod0, g_pre[0], w_in0, lru_conv_w[0], lru_conv_b[0], _rope_tables(s))

    wg = (0.5 * jnp.stack([jnp.concatenate([_block_diag(lru_w_r[0, dr]), _block_diag(lru_w_i[0, dr])], axis=1)
                           for dr in range(2)])).astype(BF16)
    bias = 0.5 * jnp.stack([lru_b_r[0, 0], lru_b_i[0, 0], lru_b_r[0, 1], lru_b_i[0, 1]])
    k4 = (0.5 * LRU_C) * jax.nn.softplus(-lru_lambda[0])
    hf, hb = _lru(xc, wg, bias, k4, ctx_len)

    lambda_init = 0.8 - 0.6 * math.exp(-0.3 * 0)
    on_lat, on_ctx = _attention(q, k, v, da_lambda[0], da_subln[0], lambda_init, ctx_len)
    h1_lat, h1_ctx = _outproj0(hf, hb, gr, on_lat, on_ctx, gd, x, ctx, mod0, g_post[0], e_w_out[0].astype(BF16))

    n_heads = ssd_a_log.shape[-1]
    inner = n_heads * SSD_HEAD_DIM
    nx = ssd_conv_w.shape[-1]
    w1 = o_w_in[0]
    wz = w1[:, :inner].astype(BF16)
    wx = w1[:, inner:inner + nx].astype(BF16)
    nd = 2 * n_heads
    wd = jnp.pad(w1[:, inner + nx:], ((0, 0), (0, LANES - nd))).astype(BF16)
    dt_bias = jnp.pad(ssd_dt_bias[0].reshape(1, nd), ((0, 0), (0, LANES - nd)))
    gz, xbc, dt = _inproj1(h1_lat, h1_ctx, mod1, g_pre[1], wz, wx, wd, ssd_conv_w[0], ssd_conv_b[0], dt_bias)
    yf, yb = _ssd(xbc, dt, ssd_a_log[0], ctx_len)
    d_skip = jnp.repeat(ssd_d[0], SSD_HEAD_DIM).reshape(1, inner)
    return _finish1(yf, yb, xbc, gz, h1_lat, mod1, d_skip, ssd_norm[0].reshape(1, inner),
                    o_w_out[0].astype(BF16), g_post[1], ctx_len)
```

```python
import functools
import math

import jax
import jax.numpy as jnp
from jax import lax
from jax.experimental import pallas as pl
from jax.experimental.pallas import tpu as pltpu

F32 = jnp.float32
BF16 = jnp.bfloat16

EPS = 1e-6
GRID_W = 64
ROPE_BASE = 10000.0
LRU_C = 8.0
LRU_BLOCKS = 8
DA_HEADS = 4
DA_HEAD_DIM = 64
DA_V_DIM = 128
SSD_HEAD_DIM = 64
SSD_STATE = 128
SSD_GROUPS = 4
SSD_CHUNK = 128

TM = 256
HALO = 8
LRU_TT = 128
LRU_PITCH = 9
KV_CHUNK = 512
ATTN_TQ = 1024
INPROJ_TM = 512
INPROJ_COLS = 512
OUTPROJ_K = 256
FINISH_TM = 512
OUTPROJ_TM = 512
LANES = 128
SUBLANES = 8
VMEM_LIMIT = 56 * 1024 * 1024

HIGHEST = lax.Precision.HIGHEST
LOG2E = math.log2(math.e)
Q_SCALE = DA_HEAD_DIM ** -0.5 * LOG2E


def _sigmoid(x):
    return 0.5 * (1.0 + jnp.tanh(0.5 * x))


def _silu(x):
    h = 0.5 * x
    return h * (1.0 + jnp.tanh(h))


def _rms(x):
    return x * lax.rsqrt(jnp.mean(x * x, axis=-1, keepdims=True) + EPS)


def _cparams(sem):
    return pltpu.CompilerParams(dimension_semantics=sem, vmem_limit_bytes=VMEM_LIMIT)


def _mod_kernel(c_ref, w_ref, b_ref, o_ref):
    s = _silu(c_ref[...])
    o_ref[0] = jnp.dot(s, w_ref[0], preferred_element_type=F32, precision=HIGHEST) + b_ref[0]


def _modulation(c_rows, w_mod, b_mod):
    depth, d, n3 = w_mod.shape
    rows = c_rows.shape[0]
    nt = 1024
    return pl.pallas_call(
        _mod_kernel,
        grid=(depth, n3 // nt),
        in_specs=[
            pl.BlockSpec((rows, d), lambda l, n: (0, 0)),
            pl.BlockSpec((1, d, nt), lambda l, n: (l, 0, n)),
            pl.BlockSpec((1, 1, nt), lambda l, n: (l, 0, n)),
        ],
        out_specs=pl.BlockSpec((1, rows, nt), lambda l, n: (l, 0, n)),
        out_shape=jax.ShapeDtypeStruct((depth, rows, n3), F32),
        compiler_params=_cparams(("arbitrary", "arbitrary")),
        name="modulation",
    )(c_rows, w_mod, b_mod.reshape(depth, 1, n3))


def _normed_rows(h_ext, mod_ref, g_ref, d):
    sh = mod_ref[0, :, 0:d]
    sc = mod_ref[0, :, d:2 * d]
    u = (_rms(h_ext) * g_ref[...]) * (1.0 + sc) + sh
    return u.astype(BF16)


def _store_ext(ext_ref, slab0, x_ext, left_ok, right_ok):
    n = x_ext.shape[1] // LANES
    rows = x_ext.shape[0] - 2 * HALO
    zero = jnp.zeros((HALO, LANES), F32)
    for c in range(n):
        blk = x_ext[:, c * LANES:(c + 1) * LANES]
        ext_ref[slab0 + c, 0:HALO, :] = jnp.where(left_ok, blk[0:HALO], zero)
        ext_ref[slab0 + c, HALO:HALO + rows, :] = blk[HALO:HALO + rows]
        ext_ref[slab0 + c, HALO + rows:, :] = jnp.where(right_ok, blk[HALO + rows:], zero)


def _conv4(ext_ref, c, w_ref, b_ref):
    rows = ext_ref.shape[1] - 2 * HALO
    ls = slice(c * LANES, (c + 1) * LANES)
    acc = b_ref[:, ls] + w_ref[0:1, ls] * ext_ref[c, pl.ds(HALO - 2, rows), :]
    acc = acc + w_ref[1:2, ls] * ext_ref[c, pl.ds(HALO - 1, rows), :]
    acc = acc + w_ref[2:3, ls] * ext_ref[c, pl.ds(HALO, rows), :]
    acc = acc + w_ref[3:4, ls] * ext_ref[c, pl.ds(HALO + 1, rows), :]
    return acc


def _segment_edges(is_ctx, n_steps):
    if is_ctx:
        return jnp.bool_(False), jnp.bool_(False)
    j = pl.program_id(0)
    return j >= 1, j <= n_steps - 2


def _uni_lat_spec(rows, width, ctx_len):
    return pl.BlockSpec((pl.Element(1), pl.Element(rows), pl.Element(width)),
                        lambda j, b: (b, pl.multiple_of(ctx_len + j * rows, TM), 0))


N_INPROJ0_IN = 11
N_INPROJ0_OUT = 6


def _inproj0_kernel(is_ctx, n_steps, *refs):
    (hm_ref, hl_ref, hr_ref, mod_ref, g_ref, w_ref, cw_ref, cb_ref, rc_ref, rs1_ref, rs2_ref) = refs[:N_INPROJ0_IN]
    xc_ref, gr_ref, q_ref, k_ref, v_ref, gd_ref, ext_ref = refs[-(N_INPROJ0_OUT + 1):]
    d = hm_ref.shape[-1]
    rows = hm_ref.shape[1]
    w5 = xc_ref.shape[-1]
    h_ext = jnp.concatenate([hl_ref[0], hm_ref[0], hr_ref[0]], axis=0)
    u_ext = _normed_rows(h_ext, mod_ref, g_ref, d)
    u = u_ext[HALO:HALO + rows]

    x_ext = jnp.dot(u_ext, w_ref[:, 0:w5], preferred_element_type=F32)
    _store_ext(ext_ref, 0, x_ext, *_segment_edges(is_ctx, n_steps))
    for cs in range(w5 // LANES):
        xc_ref[0, :, cs * LANES:(cs + 1) * LANES] = _conv4(ext_ref, cs, cw_ref, cb_ref)

    gr_ref[0] = jnp.dot(u, w_ref[:, w5:2 * w5], preferred_element_type=F32).astype(BF16)
    reps = w5 // LANES
    for idx, o_ref, post in ((2, q_ref, Q_SCALE), (3, k_ref, None)):
        t = jnp.dot(u, w_ref[:, idx * w5:(idx + 1) * w5], preferred_element_type=F32)
        if not is_ctx:
            c = jnp.concatenate([rc_ref[...]] * reps, axis=1)
            s1 = jnp.concatenate([rs1_ref[...]] * reps, axis=1)
            s2 = jnp.concatenate([rs2_ref[...]] * reps, axis=1)
            t = t * c + pltpu.roll(t, w5 - 16, 1) * s1 + pltpu.roll(t, 16, 1) * s2
        o_ref[0] = (t if post is None else t * post).astype(BF16)
    v_ref[0] = jnp.dot(u, w_ref[:, 4 * w5:5 * w5], preferred_element_type=F32).astype(BF16)
    gd_ref[0] = jnp.dot(u, w_ref[:, 5 * w5:6 * w5], preferred_element_type=F32).astype(BF16)


def _halo_specs(rows, d, n_rows):
    hb = rows // HALO
    return [pl.BlockSpec((1, rows, d), lambda j, b: (b, j, 0)),
            pl.BlockSpec((1, HALO, d), lambda j, b: (b, jnp.maximum(j * hb - 1, 0), 0)),
            pl.BlockSpec((1, HALO, d), lambda j, b: (b, jnp.minimum((j + 1) * hb, n_rows // HALO - 1), 0))]


def _inproj0(x, ctx, mod, g_pre, w_in, conv_w, conv_b, rope):
    bsz, s, d = x.shape
    ctx_len = ctx.shape[1]
    t_all = ctx_len + s
    w5 = conv_w.shape[-1]
    const2 = lambda j, b: (0, 0)
    consts = [pl.BlockSpec((1, d), const2), pl.BlockSpec(w_in.shape, const2), pl.BlockSpec(conv_w.shape, const2),
              pl.BlockSpec((1, w5), const2)]
    out_shape = ([jax.ShapeDtypeStruct((bsz, t_all, w5), F32)]
                 + [jax.ShapeDtypeStruct((bsz, t_all, w5), BF16)] * (N_INPROJ0_OUT - 1))

    def call(is_ctx, src, rows, out_spec, mod_row, bufs, name):
        n_steps = src.shape[1] // rows
        rope_spec = pl.BlockSpec((rows, LANES), lambda j, b: (j, 0))
        return pl.pallas_call(
            functools.partial(_inproj0_kernel, is_ctx, n_steps),
            grid=(n_steps, bsz),
            in_specs=_halo_specs(rows, d, src.shape[1])
            + [pl.BlockSpec((1, 1, 3 * d), lambda j, b: (mod_row(b), 0, 0))] + consts + [rope_spec] * 3
            + [pl.BlockSpec(memory_space=pl.ANY)] * len(bufs),
            out_specs=[out_spec] * N_INPROJ0_OUT,
            out_shape=out_shape,
            input_output_aliases={N_INPROJ0_IN + i: i for i in range(len(bufs))},
            scratch_shapes=[pltpu.VMEM((w5 // LANES, rows + 2 * HALO, LANES), F32)],
            compiler_params=_cparams(("arbitrary", "arbitrary")),
            name=name,
        )(src, src, src, mod, g_pre.reshape(1, d), w_in, conv_w, conv_b.reshape(1, w5), *rope, *bufs)

    outs = call(False, x, INPROJ_TM, _uni_lat_spec(INPROJ_TM, w5, ctx_len), lambda b: b, [], "inproj0")
    ctx_spec = pl.BlockSpec((1, ctx_len, w5), lambda j, b: (b, 0, 0))
    return call(True, ctx, ctx_len, ctx_spec, lambda b: bsz, outs, "inproj0_ctx")


def _sqrt_pos(y):
    return jnp.where(y > 0.0, y * lax.rsqrt(y), 0.0)


def _lru_kernel(bsz, xf_ref, xb_ref, wg_ref, bias_ref, k4_ref, of_ref, ob_ref, g_scr, a_scr, b_scr, hs_scr, h_scr):
    i = pl.program_id(0)
    w5 = xf_ref.shape[-1]
    nslab = w5 // LANES
    tt = LRU_TT

    @pl.when(i == 0)
    def _():
        h_scr[...] = jnp.zeros_like(h_scr)

    for dr, x_ref in enumerate((xf_ref, xb_ref)):
        k4 = k4_ref[dr:dr + 1, :]
        g_scr[...] = jnp.dot(x_ref[...].reshape(bsz * tt, w5).astype(BF16), wg_ref[dr], preferred_element_type=F32)
        for b in range(bsz):
            xc = x_ref[b]
            tr = jnp.tanh(g_scr[b * tt:(b + 1) * tt, 0:w5] + bias_ref[2 * dr:2 * dr + 1, :])
            ti = jnp.tanh(g_scr[b * tt:(b + 1) * tt, w5:2 * w5] + bias_ref[2 * dr + 1:2 * dr + 2, :])
            nla = k4 * tr + k4
            a = jnp.exp2(nla * (-LOG2E))
            bc = _sqrt_pos(jnp.tanh(nla) * (a * a + 1.0)) * ((0.5 * xc) * (1.0 + ti))
            for c in range(nslab):
                a_scr[dr, c, pl.ds(b, tt, stride=LRU_PITCH), :] = a[:, c * LANES:(c + 1) * LANES]
                b_scr[dr, c, pl.ds(b, tt, stride=LRU_PITCH), :] = bc[:, c * LANES:(c + 1) * LANES]

    def step(t, carry):
        hf, hb = carry
        rf = t * LRU_PITCH
        rb = (tt - 1 - t) * LRU_PITCH
        nf, nb = [], []
        for c in range(nslab):
            h = a_scr[0, c, pl.ds(rf, bsz), :] * hf[c] + b_scr[0, c, pl.ds(rf, bsz), :]
            hs_scr[0, c, pl.ds(rf, bsz), :] = h
            nf.append(h)
            h = a_scr[1, c, pl.ds(rb, bsz), :] * hb[c] + b_scr[1, c, pl.ds(rb, bsz), :]
            hs_scr[1, c, pl.ds(rb, bsz), :] = h
            nb.append(h)
        return tuple(nf), tuple(nb)

    h0f = tuple(h_scr[0, c] for c in range(nslab))
    h0b = tuple(h_scr[1, c] for c in range(nslab))
    hf, hb = lax.fori_loop(0, tt, step, (h0f, h0b), unroll=8)
    for c in range(nslab):
        h_scr[0, c] = hf[c]
        h_scr[1, c] = hb[c]

    for dr, o_ref in enumerate((of_ref, ob_ref)):
        for b in range(bsz):
            for c in range(nslab):
                o_ref[b, :, c * LANES:(c + 1) * LANES] = hs_scr[dr, c, pl.ds(b, tt, stride=LRU_PITCH), :].astype(BF16)


def _bwd_tile(i, n_ctx, n_all):
    return jnp.where(i < n_ctx, n_ctx - 1 - i, n_all - 1 - (i - n_ctx))


def _lru(xc, wg, bias, k4, ctx_len):
    bsz, t_all, w5 = xc.shape
    tt = LRU_TT
    n_all = t_all // tt
    n_ctx = ctx_len // tt
    nslab = w5 // LANES
    fwd_spec = pl.BlockSpec((bsz, tt, w5), lambda i: (0, i, 0))
    bwd_spec = pl.BlockSpec((bsz, tt, w5), lambda i: (0, _bwd_tile(i, n_ctx, n_all), 0))
    const = lambda shape: pl.BlockSpec(shape, lambda i: (0,) * len(shape))
    assert bsz <= LRU_PITCH
    coef = pltpu.VMEM((2, nslab, tt * LRU_PITCH, LANES), F32)
    return pl.pallas_call(
        functools.partial(_lru_kernel, bsz),
        grid=(n_all,),
        in_specs=[fwd_spec, bwd_spec, const(wg.shape), const(bias.shape), const(k4.shape)],
        out_specs=[fwd_spec, bwd_spec],
        out_shape=[jax.ShapeDtypeStruct((bsz, t_all, w5), BF16)] * 2,
        scratch_shapes=[pltpu.VMEM((bsz * tt, 2 * w5), F32), coef, coef, coef,
                        pltpu.VMEM((2, nslab, bsz, LANES), F32)],
        compiler_params=_cparams(("arbitrary",)),
        name="rglru",
    )(xc, xc, wg, bias, k4)


def _attn_kernel(lambda_init, n_streams, key_chunks, q_ref, k_ref, v_ref, lam_ref, sub_ref, o_ref, m_scr, acc_scr):
    qqs = []
    for st in range(n_streams):
        q = q_ref[0, st * TM:(st + 1) * TM, :]
        lane = lax.broadcasted_iota(jnp.int32, q.shape, 1)
        zero = jnp.zeros_like(q)
        qqs.append(jnp.concatenate([jnp.where(lane < DA_HEAD_DIM, q, zero),
                                    jnp.where(lane >= DA_HEAD_DIM, q, zero)], axis=0))
    m_scr[...] = jnp.full_like(m_scr, -jnp.inf)
    acc_scr[...] = jnp.zeros_like(acc_scr)

    for start, size in key_chunks:
        kc = k_ref[0, start:start + size, :]
        va = jnp.concatenate([v_ref[0, start:start + size, :], jnp.ones((size, LANES), BF16)], axis=1)
        for st in range(n_streams):
            s = lax.dot_general(qqs[st], kc, (((1,), (1,)), ((), ())), preferred_element_type=F32)
            m_prev = m_scr[st]
            m_next = jnp.maximum(m_prev, jnp.max(s, axis=1, keepdims=True))
            p = jnp.exp2(s - jnp.concatenate([m_next] * (size // LANES), axis=1))
            alpha = jnp.exp2(m_prev - m_next)
            acc_scr[st] = (acc_scr[st] * jnp.concatenate([alpha, alpha], axis=1)
                           + jnp.dot(p.astype(BF16), va, preferred_element_type=F32))
            m_scr[st] = m_next

    lm = lam_ref[...]
    lam = (jnp.exp(jnp.sum(lm[0:1] * lm[1:2], axis=1, keepdims=True))
           - jnp.exp(jnp.sum(lm[2:3] * lm[3:4], axis=1, keepdims=True)) + lambda_init)
    for st in range(n_streams):
        o = (acc_scr[st, 0:TM, 0:LANES] / acc_scr[st, 0:TM, LANES:2 * LANES]
             - lam * (acc_scr[st, TM:2 * TM, 0:LANES] / acc_scr[st, TM:2 * TM, LANES:2 * LANES]))
        o_ref[0, st * TM:(st + 1) * TM, :] = ((_rms(o) * sub_ref[...]) * (1.0 - lambda_init)).astype(BF16)


def _attention(q, k, v, da_lambda, da_subln, lambda_init, ctx_len):
    bsz, t_all, w = q.shape
    hd = DA_V_DIM
    n_lat = t_all - ctx_len
    small = [pl.BlockSpec(da_lambda.shape, lambda b, h, j: (0, 0)), pl.BlockSpec((1, hd), lambda b, h, j: (0, 0))]
    sub = da_subln.reshape(1, hd)

    def scratch(n_streams):
        return [pltpu.VMEM((n_streams, 2 * TM, LANES), F32), pltpu.VMEM((n_streams, 2 * TM, 2 * LANES), F32)]

    n_streams = ATTN_TQ // TM
    chunks = [(0, ctx_len)] + [(ctx_len + c * KV_CHUNK, KV_CHUNK) for c in range(n_lat // KV_CHUNK)]
    q_lat = pl.BlockSpec((pl.Element(1), pl.Element(ATTN_TQ), pl.Element(hd)),
                         lambda b, h, j: (b, pl.multiple_of(ctx_len + j * ATTN_TQ, TM), pl.multiple_of(h * hd, hd)))
    kv_all = pl.BlockSpec((1, t_all, hd), lambda b, h, j: (b, 0, h))
    on_lat = pl.pallas_call(
        functools.partial(_attn_kernel, lambda_init, n_streams, chunks),
        grid=(bsz, DA_HEADS, n_lat // ATTN_TQ),
        in_specs=[q_lat, kv_all, kv_all] + small,
        out_specs=pl.BlockSpec((1, ATTN_TQ, hd), lambda b, h, j: (b, j, h)),
        out_shape=jax.ShapeDtypeStruct((bsz, n_lat, w), BF16),
        scratch_shapes=scratch(n_streams),
        compiler_params=_cparams(("arbitrary", "arbitrary", "arbitrary")),
        name="diffattn",
    )(q, k, v, da_lambda, sub)

    ctx_spec = pl.BlockSpec((1, ctx_len, hd), lambda b, h, j: (b, 0, h))
    on_ctx = pl.pallas_call(
        functools.partial(_attn_kernel, lambda_init, ctx_len // TM, [(0, ctx_len)]),
        grid=(bsz, DA_HEADS, 1),
        in_specs=[ctx_spec, ctx_spec, ctx_spec] + small,
        out_specs=ctx_spec,
        out_shape=jax.ShapeDtypeStruct((bsz, ctx_len, w), BF16),
        scratch_shapes=scratch(ctx_len // TM),
        compiler_params=_cparams(("arbitrary", "arbitrary", "arbitrary")),
        name="diffattn_ctx",
    )(q, k, v, da_lambda, sub)
    return on_lat, on_ctx


def _outproj0_kernel(hf_ref, hb_ref, gr_ref, on_ref, gd_ref, res_ref, mod_ref, g_ref, w_ref, o_ref):
    d = o_ref.shape[-1]
    w5 = hf_ref.shape[-1]
    kc = OUTPROJ_K
    gt = mod_ref[0, :, 2 * d:3 * d]
    for r0 in range(0, o_ref.shape[1], TM):
        rs = slice(r0, r0 + TM)
        y = None
        for c in range(w5 // kc):
            sl = slice(c * kc, (c + 1) * kc)
            r = hf_ref[0, rs, sl].astype(F32) + hb_ref[0, rs, sl].astype(F32)
            m1 = (r * _silu(gr_ref[0, rs, sl].astype(F32))).astype(BF16)
            m2 = (on_ref[0, rs, sl].astype(F32) * _silu(gd_ref[0, rs, sl].astype(F32))).astype(BF16)
            t = jnp.dot(m1, w_ref[c * kc:(c + 1) * kc, :], preferred_element_type=F32)
            t = t + jnp.dot(m2, w_ref[w5 + c * kc:w5 + (c + 1) * kc, :], preferred_element_type=F32)
            y = t if y is None else y + t
        o_ref[0, rs, :] = res_ref[0, rs, :] + gt * (_rms(y) * g_ref[...])


def _outproj0(hf, hb, gr, on_lat, on_ctx, gd, x, ctx, mod, g_post, w_out):
    bsz, t_all, w5 = hf.shape
    d = x.shape[-1]
    ctx_len = ctx.shape[1]
    const2 = lambda j, b: (0, 0)
    consts = [pl.BlockSpec((1, d), const2), pl.BlockSpec(w_out.shape, const2)]

    def call(rows, n_steps, uni, on, res, mod_row, name):
        own = lambda width: pl.BlockSpec((1, rows, width), lambda j, b: (b, j, 0))
        return pl.pallas_call(
            _outproj0_kernel,
            grid=(n_steps, bsz),
            in_specs=[uni, uni, uni, own(w5), uni, own(d),
                      pl.BlockSpec((1, 1, 3 * d), lambda j, b: (mod_row(b), 0, 0))] + consts,
            out_specs=own(d),
            out_shape=jax.ShapeDtypeStruct((bsz, n_steps * rows, d), F32),
            compiler_params=_cparams(("arbitrary", "arbitrary")),
            name=name,
        )(hf, hb, gr, on, gd, res, mod, g_post.reshape(1, d), w_out)

    rows = OUTPROJ_TM
    lat_uni = pl.BlockSpec((pl.Element(1), pl.Element(rows), pl.Element(w5)),
                           lambda j, b: (b, pl.multiple_of(ctx_len + j * rows, TM), 0))
    h_lat = call(rows, (t_all - ctx_len) // rows, lat_uni, on_lat, x, lambda b: b, "outproj0")
    ctx_uni = pl.BlockSpec((1, ctx_len, w5), lambda j, b: (b, 0, 0))
    h_ctx = call(ctx_len, 1, ctx_uni, on_ctx, ctx, lambda b: bsz, "outproj0_ctx")
    return h_lat, h_ctx


N_INPROJ1_IN = 11
N_INPROJ1_OUT = 3


def _inproj1_kernel(is_ctx, n_steps, *refs):
    (hm_ref, hl_ref, hr_ref, mod_ref, g_ref, wz_ref, wx_ref, wd_ref, cw_ref, cb_ref, db_ref) = refs[:N_INPROJ1_IN]
    z_ref, xbc_ref, dt_ref, ext_ref = refs[-(N_INPROJ1_OUT + 1):]
    d = hm_ref.shape[-1]
    rows = hm_ref.shape[1]
    h_ext = jnp.concatenate([hl_ref[0], hm_ref[0], hr_ref[0]], axis=0)
    u_ext = _normed_rows(h_ext, mod_ref, g_ref, d)
    u = u_ext[HALO:HALO + rows]
    left_ok, right_ok = _segment_edges(is_ctx, n_steps)
    nz, nx = wz_ref.shape[1], wx_ref.shape[1]
    cw = INPROJ_COLS
    spc = cw // LANES
    n_x, n_z = nx // cw, nz // cw
    for i in range(max(n_x, n_z)):
        if i < n_x:
            x_ext = jnp.dot(u_ext, wx_ref[:, i * cw:(i + 1) * cw], preferred_element_type=F32)
            _store_ext(ext_ref, i * spc, x_ext, left_ok, right_ok)
            for cs in range(i * spc, (i + 1) * spc):
                xbc_ref[0, :, cs * LANES:(cs + 1) * LANES] = _silu(_conv4(ext_ref, cs, cw_ref, cb_ref)).astype(BF16)
        if i < n_z:
            z_ref[0, :, i * cw:(i + 1) * cw] = _silu(jnp.dot(u, wz_ref[:, i * cw:(i + 1) * cw],
                                                             preferred_element_type=F32)).astype(BF16)
    dt_ref[0] = jax.nn.softplus(jnp.dot(u, wd_ref[...], preferred_element_type=F32) + db_ref[...])


def _inproj1(h_lat, h_ctx, mod, g_pre, wz, wx, wd, conv_w, conv_b, dt_bias):
    bsz, s, d = h_lat.shape
    ctx_len = h_ctx.shape[1]
    t_all = ctx_len + s
    nz, nx, nd = wz.shape[1], wx.shape[1], wd.shape[1]
    const2 = lambda j, b: (0, 0)
    consts = [pl.BlockSpec((1, d), const2), pl.BlockSpec(wz.shape, const2), pl.BlockSpec(wx.shape, const2),
              pl.BlockSpec(wd.shape, const2), pl.BlockSpec(conv_w.shape, const2), pl.BlockSpec((1, nx), const2),
              pl.BlockSpec((1, nd), const2)]
    out_shape = [jax.ShapeDtypeStruct((bsz, t_all, nz), BF16), jax.ShapeDtypeStruct((bsz, t_all, nx), BF16),
                 jax.ShapeDtypeStruct((bsz, t_all, nd), F32)]

    def call(is_ctx, src, rows, out_spec, mod_row, bufs, name):
        n_steps = src.shape[1] // rows
        return pl.pallas_call(
            functools.partial(_inproj1_kernel, is_ctx, n_steps),
            grid=(n_steps, bsz),
            in_specs=_halo_specs(rows, d, src.shape[1])
            + [pl.BlockSpec((1, 1, 3 * d), lambda j, b: (mod_row(b), 0, 0))] + consts
            + [pl.BlockSpec(memory_space=pl.ANY)] * len(bufs),
            out_specs=[out_spec(w) for w in (nz, nx, nd)],
            out_shape=out_shape,
            input_output_aliases={N_INPROJ1_IN + i: i for i in range(len(bufs))},
            scratch_shapes=[pltpu.VMEM((nx // LANES, rows + 2 * HALO, LANES), F32)],
            compiler_params=_cparams(("arbitrary", "arbitrary")),
            name=name,
        )(src, src, src, mod, g_pre.reshape(1, d), wz, wx, wd, conv_w, conv_b.reshape(1, nx), dt_bias, *bufs)

    outs = call(False, h_lat, INPROJ_TM, lambda w: _uni_lat_spec(INPROJ_TM, w, ctx_len), lambda b: b, [], "inproj1")
    return call(True, h_ctx, ctx_len, lambda w: pl.BlockSpec((1, ctx_len, w), lambda j, b: (b, 0, 0)),
                lambda b: bsz, outs, "inproj1_ctx")


def _ssd_prologue(reverse, dr, dt_ref, alog_ref):
    ch = SSD_CHUNK
    dt = dt_ref[0]
    adt = dt * (-jnp.exp(alog_ref[dr:dr + 1, :]) * LOG2E)
    row = lax.broadcasted_iota(jnp.int32, (ch, ch), 0)
    col = lax.broadcasted_iota(jnp.int32, (ch, ch), 1)
    mask = (row <= col) if reverse else (row >= col)
    cs = jnp.dot(mask.astype(F32), adt, preferred_element_type=F32, precision=HIGHEST)
    last = 0 if reverse else ch - 1
    cs_t = cs.T
    dt_t = dt.T
    w_t = jnp.exp2(cs_t[:, last:last + 1] - cs_t) * dt_t
    crow = cs_t - jnp.log2(dt_t)
    e_tot = jnp.exp2(cs[last:last + 1, :])
    return mask, cs, w_t, crow, e_tot


def _ssd_group(dr, g, xbc_ref, s_scr, inner):
    gn = SSD_GROUPS * SSD_STATE
    bg = xbc_ref[0, :, inner + g * SSD_STATE:inner + (g + 1) * SSD_STATE]
    cg = xbc_ref[0, :, inner + gn + g * SSD_STATE:inner + gn + (g + 1) * SSD_STATE]
    cb = lax.dot_general(cg, bg, (((1,), (1,)), ((), ())), preferred_element_type=F32).astype(BF16)
    bg_t = bg.astype(F32).T.astype(BF16)
    s_g = s_scr[dr, g]
    y_off = jnp.dot(cg, s_g.astype(BF16), preferred_element_type=F32)
    return cb, bg_t, s_g, y_off


def _ssd_pair(dr, g, pp, n_heads, pro, grp, xbc_ref, s_scr, y_ref):
    ch = SSD_CHUNK
    mask, cs, w_t, crow, e_tot = pro
    cb, bg_t, s_g, y_off = grp
    pairs_per_group = n_heads // SSD_GROUPS // 2
    p = g * pairs_per_group + pp
    h1 = dr * n_heads + 2 * p
    ls = slice(pp * LANES, (pp + 1) * LANES)
    left = lax.broadcasted_iota(jnp.int32, (ch, LANES), 1) < SSD_HEAD_DIM
    x2 = xbc_ref[0, :, p * LANES:(p + 1) * LANES]
    zero = jnp.zeros_like(x2)
    wx = jnp.concatenate([jnp.where(left, x2, zero), jnp.where(left, zero, x2)], axis=0)
    ms, bws, cols = [], [], []
    for h in (h1, h1 + 1):
        ccol = jnp.broadcast_to(cs[:, h:h + 1], (ch, ch))
        cols.append(ccol)
        ms.append(cb * jnp.exp2(jnp.where(mask, ccol - crow[h:h + 1, :], -jnp.inf)).astype(BF16))
        bws.append(bg_t * w_t[h:h + 1, :].astype(BF16))
    lhs = jnp.concatenate([jnp.concatenate(ms, axis=1), jnp.concatenate(bws, axis=1)], axis=0)
    r = jnp.dot(lhs, wx, preferred_element_type=F32)
    y = r[0:ch] + jnp.exp2(jnp.where(left, cols[0], cols[1])) * y_off[:, ls]
    y_ref[0, :, p * LANES:(p + 1) * LANES] = y.astype(BF16)
    dec = jnp.where(left[0:1, :], e_tot[:, h1:h1 + 1], e_tot[:, h1 + 1:h1 + 2])
    s_scr[dr, g, :, ls] = s_g[:, ls] * dec + r[ch:2 * ch]


def _ssd_kernel(xf_ref, df_ref, xb_ref, db_ref, alog_ref, yf_ref, yb_ref, s_scr):
    @pl.when(pl.program_id(1) == 0)
    def _():
        s_scr[...] = jnp.zeros_like(s_scr)

    inner = yf_ref.shape[-1]
    n_heads = inner // SSD_HEAD_DIM
    dirs = ((xf_ref, df_ref, yf_ref, False), (xb_ref, db_ref, yb_ref, True))
    pros = [_ssd_prologue(rev, dr, d_ref, alog_ref) for dr, (_, d_ref, _, rev) in enumerate(dirs)]
    for g in range(SSD_GROUPS):
        grps = [_ssd_group(dr, g, x_ref, s_scr, inner) for dr, (x_ref, _, _, _) in enumerate(dirs)]
        for pp in range(n_heads // SSD_GROUPS // 2):
            for dr, (x_ref, _, y_ref, _) in enumerate(dirs):
                _ssd_pair(dr, g, pp, n_heads, pros[dr], grps[dr], x_ref, s_scr, y_ref)


def _ssd(xbc, dt, a_log, ctx_len):
    bsz, t_all, nx = xbc.shape
    nd = dt.shape[-1]
    n_heads = a_log.shape[-1]
    inner = n_heads * SSD_HEAD_DIM
    a_log = jnp.stack([jnp.pad(a_log[dr], (dr * n_heads, nd - (dr + 1) * n_heads)) for dr in range(2)])
    ch = SSD_CHUNK
    n_all = t_all // ch
    n_ctx = ctx_len // ch
    fwd = lambda b, i: (b, i, 0)
    bwd = lambda b, i: (b, _bwd_tile(i, n_ctx, n_all), 0)
    return pl.pallas_call(
        _ssd_kernel,
        grid=(bsz, n_all),
        in_specs=[pl.BlockSpec((1, ch, nx), fwd), pl.BlockSpec((1, ch, nd), fwd),
                  pl.BlockSpec((1, ch, nx), bwd), pl.BlockSpec((1, ch, nd), bwd),
                  pl.BlockSpec(a_log.shape, lambda b, i: (0, 0))],
        out_specs=[pl.BlockSpec((1, ch, inner), fwd), pl.BlockSpec((1, ch, inner), bwd)],
        out_shape=[jax.ShapeDtypeStruct((bsz, t_all, inner), BF16)] * 2,
        scratch_shapes=[pltpu.VMEM((2, SSD_GROUPS, SSD_STATE, inner // SSD_GROUPS), F32)],
        compiler_params=_cparams(("arbitrary", "arbitrary")),
        name="ssd",
    )(xbc, dt, xbc, dt, a_log)


def _finish1_kernel(yf_ref, yb_ref, xs_ref, gz_ref, h_ref, mod_ref, dsk_ref, nw_ref, w_ref, g_ref, o_ref):
    d = o_ref.shape[-1]
    inner = yf_ref.shape[-1]
    gw = inner // SSD_GROUPS
    gt = mod_ref[0, :, 2 * d:3 * d]
    for r0 in range(0, o_ref.shape[1], TM):
        rs = slice(r0, r0 + TM)
        out = None
        for g in range(SSD_GROUPS):
            sl = slice(g * gw, (g + 1) * gw)
            y = (yf_ref[0, rs, sl] + yb_ref[0, rs, sl]).astype(F32) + dsk_ref[:, sl] * xs_ref[0, rs, sl].astype(F32)
            y = y * gz_ref[0, rs, sl].astype(F32)
            t = jnp.dot((_rms(y) * nw_ref[:, sl]).astype(BF16), w_ref[sl, :], preferred_element_type=F32)
            out = t if out is None else out + t
        o_ref[0, rs, :] = h_ref[0, rs, :] + gt * (_rms(out) * g_ref[...])


def _finish1(yf, yb, xbc, gz, h, mod, d_skip, norm_w, w_out, g_post, ctx_len):
    bsz, t_all, inner = yf.shape
    d = h.shape[-1]
    rows = FINISH_TM
    n_lat = (t_all - ctx_len) // rows

    def lat(width):
        return pl.BlockSpec((pl.Element(1), pl.Element(rows), pl.Element(width)),
                            lambda j, b: (b, pl.multiple_of(ctx_len + j * rows, TM), 0))

    const2 = lambda j, b: (0, 0)
    return pl.pallas_call(
        _finish1_kernel,
        grid=(n_lat, bsz),
        in_specs=[lat(inner)] * 4 + [
            pl.BlockSpec((1, rows, d), lambda j, b: (b, j, 0)),
            pl.BlockSpec((1, 1, 3 * d), lambda j, b: (b, 0, 0)),
            pl.BlockSpec((1, inner), const2),
            pl.BlockSpec((1, inner), const2),
            pl.BlockSpec(w_out.shape, const2),
            pl.BlockSpec((1, d), const2),
        ],
        out_specs=pl.BlockSpec((1, rows, d), lambda j, b: (b, j, 0)),
        out_shape=jax.ShapeDtypeStruct((bsz, n_lat * rows, d), F32),
        compiler_params=_cparams(("arbitrary", "arbitrary")),
        name="finish1",
    )(yf, yb, xbc, gz, h, mod, d_skip, norm_w, w_out, g_post.reshape(1, d))


def _rope_tables(n_tokens):
    rows = n_tokens // GRID_W
    row = jnp.repeat(jnp.arange(rows, dtype=F32), GRID_W)
    col = jnp.tile(jnp.arange(GRID_W, dtype=F32), rows)
    n_freq = DA_HEAD_DIM // 4
    inv = ROPE_BASE ** (-jnp.arange(n_freq, dtype=F32) / n_freq)
    ang = jnp.concatenate([row[:, None] * inv, col[:, None] * inv], axis=-1)
    cos, sin = jnp.cos(ang), jnp.sin(ang)
    cr, cc, sr, sc = cos[:, :n_freq], cos[:, n_freq:], sin[:, :n_freq], sin[:, n_freq:]
    zr = jnp.zeros_like(sr)
    c64 = jnp.concatenate([cr, cr, cc, cc], axis=1)
    s1 = jnp.concatenate([-sr, zr, -sc, zr], axis=1)
    s2 = jnp.concatenate([zr, sr, zr, sc], axis=1)
    reps = LANES // DA_HEAD_DIM
    return tuple(jnp.tile(t, (1, reps)) for t in (c64, s1, s2))


def _block_diag(w):
    n, c, d = w.shape
    eye = jnp.eye(n, dtype=w.dtype)
    return (eye[:, None, :, None] * w[:, :, None, :]).reshape(n * c, n * d)


def kernel(x, c, ctx, c_ctx, w_mod, b_mod, g_pre, g_post, e_w_in, e_w_out, lru_conv_w, lru_conv_b, lru_w_r, lru_b_r, lru_w_i, lru_b_i, lru_lambda, da_lambda, da_subln, o_w_in, o_w_out, ssd_conv_w, ssd_conv_b, ssd_a_log, ssd_dt_bias, ssd_d, ssd_norm):
    bsz, s, d = x.shape
    ctx_len = ctx.shape[1]
    assert bsz == SUBLANES and ctx_len == TM and s % KV_CHUNK == 0 and w_mod.shape[0] == 2
    assert e_w_in.shape[0] == 1 and o_w_in.shape[0] == 1

    n_rows = 2 * SUBLANES
    c_rows = jnp.concatenate([c, c_ctx[None, :], jnp.zeros((n_rows - bsz - 1, d), F32)], axis=0)
    mod = _modulation(c_rows, w_mod, b_mod)
    mod0 = mod[0].reshape(n_rows, 1, 3 * d)
    mod1 = mod[1].reshape(n_rows, 1, 3 * d)

    w5 = lru_conv_w.shape[-1]
    w_in0 = e_w_in[0].astype(BF16)
    xc, gr, q, k, v, gd = _inproj0(x, ctx, mod0, g_pre[0], w_in0, lru_conv_w[0], lru_conv_b[0], _rope_tables(s))

    wg = (0.5 * jnp.stack([jnp.concatenate([_block_diag(lru_w_r[0, dr]), _block_diag(lru_w_i[0, dr])], axis=1)
                           for dr in range(2)])).astype(BF16)
    bias = 0.5 * jnp.stack([lru_b_r[0, 0], lru_b_i[0, 0], lru_b_r[0, 1], lru_b_i[0, 1]])
    k4 = (0.5 * LRU_C) * jax.nn.softplus(-lru_lambda[0])
    hf, hb = _lru(xc, wg, bias, k4, ctx_len)

    lambda_init = 0.8 - 0.6 * math.exp(-0.3 * 0)
    on_lat, on_ctx = _attention(q, k, v, da_lambda[0], da_subln[0], lambda_init, ctx_len)
    h1_lat, h1_ctx = _outproj0(hf, hb, gr, on_lat, on_ctx, gd, x, ctx, mod0, g_post[0], e_w_out[0].astype(BF16))

    n_heads = ssd_a_log.shape[-1]
    inner = n_heads * SSD_HEAD_DIM
    nx = ssd_conv_w.shape[-1]
    w1 = o_w_in[0]
    wz = w1[:, :inner].astype(BF16)
    wx = w1[:, inner:inner + nx].astype(BF16)
    nd = 2 * n_heads
    wd = jnp.pad(w1[:, inner + nx:], ((0, 0), (0, LANES - nd))).astype(BF16)
    dt_bias = jnp.pad(ssd_dt_bias[0].reshape(1, nd), ((0, 0), (0, LANES - nd)))
    gz, xbc, dt = _inproj1(h1_lat, h1_ctx, mod1, g_pre[1], wz, wx, wd, ssd_conv_w[0], ssd_conv_b[0], dt_bias)
    yf, yb = _ssd(xbc, dt, ssd_a_log[0], ctx_len)
    d_skip = jnp.repeat(ssd_d[0], SSD_HEAD_DIM).reshape(1, inner)
    return _finish1(yf, yb, xbc, gz, h1_lat, mod1, d_skip, ssd_norm[0].reshape(1, inner),
                    o_w_out[0].astype(BF16), g_post[1], ctx_len)
```

```python
import functools
import math

import jax
import jax.numpy as jnp
from jax import lax
from jax.experimental import pallas as pl
from jax.experimental.pallas import tpu as pltpu

F32 = jnp.float32
BF16 = jnp.bfloat16

EPS = 1e-6
GRID_W = 64
ROPE_BASE = 10000.0
LRU_C = 8.0
LRU_BLOCKS = 8
DA_HEADS = 4
DA_HEAD_DIM = 64
DA_V_DIM = 128
SSD_HEAD_DIM = 64
SSD_STATE = 128
SSD_GROUPS = 4
SSD_CHUNK = 128

TM = 256
HALO = 8
LRU_TT = 128
LRU_PITCH = 9
KV_CHUNK = 512
ATTN_TQ = 1024
INPROJ_TM = 512
INPROJ_COLS = 512
OUTPROJ_K = 256
FINISH_TM = 512
OUTPROJ_TM = 512
LANES = 128
SUBLANES = 8
VMEM_LIMIT = 56 * 1024 * 1024

HIGHEST = lax.Precision.HIGHEST
LOG2E = math.log2(math.e)
Q_SCALE = DA_HEAD_DIM ** -0.5 * LOG2E


def _sigmoid(x):
    return 0.5 * (1.0 + jnp.tanh(0.5 * x))


def _silu(x):
    h = 0.5 * x
    return h * (1.0 + jnp.tanh(h))


def _rms(x):
    return x * lax.rsqrt(jnp.mean(x * x, axis=-1, keepdims=True) + EPS)


def _cparams(sem):
    return pltpu.CompilerParams(dimension_semantics=sem, vmem_limit_bytes=VMEM_LIMIT)


def _mod_kernel(c_ref, w_ref, b_ref, o_ref):
    s = _silu(c_ref[...])
    o_ref[0] = jnp.dot(s, w_ref[0], preferred_element_type=F32, precision=HIGHEST) + b_ref[0]


def _modulation(c_rows, w_mod, b_mod):
    depth, d, n3 = w_mod.shape
    rows = c_rows.shape[0]
    nt = 1024
    return pl.pallas_call(
        _mod_kernel,
        grid=(depth, n3 // nt),
        in_specs=[
            pl.BlockSpec((rows, d), lambda l, n: (0, 0)),
            pl.BlockSpec((1, d, nt), lambda l, n: (l, 0, n)),
            pl.BlockSpec((1, 1, nt), lambda l, n: (l, 0, n)),
        ],
        out_specs=pl.BlockSpec((1, rows, nt), lambda l, n: (l, 0, n)),
        out_shape=jax.ShapeDtypeStruct((depth, rows, n3), F32),
        compiler_params=_cparams(("arbitrary", "arbitrary")),
        name="modulation",
    )(c_rows, w_mod, b_mod.reshape(depth, 1, n3))


def _normed_rows(h_ext, mod_ref, g_ref, d):
    sh = mod_ref[0, :, 0:d]
    sc = mod_ref[0, :, d:2 * d]
    u = (_rms(h_ext) * g_ref[...]) * (1.0 + sc) + sh
    return u.astype(BF16)


def _store_ext(ext_ref, slab0, x_ext, left_ok, right_ok):
    n = x_ext.shape[1] // LANES
    rows = x_ext.shape[0] - 2 * HALO
    zero = jnp.zeros((HALO, LANES), F32)
    for c in range(n):
        blk = x_ext[:, c * LANES:(c + 1) * LANES]
        ext_ref[slab0 + c, 0:HALO, :] = jnp.where(left_ok, blk[0:HALO], zero)
        ext_ref[slab0 + c, HALO:HALO + rows, :] = blk[HALO:HALO + rows]
        ext_ref[slab0 + c, HALO + rows:, :] = jnp.where(right_ok, blk[HALO + rows:], zero)


def _conv4(ext_ref, c, w_ref, b_ref):
    rows = ext_ref.shape[1] - 2 * HALO
    ls = slice(c * LANES, (c + 1) * LANES)
    acc = b_ref[:, ls] + w_ref[0:1, ls] * ext_ref[c, pl.ds(HALO - 2, rows), :]
    acc = acc + w_ref[1:2, ls] * ext_ref[c, pl.ds(HALO - 1, rows), :]
    acc = acc + w_ref[2:3, ls] * ext_ref[c, pl.ds(HALO, rows), :]
    acc = acc + w_ref[3:4, ls] * ext_ref[c, pl.ds(HALO + 1, rows), :]
    return acc


def _segment_edges(is_ctx, n_steps):
    if is_ctx:
        return jnp.bool_(False), jnp.bool_(False)
    j = pl.program_id(0)
    return j >= 1, j <= n_steps - 2


def _uni_lat_spec(rows, width, ctx_len):
    return pl.BlockSpec((pl.Element(1), pl.Element(rows), pl.Element(width)),
                        lambda j, b: (b, pl.multiple_of(ctx_len + j * rows, TM), 0))


def _inproj0_kernel(is_ctx, n_steps, hm_ref, hl_ref, hr_ref, mod_ref, g_ref, w_ref, cw_ref, cb_ref,
                    rc_ref, rs1_ref, rs2_ref, xc_ref, gr_ref, q_ref, k_ref, v_ref, gd_ref, ext_ref):
    d = hm_ref.shape[-1]
    rows = hm_ref.shape[1]
    w5 = xc_ref.shape[-1]
    h_ext = jnp.concatenate([hl_ref[0], hm_ref[0], hr_ref[0]], axis=0)
    u_ext = _normed_rows(h_ext, mod_ref, g_ref, d)
    u = u_ext[HALO:HALO + rows]

    x_ext = jnp.dot(u_ext, w_ref[:, 0:w5], preferred_element_type=F32)
    _store_ext(ext_ref, 0, x_ext, *_segment_edges(is_ctx, n_steps))
    for cs in range(w5 // LANES):
        xc_ref[0, :, cs * LANES:(cs + 1) * LANES] = _conv4(ext_ref, cs, cw_ref, cb_ref)

    gr_ref[0] = jnp.dot(u, w_ref[:, w5:2 * w5], preferred_element_type=F32).astype(BF16)
    reps = w5 // LANES
    for idx, o_ref, post in ((2, q_ref, Q_SCALE), (3, k_ref, None)):
        t = jnp.dot(u, w_ref[:, idx * w5:(idx + 1) * w5], preferred_element_type=F32)
        if not is_ctx:
            c = jnp.concatenate([rc_ref[...]] * reps, axis=1)
            s1 = jnp.concatenate([rs1_ref[...]] * reps, axis=1)
            s2 = jnp.concatenate([rs2_ref[...]] * reps, axis=1)
            t = t * c + pltpu.roll(t, w5 - 16, 1) * s1 + pltpu.roll(t, 16, 1) * s2
        o_ref[0] = (t if post is None else t * post).astype(BF16)
    v_ref[0] = jnp.dot(u, w_ref[:, 4 * w5:5 * w5], preferred_element_type=F32).astype(BF16)
    gd_ref[0] = jnp.dot(u, w_ref[:, 5 * w5:6 * w5], preferred_element_type=F32).astype(BF16)


def _halo_specs(rows, d, n_rows):
    hb = rows // HALO
    return [pl.BlockSpec((1, rows, d), lambda j, b: (b, j, 0)),
            pl.BlockSpec((1, HALO, d), lambda j, b: (b, jnp.maximum(j * hb - 1, 0), 0)),
            pl.BlockSpec((1, HALO, d), lambda j, b: (b, jnp.minimum((j + 1) * hb, n_rows // HALO - 1), 0))]


def _inproj0(x, ctx, mod, g_pre, w_in, conv_w, conv_b, rope):
    bsz, s, d = x.shape
    w5 = conv_w.shape[-1]
    const2 = lambda j, b: (0, 0)
    consts = [pl.BlockSpec((1, d), const2), pl.BlockSpec(w_in.shape, const2), pl.BlockSpec(conv_w.shape, const2),
              pl.BlockSpec((1, w5), const2)]

    def call(is_ctx, src, rows, mod_row, name):
        n_rows = src.shape[1]
        n_steps = n_rows // rows
        rope_spec = pl.BlockSpec((rows, LANES), lambda j, b: (j, 0))
        return pl.pallas_call(
            functools.partial(_inproj0_kernel, is_ctx, n_steps),
            grid=(n_steps, bsz),
            in_specs=_halo_specs(rows, d, n_rows)
            + [pl.BlockSpec((1, 1, 3 * d), lambda j, b: (mod_row(b), 0, 0))] + consts + [rope_spec] * 3,
            out_specs=[pl.BlockSpec((1, rows, w5), lambda j, b: (b, j, 0))] * 6,
            out_shape=[jax.ShapeDtypeStruct((bsz, n_rows, w5), F32)]
            + [jax.ShapeDtypeStruct((bsz, n_rows, w5), BF16)] * 5,
            scratch_shapes=[pltpu.VMEM((w5 // LANES, rows + 2 * HALO, LANES), F32)],
            compiler_params=_cparams(("arbitrary", "arbitrary")),
            name=name,
        )(src, src, src, mod, g_pre.reshape(1, d), w_in, conv_w, conv_b.reshape(1, w5), *rope)

    return (call(False, x, INPROJ_TM, lambda b: b, "inproj0"),
            call(True, ctx, ctx.shape[1], lambda b: bsz, "inproj0_ctx"))


def _sqrt_pos(y):
    return jnp.where(y > 0.0, y * lax.rsqrt(y), 0.0)


def _lru_kernel(bsz, n_ctx, xfl_ref, xfc_ref, xbl_ref, xbc_ref, wg_ref, bias_ref, k4_ref, of_ref, ob_ref,
                x_scr, g_scr, a_scr, b_scr, hs_scr, h_scr):
    i = pl.program_id(0)
    w5 = xfl_ref.shape[-1]
    nslab = w5 // LANES
    tt = LRU_TT

    @pl.when(i == 0)
    def _():
        h_scr[...] = jnp.zeros_like(h_scr)

    for dr, (lat_ref, ctx_ref) in enumerate(((xfl_ref, xfc_ref), (xbl_ref, xbc_ref))):
        k4 = k4_ref[dr:dr + 1, :]
        x_ref = x_scr.at[dr]
        x_ref[...] = jnp.where(i < n_ctx, ctx_ref[...], lat_ref[...])
        g_scr[...] = jnp.dot(x_ref[...].reshape(bsz * tt, w5).astype(BF16), wg_ref[dr], preferred_element_type=F32)
        for b in range(bsz):
            xc = x_ref[b]
            tr = jnp.tanh(g_scr[b * tt:(b + 1) * tt, 0:w5] + bias_ref[2 * dr:2 * dr + 1, :])
            ti = jnp.tanh(g_scr[b * tt:(b + 1) * tt, w5:2 * w5] + bias_ref[2 * dr + 1:2 * dr + 2, :])
            nla = k4 * tr + k4
            a = jnp.exp2(nla * (-LOG2E))
            bc = _sqrt_pos(jnp.tanh(nla) * (a * a + 1.0)) * ((0.5 * xc) * (1.0 + ti))
            for c in range(nslab):
                a_scr[dr, c, pl.ds(b, tt, stride=LRU_PITCH), :] = a[:, c * LANES:(c + 1) * LANES]
                b_scr[dr, c, pl.ds(b, tt, stride=LRU_PITCH), :] = bc[:, c * LANES:(c + 1) * LANES]

    def step(t, carry):
        hf, hb = carry
        rf = t * LRU_PITCH
        rb = (tt - 1 - t) * LRU_PITCH
        nf, nb = [], []
        for c in range(nslab):
            h = a_scr[0, c, pl.ds(rf, bsz), :] * hf[c] + b_scr[0, c, pl.ds(rf, bsz), :]
            hs_scr[0, c, pl.ds(rf, bsz), :] = h
            nf.append(h)
            h = a_scr[1, c, pl.ds(rb, bsz), :] * hb[c] + b_scr[1, c, pl.ds(rb, bsz), :]
            hs_scr[1, c, pl.ds(rb, bsz), :] = h
            nb.append(h)
        return tuple(nf), tuple(nb)

    h0f = tuple(h_scr[0, c] for c in range(nslab))
    h0b = tuple(h_scr[1, c] for c in range(nslab))
    hf, hb = lax.fori_loop(0, tt, step, (h0f, h0b), unroll=8)
    for c in range(nslab):
        h_scr[0, c] = hf[c]
        h_scr[1, c] = hb[c]

    for dr, o_ref in enumerate((of_ref, ob_ref)):
        for b in range(bsz):
            for c in range(nslab):
                o_ref[b, :, c * LANES:(c + 1) * LANES] = hs_scr[dr, c, pl.ds(b, tt, stride=LRU_PITCH), :].astype(BF16)


def _bwd_tile(i, n_ctx, n_all):
    return jnp.where(i < n_ctx, n_ctx - 1 - i, n_all - 1 - (i - n_ctx))


def _walk_specs(block, n_ctx, n_lat, lead):
    def spec(tile):
        return pl.BlockSpec(block, lambda *ids: lead(*ids) + (tile(ids[-1]), 0))
    lat_pos = lambda i: jnp.clip(i - n_ctx, 0, n_lat - 1)
    ctx_pos = lambda i: jnp.minimum(i, n_ctx - 1)
    return [spec(lat_pos), spec(ctx_pos), spec(lambda i: n_lat - 1 - lat_pos(i)), spec(lambda i: n_ctx - 1 - ctx_pos(i))]


def _lru(xc_lat, xc_ctx, wg, bias, k4):
    bsz, s, w5 = xc_lat.shape
    tt = LRU_TT
    n_lat = s // tt
    n_ctx = xc_ctx.shape[1] // tt
    n_all = n_lat + n_ctx
    nslab = w5 // LANES
    fwd_spec = pl.BlockSpec((bsz, tt, w5), lambda i: (0, i, 0))
    bwd_spec = pl.BlockSpec((bsz, tt, w5), lambda i: (0, _bwd_tile(i, n_ctx, n_all), 0))
    const = lambda shape: pl.BlockSpec(shape, lambda i: (0,) * len(shape))
    assert bsz <= LRU_PITCH
    coef = pltpu.VMEM((2, nslab, tt * LRU_PITCH, LANES), F32)
    return pl.pallas_call(
        functools.partial(_lru_kernel, bsz, n_ctx),
        grid=(n_all,),
        in_specs=_walk_specs((bsz, tt, w5), n_ctx, n_lat, lambda i: (0,))
        + [const(wg.shape), const(bias.shape), const(k4.shape)],
        out_specs=[fwd_spec, bwd_spec],
        out_shape=[jax.ShapeDtypeStruct((bsz, n_all * tt, w5), BF16)] * 2,
        scratch_shapes=[pltpu.VMEM((2, bsz, tt, w5), F32), pltpu.VMEM((bsz * tt, 2 * w5), F32), coef, coef, coef,
                        pltpu.VMEM((2, nslab, bsz, LANES), F32)],
        compiler_params=_cparams(("arbitrary",)),
        name="rglru",
    )(xc_lat, xc_ctx, xc_lat, xc_ctx, wg, bias, k4)


def _attn_kernel(lambda_init, n_streams, n_lat_chunks, q_ref, kc_ref, vc_ref, kl_ref, vl_ref, lam_ref, sub_ref,
                 o_ref, m_scr, acc_scr):
    key_chunks = [(kc_ref, vc_ref, 0, kc_ref.shape[1])]
    key_chunks += [(kl_ref, vl_ref, c * KV_CHUNK, KV_CHUNK) for c in range(n_lat_chunks)]
    qqs = []
    for st in range(n_streams):
        q = q_ref[0, st * TM:(st + 1) * TM, :]
        lane = lax.broadcasted_iota(jnp.int32, q.shape, 1)
        zero = jnp.zeros_like(q)
        qqs.append(jnp.concatenate([jnp.where(lane < DA_HEAD_DIM, q, zero),
                                    jnp.where(lane >= DA_HEAD_DIM, q, zero)], axis=0))
    m_scr[...] = jnp.full_like(m_scr, -jnp.inf)
    acc_scr[...] = jnp.zeros_like(acc_scr)

    for k_ref, v_ref, start, size in key_chunks:
        kc = k_ref[0, start:start + size, :]
        va = jnp.concatenate([v_ref[0, start:start + size, :], jnp.ones((size, LANES), BF16)], axis=1)
        for st in range(n_streams):
            s = lax.dot_general(qqs[st], kc, (((1,), (1,)), ((), ())), preferred_element_type=F32)
            m_prev = m_scr[st]
            m_next = jnp.maximum(m_prev, jnp.max(s, axis=1, keepdims=True))
            p = jnp.exp2(s - jnp.concatenate([m_next] * (size // LANES), axis=1))
            alpha = jnp.exp2(m_prev - m_next)
            acc_scr[st] = (acc_scr[st] * jnp.concatenate([alpha, alpha], axis=1)
                           + jnp.dot(p.astype(BF16), va, preferred_element_type=F32))
            m_scr[st] = m_next

    lm = lam_ref[...]
    lam = (jnp.exp(jnp.sum(lm[0:1] * lm[1:2], axis=1, keepdims=True))
           - jnp.exp(jnp.sum(lm[2:3] * lm[3:4], axis=1, keepdims=True)) + lambda_init)
    for st in range(n_streams):
        o = (acc_scr[st, 0:TM, 0:LANES] / acc_scr[st, 0:TM, LANES:2 * LANES]
             - lam * (acc_scr[st, TM:2 * TM, 0:LANES] / acc_scr[st, TM:2 * TM, LANES:2 * LANES]))
        o_ref[0, st * TM:(st + 1) * TM, :] = ((_rms(o) * sub_ref[...]) * (1.0 - lambda_init)).astype(BF16)


def _attention(qkv_lat, qkv_ctx, da_lambda, da_subln, lambda_init):
    q_lat, k_lat, v_lat = qkv_lat
    q_ctx, k_ctx, v_ctx = qkv_ctx
    bsz, n_lat, w = q_lat.shape
    ctx_len = q_ctx.shape[1]
    hd = DA_V_DIM
    small = [pl.BlockSpec(da_lambda.shape, lambda b, h, j: (0, 0)), pl.BlockSpec((1, hd), lambda b, h, j: (0, 0))]
    sub = da_subln.reshape(1, hd)
    ctx_spec = pl.BlockSpec((1, ctx_len, hd), lambda b, h, j: (b, 0, h))

    def call(q, q_rows, kl, vl, n_lat_chunks, name):
        n_streams = q_rows // TM
        q_spec = pl.BlockSpec((1, q_rows, hd), lambda b, h, j: (b, j, h))
        lat_spec = pl.BlockSpec((1, kl.shape[1], hd), lambda b, h, j: (b, 0, h))
        return pl.pallas_call(
            functools.partial(_attn_kernel, lambda_init, n_streams, n_lat_chunks),
            grid=(bsz, DA_HEADS, q.shape[1] // q_rows),
            in_specs=[q_spec, ctx_spec, ctx_spec, lat_spec, lat_spec] + small,
            out_specs=q_spec,
            out_shape=jax.ShapeDtypeStruct(q.shape, BF16),
            scratch_shapes=[pltpu.VMEM((n_streams, 2 * TM, LANES), F32),
                            pltpu.VMEM((n_streams, 2 * TM, 2 * LANES), F32)],
            compiler_params=_cparams(("arbitrary", "arbitrary", "arbitrary")),
            name=name,
        )(q, k_ctx, v_ctx, kl, vl, da_lambda, sub)

    on_lat = call(q_lat, ATTN_TQ, k_lat, v_lat, n_lat // KV_CHUNK, "diffattn")
    on_ctx = call(q_ctx, ctx_len, k_ctx, v_ctx, 0, "diffattn_ctx")
    return on_lat, on_ctx


def _outproj0_kernel(hf_ref, hb_ref, gr_ref, on_ref, gd_ref, res_ref, mod_ref, g_ref, w_ref, o_ref):
    d = o_ref.shape[-1]
    w5 = hf_ref.shape[-1]
    kc = OUTPROJ_K
    gt = mod_ref[0, :, 2 * d:3 * d]
    for r0 in range(0, o_ref.shape[1], TM):
        rs = slice(r0, r0 + TM)
        y = None
        for c in range(w5 // kc):
            sl = slice(c * kc, (c + 1) * kc)
            r = hf_ref[0, rs, sl].astype(F32) + hb_ref[0, rs, sl].astype(F32)
            m1 = (r * _silu(gr_ref[0, rs, sl].astype(F32))).astype(BF16)
            m2 = (on_ref[0, rs, sl].astype(F32) * _silu(gd_ref[0, rs, sl].astype(F32))).astype(BF16)
            t = jnp.dot(m1, w_ref[c * kc:(c + 1) * kc, :], preferred_element_type=F32)
            t = t + jnp.dot(m2, w_ref[w5 + c * kc:w5 + (c + 1) * kc, :], preferred_element_type=F32)
            y = t if y is None else y + t
        o_ref[0, rs, :] = res_ref[0, rs, :] + gt * (_rms(y) * g_ref[...])


def _outproj0(hf, hb, lat, ctxs, x, ctx, mod, g_post, w_out):
    bsz, _, w5 = hf.shape
    d = x.shape[-1]
    ctx_len = ctx.shape[1]
    const2 = lambda j, b: (0, 0)
    consts = [pl.BlockSpec((1, d), const2), pl.BlockSpec(w_out.shape, const2)]

    def call(rows, uni, parts, res, mod_row, name):
        own = lambda width: pl.BlockSpec((1, rows, width), lambda j, b: (b, j, 0))
        gr, on, gd = parts
        return pl.pallas_call(
            _outproj0_kernel,
            grid=(res.shape[1] // rows, bsz),
            in_specs=[uni, uni, own(w5), own(w5), own(w5), own(d),
                      pl.BlockSpec((1, 1, 3 * d), lambda j, b: (mod_row(b), 0, 0))] + consts,
            out_specs=own(d),
            out_shape=jax.ShapeDtypeStruct(res.shape, F32),
            compiler_params=_cparams(("arbitrary", "arbitrary")),
            name=name,
        )(hf, hb, gr, on, gd, res, mod, g_post.reshape(1, d), w_out)

    h_lat = call(OUTPROJ_TM, _uni_lat_spec(OUTPROJ_TM, w5, ctx_len), lat, x, lambda b: b, "outproj0")
    ctx_uni = pl.BlockSpec((1, ctx_len, w5), lambda j, b: (b, 0, 0))
    h_ctx = call(ctx_len, ctx_uni, ctxs, ctx, lambda b: bsz, "outproj0_ctx")
    return h_lat, h_ctx


def _inproj1_kernel(is_ctx, n_steps, hm_ref, hl_ref, hr_ref, mod_ref, g_ref, wz_ref, wx_ref, wd_ref, cw_ref, cb_ref,
                    db_ref, z_ref, xbc_ref, dt_ref, ext_ref):
    d = hm_ref.shape[-1]
    rows = hm_ref.shape[1]
    h_ext = jnp.concatenate([hl_ref[0], hm_ref[0], hr_ref[0]], axis=0)
    u_ext = _normed_rows(h_ext, mod_ref, g_ref, d)
    u = u_ext[HALO:HALO + rows]
    left_ok, right_ok = _segment_edges(is_ctx, n_steps)
    nz, nx = wz_ref.shape[1], wx_ref.shape[1]
    cw = INPROJ_COLS
    spc = cw // LANES
    n_x, n_z = nx // cw, nz // cw
    for i in range(max(n_x, n_z)):
        if i < n_x:
            x_ext = jnp.dot(u_ext, wx_ref[:, i * cw:(i + 1) * cw], preferred_element_type=F32)
            _store_ext(ext_ref, i * spc, x_ext, left_ok, right_ok)
            for cs in range(i * spc, (i + 1) * spc):
                xbc_ref[0, :, cs * LANES:(cs + 1) * LANES] = _silu(_conv4(ext_ref, cs, cw_ref, cb_ref)).astype(BF16)
        if i < n_z:
            z_ref[0, :, i * cw:(i + 1) * cw] = _silu(jnp.dot(u, wz_ref[:, i * cw:(i + 1) * cw],
                                                             preferred_element_type=F32)).astype(BF16)
    dt_ref[0] = jax.nn.softplus(jnp.dot(u, wd_ref[...], preferred_element_type=F32) + db_ref[...])


def _inproj1(h_lat, h_ctx, mod, g_pre, wz, wx, wd, conv_w, conv_b, dt_bias):
    bsz, _, d = h_lat.shape
    nz, nx, nd = wz.shape[1], wx.shape[1], wd.shape[1]
    const2 = lambda j, b: (0, 0)
    consts = [pl.BlockSpec((1, d), const2), pl.BlockSpec(wz.shape, const2), pl.BlockSpec(wx.shape, const2),
              pl.BlockSpec(wd.shape, const2), pl.BlockSpec(conv_w.shape, const2), pl.BlockSpec((1, nx), const2),
              pl.BlockSpec((1, nd), const2)]

    def call(is_ctx, src, rows, mod_row, name):
        n_rows = src.shape[1]
        n_steps = n_rows // rows
        return pl.pallas_call(
            functools.partial(_inproj1_kernel, is_ctx, n_steps),
            grid=(n_steps, bsz),
            in_specs=_halo_specs(rows, d, n_rows)
            + [pl.BlockSpec((1, 1, 3 * d), lambda j, b: (mod_row(b), 0, 0))] + consts,
            out_specs=[pl.BlockSpec((1, rows, w), lambda j, b: (b, j, 0)) for w in (nz, nx, nd)],
            out_shape=[jax.ShapeDtypeStruct((bsz, n_rows, nz), BF16), jax.ShapeDtypeStruct((bsz, n_rows, nx), BF16),
                       jax.ShapeDtypeStruct((bsz, n_rows, nd), F32)],
            scratch_shapes=[pltpu.VMEM((nx // LANES, rows + 2 * HALO, LANES), F32)],
            compiler_params=_cparams(("arbitrary", "arbitrary")),
            name=name,
        )(src, src, src, mod, g_pre.reshape(1, d), wz, wx, wd, conv_w, conv_b.reshape(1, nx), dt_bias)

    return (call(False, h_lat, INPROJ_TM, lambda b: b, "inproj1"),
            call(True, h_ctx, h_ctx.shape[1], lambda b: bsz, "inproj1_ctx"))


def _ssd_prologue(reverse, dr, dt_ref, alog_ref):
    ch = SSD_CHUNK
    dt = dt_ref[0]
    adt = dt * (-jnp.exp(alog_ref[dr:dr + 1, :]) * LOG2E)
    row = lax.broadcasted_iota(jnp.int32, (ch, ch), 0)
    col = lax.broadcasted_iota(jnp.int32, (ch, ch), 1)
    mask = (row <= col) if reverse else (row >= col)
    cs = jnp.dot(mask.astype(F32), adt, preferred_element_type=F32, precision=HIGHEST)
    last = 0 if reverse else ch - 1
    cs_t = cs.T
    dt_t = dt.T
    w_t = jnp.exp2(cs_t[:, last:last + 1] - cs_t) * dt_t
    crow = cs_t - jnp.log2(dt_t)
    e_tot = jnp.exp2(cs[last:last + 1, :])
    return mask, cs, w_t, crow, e_tot


def _ssd_group(dr, g, xbc_ref, s_scr, inner):
    gn = SSD_GROUPS * SSD_STATE
    bg = xbc_ref[0, :, inner + g * SSD_STATE:inner + (g + 1) * SSD_STATE]
    cg = xbc_ref[0, :, inner + gn + g * SSD_STATE:inner + gn + (g + 1) * SSD_STATE]
    cb = lax.dot_general(cg, bg, (((1,), (1,)), ((), ())), preferred_element_type=F32).astype(BF16)
    bg_t = bg.astype(F32).T.astype(BF16)
    s_g = s_scr[dr, g]
    y_off = jnp.dot(cg, s_g.astype(BF16), preferred_element_type=F32)
    return cb, bg_t, s_g, y_off


def _ssd_pair(dr, g, pp, n_heads, pro, grp, xbc_ref, s_scr, y_ref):
    ch = SSD_CHUNK
    mask, cs, w_t, crow, e_tot = pro
    cb, bg_t, s_g, y_off = grp
    pairs_per_group = n_heads // SSD_GROUPS // 2
    p = g * pairs_per_group + pp
    h1 = dr * n_heads + 2 * p
    ls = slice(pp * LANES, (pp + 1) * LANES)
    left = lax.broadcasted_iota(jnp.int32, (ch, LANES), 1) < SSD_HEAD_DIM
    x2 = xbc_ref[0, :, p * LANES:(p + 1) * LANES]
    zero = jnp.zeros_like(x2)
    wx = jnp.concatenate([jnp.where(left, x2, zero), jnp.where(left, zero, x2)], axis=0)
    ms, bws, cols = [], [], []
    for h in (h1, h1 + 1):
        ccol = jnp.broadcast_to(cs[:, h:h + 1], (ch, ch))
        cols.append(ccol)
        ms.append(cb * jnp.exp2(jnp.where(mask, ccol - crow[h:h + 1, :], -jnp.inf)).astype(BF16))
        bws.append(bg_t * w_t[h:h + 1, :].astype(BF16))
    lhs = jnp.concatenate([jnp.concatenate(ms, axis=1), jnp.concatenate(bws, axis=1)], axis=0)
    r = jnp.dot(lhs, wx, preferred_element_type=F32)
    y = r[0:ch] + jnp.exp2(jnp.where(left, cols[0], cols[1])) * y_off[:, ls]
    y_ref[0, :, p * LANES:(p + 1) * LANES] = y.astype(BF16)
    dec = jnp.where(left[0:1, :], e_tot[:, h1:h1 + 1], e_tot[:, h1 + 1:h1 + 2])
    s_scr[dr, g, :, ls] = s_g[:, ls] * dec + r[ch:2 * ch]


def _ssd_kernel(n_ctx, xfl_ref, xfc_ref, xbl_ref, xbc_ref, dfl_ref, dfc_ref, dbl_ref, dbc_ref, alog_ref,
                yf_ref, yb_ref, x_scr, d_scr, s_scr):
    i = pl.program_id(1)

    @pl.when(i == 0)
    def _():
        s_scr[...] = jnp.zeros_like(s_scr)

    for dr, (xl, xc, dl, dc) in enumerate(((xfl_ref, xfc_ref, dfl_ref, dfc_ref), (xbl_ref, xbc_ref, dbl_ref, dbc_ref))):
        x_scr[dr] = jnp.where(i < n_ctx, xc[...], xl[...])
        d_scr[dr] = jnp.where(i < n_ctx, dc[...], dl[...])

    inner = yf_ref.shape[-1]
    n_heads = inner // SSD_HEAD_DIM
    dirs = ((x_scr.at[0], d_scr.at[0], yf_ref, False), (x_scr.at[1], d_scr.at[1], yb_ref, True))
    pros = [_ssd_prologue(rev, dr, d_ref, alog_ref) for dr, (_, d_ref, _, rev) in enumerate(dirs)]
    for g in range(SSD_GROUPS):
        grps = [_ssd_group(dr, g, x_ref, s_scr, inner) for dr, (x_ref, _, _, _) in enumerate(dirs)]
        for pp in range(n_heads // SSD_GROUPS // 2):
            for dr, (x_ref, _, y_ref, _) in enumerate(dirs):
                _ssd_pair(dr, g, pp, n_heads, pros[dr], grps[dr], x_ref, s_scr, y_ref)


def _ssd(xbc_lat, dt_lat, xbc_ctx, dt_ctx, a_log):
    bsz, s, nx = xbc_lat.shape
    nd = dt_lat.shape[-1]
    n_heads = a_log.shape[-1]
    inner = n_heads * SSD_HEAD_DIM
    a_log = jnp.stack([jnp.pad(a_log[dr], (dr * n_heads, nd - (dr + 1) * n_heads)) for dr in range(2)])
    ch = SSD_CHUNK
    n_lat = s // ch
    n_ctx = xbc_ctx.shape[1] // ch
    n_all = n_lat + n_ctx
    fwd = lambda b, i: (b, i, 0)
    bwd = lambda b, i: (b, _bwd_tile(i, n_ctx, n_all), 0)
    lead = lambda b, i: (b,)
    return pl.pallas_call(
        functools.partial(_ssd_kernel, n_ctx),
        grid=(bsz, n_all),
        in_specs=_walk_specs((1, ch, nx), n_ctx, n_lat, lead) + _walk_specs((1, ch, nd), n_ctx, n_lat, lead)
        + [pl.BlockSpec(a_log.shape, lambda b, i: (0, 0))],
        out_specs=[pl.BlockSpec((1, ch, inner), fwd), pl.BlockSpec((1, ch, inner), bwd)],
        out_shape=[jax.ShapeDtypeStruct((bsz, n_all * ch, inner), BF16)] * 2,
        scratch_shapes=[pltpu.VMEM((2, 1, ch, nx), BF16), pltpu.VMEM((2, 1, ch, nd), F32),
                        pltpu.VMEM((2, SSD_GROUPS, SSD_STATE, inner // SSD_GROUPS), F32)],
        compiler_params=_cparams(("arbitrary", "arbitrary")),
        name="ssd",
    )(xbc_lat, xbc_ctx, xbc_lat, xbc_ctx, dt_lat, dt_ctx, dt_lat, dt_ctx, a_log)


def _finish1_kernel(yf_ref, yb_ref, xs_ref, gz_ref, h_ref, mod_ref, dsk_ref, nw_ref, w_ref, g_ref, o_ref):
    d = o_ref.shape[-1]
    inner = yf_ref.shape[-1]
    gw = inner // SSD_GROUPS
    gt = mod_ref[0, :, 2 * d:3 * d]
    for r0 in range(0, o_ref.shape[1], TM):
        rs = slice(r0, r0 + TM)
        out = None
        for g in range(SSD_GROUPS):
            sl = slice(g * gw, (g + 1) * gw)
            y = (yf_ref[0, rs, sl] + yb_ref[0, rs, sl]).astype(F32) + dsk_ref[:, sl] * xs_ref[0, rs, sl].astype(F32)
            y = y * gz_ref[0, rs, sl].astype(F32)
            t = jnp.dot((_rms(y) * nw_ref[:, sl]).astype(BF16), w_ref[sl, :], preferred_element_type=F32)
            out = t if out is None else out + t
        o_ref[0, rs, :] = h_ref[0, rs, :] + gt * (_rms(out) * g_ref[...])


def _finish1(yf, yb, xbc, gz, h, mod, d_skip, norm_w, w_out, g_post):
    bsz, t_all, inner = yf.shape
    d = h.shape[-1]
    rows = FINISH_TM
    n_lat = h.shape[1] // rows
    ctx_len = t_all - h.shape[1]
    own = lambda width: pl.BlockSpec((1, rows, width), lambda j, b: (b, j, 0))
    const2 = lambda j, b: (0, 0)
    return pl.pallas_call(
        _finish1_kernel,
        grid=(n_lat, bsz),
        in_specs=[_uni_lat_spec(rows, inner, ctx_len)] * 2 + [own(inner), own(inner)] + [
            pl.BlockSpec((1, rows, d), lambda j, b: (b, j, 0)),
            pl.BlockSpec((1, 1, 3 * d), lambda j, b: (b, 0, 0)),
            pl.BlockSpec((1, inner), const2),
            pl.BlockSpec((1, inner), const2),
            pl.BlockSpec(w_out.shape, const2),
            pl.BlockSpec((1, d), const2),
        ],
        out_specs=pl.BlockSpec((1, rows, d), lambda j, b: (b, j, 0)),
        out_shape=jax.ShapeDtypeStruct((bsz, n_lat * rows, d), F32),
        compiler_params=_cparams(("arbitrary", "arbitrary")),
        name="finish1",
    )(yf, yb, xbc, gz, h, mod, d_skip, norm_w, w_out, g_post.reshape(1, d))


def _rope_tables(n_tokens):
    rows = n_tokens // GRID_W
    row = jnp.repeat(jnp.arange(rows, dtype=F32), GRID_W)
    col = jnp.tile(jnp.arange(GRID_W, dtype=F32), rows)
    n_freq = DA_HEAD_DIM // 4
    inv = ROPE_BASE ** (-jnp.arange(n_freq, dtype=F32) / n_freq)
    ang = jnp.concatenate([row[:, None] * inv, col[:, None] * inv], axis=-1)
    cos, sin = jnp.cos(ang), jnp.sin(ang)
    cr, cc, sr, sc = cos[:, :n_freq], cos[:, n_freq:], sin[:, :n_freq], sin[:, n_freq:]
    zr = jnp.zeros_like(sr)
    c64 = jnp.concatenate([cr, cr, cc, cc], axis=1)
    s1 = jnp.concatenate([-sr, zr, -sc, zr], axis=1)
    s2 = jnp.concatenate([zr, sr, zr, sc], axis=1)
    reps = LANES // DA_HEAD_DIM
    return tuple(jnp.tile(t, (1, reps)) for t in (c64, s1, s2))


def _block_diag(w):
    n, c, d = w.shape
    eye = jnp.eye(n, dtype=w.dtype)
    return (eye[:, None, :, None] * w[:, :, None, :]).reshape(n * c, n * d)


def kernel(x, c, ctx, c_ctx, w_mod, b_mod, g_pre, g_post, e_w_in, e_w_out, lru_conv_w, lru_conv_b, lru_w_r, lru_b_r, lru_w_i, lru_b_i, lru_lambda, da_lambda, da_subln, o_w_in, o_w_out, ssd_conv_w, ssd_conv_b, ssd_a_log, ssd_dt_bias, ssd_d, ssd_norm):
    bsz, s, d = x.shape
    ctx_len = ctx.shape[1]
    assert bsz == SUBLANES and ctx_len == TM and s % KV_CHUNK == 0 and w_mod.shape[0] == 2
    assert e_w_in.shape[0] == 1 and o_w_in.shape[0] == 1

    n_rows = 2 * SUBLANES
    c_rows = jnp.concatenate([c, c_ctx[None, :], jnp.zeros((n_rows - bsz - 1, d), F32)], axis=0)
    mod = _modulation(c_rows, w_mod, b_mod)
    mod0 = mod[0].reshape(n_rows, 1, 3 * d)
    mod1 = mod[1].reshape(n_rows, 1, 3 * d)

    w5 = lru_conv_w.shape[-1]
    w_in0 = e_w_in[0].astype(BF16)
    (xc, gr, q, k, v, gd), (xc_c, gr_c, q_c, k_c, v_c, gd_c) = _inproj0(
        x, ctx, mod0, g_pre[0], w_in0, lru_conv_w[0], lru_conv_b[0], _rope_tables(s))

    wg = (0.5 * jnp.stack([jnp.concatenate([_block_diag(lru_w_r[0, dr]), _block_diag(lru_w_i[0, dr])], axis=1)
                           for dr in range(2)])).astype(BF16)
    bias = 0.5 * jnp.stack([lru_b_r[0, 0], lru_b_i[0, 0], lru_b_r[0, 1], lru_b_i[0, 1]])
    k4 = (0.5 * LRU_C) * jax.nn.softplus(-lru_lambda[0])
    hf, hb = _lru(xc, xc_c, wg, bias, k4)

    lambda_init = 0.8 - 0.6 * math.exp(-0.3 * 0)
    on, on_c = _attention((q, k, v), (q_c, k_c, v_c), da_lambda[0], da_subln[0], lambda_init)
    h1_lat, h1_ctx = _outproj0(hf, hb, (gr, on, gd), (gr_c, on_c, gd_c), x, ctx, mod0, g_post[0],
                               e_w_out[0].astype(BF16))

    n_heads = ssd_a_log.shape[-1]
    inner = n_heads * SSD_HEAD_DIM
    nx = ssd_conv_w.shape[-1]
    w1 = o_w_in[0]
    wz = w1[:, :inner].astype(BF16)
    wx = w1[:, inner:inner + nx].astype(BF16)
    nd = 2 * n_heads
    wd = jnp.pad(w1[:, inner + nx:], ((0, 0), (0, LANES - nd))).astype(BF16)
    dt_bias = jnp.pad(ssd_dt_bias[0].reshape(1, nd), ((0, 0), (0, LANES - nd)))
    (gz, xbc, dt), (_, xbc_c, dt_c) = _inproj1(h1_lat, h1_ctx, mod1, g_pre[1], wz, wx, wd, ssd_conv_w[0],
                                               ssd_conv_b[0], dt_bias)
    yf, yb = _ssd(xbc, dt, xbc_c, dt_c, ssd_a_log[0])
    d_skip = jnp.repeat(ssd_d[0], SSD_HEAD_DIM).reshape(1, inner)
    return _finish1(yf, yb, xbc, gz, h1_lat, mod1, d_skip, ssd_norm[0].reshape(1, inner),
                    o_w_out[0].astype(BF16), g_post[1])
```

```python
import functools
import math

import jax
import jax.numpy as jnp
from jax import lax
from jax.experimental import pallas as pl
from jax.experimental.pallas import tpu as pltpu

F32 = jnp.float32
BF16 = jnp.bfloat16

EPS = 1e-6
GRID_W = 64
ROPE_BASE = 10000.0
LRU_C = 8.0
LRU_BLOCKS = 8
DA_HEADS = 4
DA_HEAD_DIM = 64
DA_V_DIM = 128
SSD_HEAD_DIM = 64
SSD_STATE = 128
SSD_GROUPS = 4
SSD_CHUNK = 128

TM = 256
HALO = 8
LRU_TT = 128
LRU_PITCH = 9
KV_CHUNK = 512
ATTN_TQ = 1024
INPROJ_TM = 512
INPROJ_COLS = 512
OUTPROJ_K = 256
FINISH_TM = 512
OUTPROJ_TM = 512
LANES = 128
SUBLANES = 8
VMEM_LIMIT = 56 * 1024 * 1024

HIGHEST = lax.Precision.HIGHEST
LOG2E = math.log2(math.e)
Q_SCALE = DA_HEAD_DIM ** -0.5 * LOG2E


def _sigmoid(x):
    return 0.5 * (1.0 + jnp.tanh(0.5 * x))


def _silu(x):
    h = 0.5 * x
    return h * (1.0 + jnp.tanh(h))


def _rms(x):
    return x * lax.rsqrt(jnp.mean(x * x, axis=-1, keepdims=True) + EPS)


def _cparams(sem):
    return pltpu.CompilerParams(dimension_semantics=sem, vmem_limit_bytes=VMEM_LIMIT)


def _mod_kernel(c_ref, w_ref, b_ref, o_ref):
    s = _silu(c_ref[...])
    o_ref[0] = jnp.dot(s, w_ref[0], preferred_element_type=F32, precision=HIGHEST) + b_ref[0]


def _modulation(c_rows, w_mod, b_mod):
    depth, d, n3 = w_mod.shape
    rows = c_rows.shape[0]
    nt = 1024
    return pl.pallas_call(
        _mod_kernel,
        grid=(depth, n3 // nt),
        in_specs=[
            pl.BlockSpec((rows, d), lambda l, n: (0, 0)),
            pl.BlockSpec((1, d, nt), lambda l, n: (l, 0, n)),
            pl.BlockSpec((1, 1, nt), lambda l, n: (l, 0, n)),
        ],
        out_specs=pl.BlockSpec((1, rows, nt), lambda l, n: (l, 0, n)),
        out_shape=jax.ShapeDtypeStruct((depth, rows, n3), F32),
        compiler_params=_cparams(("arbitrary", "arbitrary")),
        name="modulation",
    )(c_rows, w_mod, b_mod.reshape(depth, 1, n3))


def _normed_rows(h_ext, mod_ref, g_ref, d):
    sh = mod_ref[0, :, 0:d]
    sc = mod_ref[0, :, d:2 * d]
    u = (_rms(h_ext) * g_ref[...]) * (1.0 + sc) + sh
    return u.astype(BF16)


def _store_ext(ext_ref, slab0, x_ext, left_ok, right_ok):
    n = x_ext.shape[1] // LANES
    rows = x_ext.shape[0] - 2 * HALO
    zero = jnp.zeros((HALO, LANES), F32)
    for c in range(n):
        blk = x_ext[:, c * LANES:(c + 1) * LANES]
        ext_ref[slab0 + c, 0:HALO, :] = jnp.where(left_ok, blk[0:HALO], zero)
        ext_ref[slab0 + c, HALO:HALO + rows, :] = blk[HALO:HALO + rows]
        ext_ref[slab0 + c, HALO + rows:, :] = jnp.where(right_ok, blk[HALO + rows:], zero)


def _conv4(ext_ref, c, w_ref, b_ref):
    rows = ext_ref.shape[1] - 2 * HALO
    ls = slice(c * LANES, (c + 1) * LANES)
    acc = b_ref[:, ls] + w_ref[0:1, ls] * ext_ref[c, pl.ds(HALO - 2, rows), :]
    acc = acc + w_ref[1:2, ls] * ext_ref[c, pl.ds(HALO - 1, rows), :]
    acc = acc + w_ref[2:3, ls] * ext_ref[c, pl.ds(HALO, rows), :]
    acc = acc + w_ref[3:4, ls] * ext_ref[c, pl.ds(HALO + 1, rows), :]
    return acc


def _segment_edges(is_ctx, n_steps):
    if is_ctx:
        return jnp.bool_(False), jnp.bool_(False)
    j = pl.program_id(0)
    return j >= 1, j <= n_steps - 2


def _uni_lat_spec(rows, width, ctx_len):
    return pl.BlockSpec((pl.Element(1), pl.Element(rows), pl.Element(width)),
                        lambda j, b: (b, pl.multiple_of(ctx_len + j * rows, TM), 0))


def _inproj0_kernel(is_ctx, n_steps, hm_ref, hl_ref, hr_ref, mod_ref, g_ref, w_ref, cw_ref, cb_ref,
                    rc_ref, rs1_ref, rs2_ref, xc_ref, gr_ref, q_ref, k_ref, v_ref, gd_ref, ext_ref):
    d = hm_ref.shape[-1]
    rows = hm_ref.shape[1]
    w5 = xc_ref.shape[-1]
    h_ext = jnp.concatenate([hl_ref[0], hm_ref[0], hr_ref[0]], axis=0)
    u_ext = _normed_rows(h_ext, mod_ref, g_ref, d)
    u = u_ext[HALO:HALO + rows]

    x_ext = jnp.dot(u_ext, w_ref[:, 0:w5], preferred_element_type=F32)
    _store_ext(ext_ref, 0, x_ext, *_segment_edges(is_ctx, n_steps))
    for cs in range(w5 // LANES):
        xc_ref[0, :, cs * LANES:(cs + 1) * LANES] = _conv4(ext_ref, cs, cw_ref, cb_ref)

    gr_ref[0] = jnp.dot(u, w_ref[:, w5:2 * w5], preferred_element_type=F32).astype(BF16)
    reps = w5 // LANES
    for idx, o_ref, post in ((2, q_ref, Q_SCALE), (3, k_ref, None)):
        t = jnp.dot(u, w_ref[:, idx * w5:(idx + 1) * w5], preferred_element_type=F32)
        if not is_ctx:
            c = jnp.concatenate([rc_ref[...]] * reps, axis=1)
            s1 = jnp.concatenate([rs1_ref[...]] * reps, axis=1)
            s2 = jnp.concatenate([rs2_ref[...]] * reps, axis=1)
            t = t * c + pltpu.roll(t, w5 - 16, 1) * s1 + pltpu.roll(t, 16, 1) * s2
        o_ref[0] = (t if post is None else t * post).astype(BF16)
    v_ref[0] = jnp.dot(u, w_ref[:, 4 * w5:5 * w5], preferred_element_type=F32).astype(BF16)
    gd_ref[0] = jnp.dot(u, w_ref[:, 5 * w5:6 * w5], preferred_element_type=F32).astype(BF16)


def _halo_specs(rows, d, n_rows):
    hb = rows // HALO
    return [pl.BlockSpec((1, rows, d), lambda j, b: (b, j, 0)),
            pl.BlockSpec((1, HALO, d), lambda j, b: (b, jnp.maximum(j * hb - 1, 0), 0)),
            pl.BlockSpec((1, HALO, d), lambda j, b: (b, jnp.minimum((j + 1) * hb, n_rows // HALO - 1), 0))]


def _inproj0(x, ctx, mod, g_pre, w_in, conv_w, conv_b, rope):
    bsz, s, d = x.shape
    w5 = conv_w.shape[-1]
    const2 = lambda j, b: (0, 0)
    consts = [pl.BlockSpec((1, d), const2), pl.BlockSpec(w_in.shape, const2), pl.BlockSpec(conv_w.shape, const2),
              pl.BlockSpec((1, w5), const2)]

    def call(is_ctx, src, rows, mod_row, name):
        n_rows = src.shape[1]
        n_steps = n_rows // rows
        rope_spec = pl.BlockSpec((rows, LANES), lambda j, b: (j, 0))
        return pl.pallas_call(
            functools.partial(_inproj0_kernel, is_ctx, n_steps),
            grid=(n_steps, bsz),
            in_specs=_halo_specs(rows, d, n_rows)
            + [pl.BlockSpec((1, 1, 3 * d), lambda j, b: (mod_row(b), 0, 0))] + consts + [rope_spec] * 3,
            out_specs=[pl.BlockSpec((1, rows, w5), lambda j, b: (b, j, 0))] * 6,
            out_shape=[jax.ShapeDtypeStruct((bsz, n_rows, w5), F32)]
            + [jax.ShapeDtypeStruct((bsz, n_rows, w5), BF16)] * 5,
            scratch_shapes=[pltpu.VMEM((w5 // LANES, rows + 2 * HALO, LANES), F32)],
            compiler_params=_cparams(("arbitrary", "arbitrary")),
            name=name,
        )(src, src, src, mod, g_pre.reshape(1, d), w_in, conv_w, conv_b.reshape(1, w5), *rope)

    return (call(False, x, INPROJ_TM, lambda b: b, "inproj0"),
            call(True, ctx, ctx.shape[1], lambda b: bsz, "inproj0_ctx"))


def _sqrt_pos(y):
    return jnp.where(y > 0.0, y * lax.rsqrt(y), 0.0)


def _lru_kernel(bsz, n_ctx, xfl_ref, xfc_ref, xbl_ref, xbc_ref, wg_ref, bias_ref, k4_ref, of_ref, ob_ref,
                g_scr, a_scr, b_scr, hs_scr, h_scr):
    i = pl.program_id(0)
    w5 = xfl_ref.shape[-1]
    nslab = w5 // LANES
    tt = LRU_TT

    @pl.when(i == 0)
    def _():
        h_scr[...] = jnp.zeros_like(h_scr)

    def coefficients(xf_ref, xb_ref):
        for dr, x_ref in enumerate((xf_ref, xb_ref)):
            k4 = k4_ref[dr:dr + 1, :]
            g_scr[...] = jnp.dot(x_ref[...].reshape(bsz * tt, w5).astype(BF16), wg_ref[dr],
                                 preferred_element_type=F32)
            for b in range(bsz):
                xc = x_ref[b]
                tr = jnp.tanh(g_scr[b * tt:(b + 1) * tt, 0:w5] + bias_ref[2 * dr:2 * dr + 1, :])
                ti = jnp.tanh(g_scr[b * tt:(b + 1) * tt, w5:2 * w5] + bias_ref[2 * dr + 1:2 * dr + 2, :])
                nla = k4 * tr + k4
                a = jnp.exp2(nla * (-LOG2E))
                bc = _sqrt_pos(jnp.tanh(nla) * (a * a + 1.0)) * ((0.5 * xc) * (1.0 + ti))
                for c in range(nslab):
                    a_scr[dr, c, pl.ds(b, tt, stride=LRU_PITCH), :] = a[:, c * LANES:(c + 1) * LANES]
                    b_scr[dr, c, pl.ds(b, tt, stride=LRU_PITCH), :] = bc[:, c * LANES:(c + 1) * LANES]

    pl.when(i < n_ctx)(lambda: coefficients(xfc_ref, xbc_ref))
    pl.when(i >= n_ctx)(lambda: coefficients(xfl_ref, xbl_ref))

    def step(t, carry):
        hf, hb = carry
        rf = t * LRU_PITCH
        rb = (tt - 1 - t) * LRU_PITCH
        nf, nb = [], []
        for c in range(nslab):
            h = a_scr[0, c, pl.ds(rf, bsz), :] * hf[c] + b_scr[0, c, pl.ds(rf, bsz), :]
            hs_scr[0, c, pl.ds(rf, bsz), :] = h
            nf.append(h)
            h = a_scr[1, c, pl.ds(rb, bsz), :] * hb[c] + b_scr[1, c, pl.ds(rb, bsz), :]
            hs_scr[1, c, pl.ds(rb, bsz), :] = h
            nb.append(h)
        return tuple(nf), tuple(nb)

    h0f = tuple(h_scr[0, c] for c in range(nslab))
    h0b = tuple(h_scr[1, c] for c in range(nslab))
    hf, hb = lax.fori_loop(0, tt, step, (h0f, h0b), unroll=8)
    for c in range(nslab):
        h_scr[0, c] = hf[c]
        h_scr[1, c] = hb[c]

    for dr, o_ref in enumerate((of_ref, ob_ref)):
        for b in range(bsz):
            for c in range(nslab):
                o_ref[b, :, c * LANES:(c + 1) * LANES] = hs_scr[dr, c, pl.ds(b, tt, stride=LRU_PITCH), :].astype(BF16)


def _bwd_tile(i, n_ctx, n_all):
    return jnp.where(i < n_ctx, n_ctx - 1 - i, n_all - 1 - (i - n_ctx))


def _walk_specs(block, n_ctx, n_lat, lead):
    def spec(tile):
        return pl.BlockSpec(block, lambda *ids: lead(*ids) + (tile(ids[-1]), 0))
    lat_pos = lambda i: jnp.clip(i - n_ctx, 0, n_lat - 1)
    ctx_pos = lambda i: jnp.minimum(i, n_ctx - 1)
    return [spec(lat_pos), spec(ctx_pos), spec(lambda i: n_lat - 1 - lat_pos(i)), spec(lambda i: n_ctx - 1 - ctx_pos(i))]


def _lru(xc_lat, xc_ctx, wg, bias, k4):
    bsz, s, w5 = xc_lat.shape
    tt = LRU_TT
    n_lat = s // tt
    n_ctx = xc_ctx.shape[1] // tt
    n_all = n_lat + n_ctx
    nslab = w5 // LANES
    fwd_spec = pl.BlockSpec((bsz, tt, w5), lambda i: (0, i, 0))
    bwd_spec = pl.BlockSpec((bsz, tt, w5), lambda i: (0, _bwd_tile(i, n_ctx, n_all), 0))
    const = lambda shape: pl.BlockSpec(shape, lambda i: (0,) * len(shape))
    assert bsz <= LRU_PITCH
    coef = pltpu.VMEM((2, nslab, tt * LRU_PITCH, LANES), F32)
    return pl.pallas_call(
        functools.partial(_lru_kernel, bsz, n_ctx),
        grid=(n_all,),
        in_specs=_walk_specs((bsz, tt, w5), n_ctx, n_lat, lambda i: (0,))
        + [const(wg.shape), const(bias.shape), const(k4.shape)],
        out_specs=[fwd_spec, bwd_spec],
        out_shape=[jax.ShapeDtypeStruct((bsz, n_all * tt, w5), BF16)] * 2,
        scratch_shapes=[pltpu.VMEM((bsz * tt, 2 * w5), F32), coef, coef, coef,
                        pltpu.VMEM((2, nslab, bsz, LANES), F32)],
        compiler_params=_cparams(("arbitrary",)),
        name="rglru",
    )(xc_lat, xc_ctx, xc_lat, xc_ctx, wg, bias, k4)


def _attn_kernel(lambda_init, n_streams, n_lat_chunks, q_ref, kc_ref, vc_ref, kl_ref, vl_ref, lam_ref, sub_ref,
                 o_ref, m_scr, acc_scr):
    key_chunks = [(kc_ref, vc_ref, 0, kc_ref.shape[1])]
    key_chunks += [(kl_ref, vl_ref, c * KV_CHUNK, KV_CHUNK) for c in range(n_lat_chunks)]
    qqs = []
    for st in range(n_streams):
        q = q_ref[0, st * TM:(st + 1) * TM, :]
        lane = lax.broadcasted_iota(jnp.int32, q.shape, 1)
        zero = jnp.zeros_like(q)
        qqs.append(jnp.concatenate([jnp.where(lane < DA_HEAD_DIM, q, zero),
                                    jnp.where(lane >= DA_HEAD_DIM, q, zero)], axis=0))
    m_scr[...] = jnp.full_like(m_scr, -jnp.inf)
    acc_scr[...] = jnp.zeros_like(acc_scr)

    for k_ref, v_ref, start, size in key_chunks:
        kc = k_ref[0, start:start + size, :]
        va = jnp.concatenate([v_ref[0, start:start + size, :], jnp.ones((size, LANES), BF16)], axis=1)
        for st in range(n_streams):
            s = lax.dot_general(qqs[st], kc, (((1,), (1,)), ((), ())), preferred_element_type=F32)
            m_prev = m_scr[st]
            m_next = jnp.maximum(m_prev, jnp.max(s, axis=1, keepdims=True))
            p = jnp.exp2(s - jnp.concatenate([m_next] * (size // LANES), axis=1))
            alpha = jnp.exp2(m_prev - m_next)
            acc_scr[st] = (acc_scr[st] * jnp.concatenate([alpha, alpha], axis=1)
                           + jnp.dot(p.astype(BF16), va, preferred_element_type=F32))
            m_scr[st] = m_next

    lm = lam_ref[...]
    lam = (jnp.exp(jnp.sum(lm[0:1] * lm[1:2], axis=1, keepdims=True))
           - jnp.exp(jnp.sum(lm[2:3] * lm[3:4], axis=1, keepdims=True)) + lambda_init)
    for st in range(n_streams):
        o = (acc_scr[st, 0:TM, 0:LANES] / acc_scr[st, 0:TM, LANES:2 * LANES]
             - lam * (acc_scr[st, TM:2 * TM, 0:LANES] / acc_scr[st, TM:2 * TM, LANES:2 * LANES]))
        o_ref[0, st * TM:(st + 1) * TM, :] = ((_rms(o) * sub_ref[...]) * (1.0 - lambda_init)).astype(BF16)


def _attention(qkv_lat, qkv_ctx, da_lambda, da_subln, lambda_init):
    q_lat, k_lat, v_lat = qkv_lat
    q_ctx, k_ctx, v_ctx = qkv_ctx
    bsz, n_lat, w = q_lat.shape
    ctx_len = q_ctx.shape[1]
    hd = DA_V_DIM
    small = [pl.BlockSpec(da_lambda.shape, lambda b, h, j: (0, 0)), pl.BlockSpec((1, hd), lambda b, h, j: (0, 0))]
    sub = da_subln.reshape(1, hd)
    ctx_spec = pl.BlockSpec((1, ctx_len, hd), lambda b, h, j: (b, 0, h))

    def call(q, q_rows, kl, vl, n_lat_chunks, name):
        n_streams = q_rows // TM
        q_spec = pl.BlockSpec((1, q_rows, hd), lambda b, h, j: (b, j, h))
        lat_spec = pl.BlockSpec((1, kl.shape[1], hd), lambda b, h, j: (b, 0, h))
        return pl.pallas_call(
            functools.partial(_attn_kernel, lambda_init, n_streams, n_lat_chunks),
            grid=(bsz, DA_HEADS, q.shape[1] // q_rows),
            in_specs=[q_spec, ctx_spec, ctx_spec, lat_spec, lat_spec] + small,
            out_specs=q_spec,
            out_shape=jax.ShapeDtypeStruct(q.shape, BF16),
            scratch_shapes=[pltpu.VMEM((n_streams, 2 * TM, LANES), F32),
                            pltpu.VMEM((n_streams, 2 * TM, 2 * LANES), F32)],
            compiler_params=_cparams(("arbitrary", "arbitrary", "arbitrary")),
            name=name,
        )(q, k_ctx, v_ctx, kl, vl, da_lambda, sub)

    on_lat = call(q_lat, ATTN_TQ, k_lat, v_lat, n_lat // KV_CHUNK, "diffattn")
    on_ctx = call(q_ctx, ctx_len, k_ctx, v_ctx, 0, "diffattn_ctx")
    return on_lat, on_ctx


def _outproj0_kernel(hf_ref, hb_ref, gr_ref, on_ref, gd_ref, res_ref, mod_ref, g_ref, w_ref, o_ref):
    d = o_ref.shape[-1]
    w5 = hf_ref.shape[-1]
    kc = OUTPROJ_K
    gt = mod_ref[0, :, 2 * d:3 * d]
    for r0 in range(0, o_ref.shape[1], TM):
        rs = slice(r0, r0 + TM)
        y = None
        for c in range(w5 // kc):
            sl = slice(c * kc, (c + 1) * kc)
            r = hf_ref[0, rs, sl].astype(F32) + hb_ref[0, rs, sl].astype(F32)
            m1 = (r * _silu(gr_ref[0, rs, sl].astype(F32))).astype(BF16)
            m2 = (on_ref[0, rs, sl].astype(F32) * _silu(gd_ref[0, rs, sl].astype(F32))).astype(BF16)
            t = jnp.dot(m1, w_ref[c * kc:(c + 1) * kc, :], preferred_element_type=F32)
            t = t + jnp.dot(m2, w_ref[w5 + c * kc:w5 + (c + 1) * kc, :], preferred_element_type=F32)
            y = t if y is None else y + t
        o_ref[0, rs, :] = res_ref[0, rs, :] + gt * (_rms(y) * g_ref[...])


def _outproj0(hf, hb, lat, ctxs, x, ctx, mod, g_post, w_out):
    bsz, _, w5 = hf.shape
    d = x.shape[-1]
    ctx_len = ctx.shape[1]
    const2 = lambda j, b: (0, 0)
    consts = [pl.BlockSpec((1, d), const2), pl.BlockSpec(w_out.shape, const2)]

    def call(rows, uni, parts, res, mod_row, name):
        own = lambda width: pl.BlockSpec((1, rows, width), lambda j, b: (b, j, 0))
        gr, on, gd = parts
        return pl.pallas_call(
            _outproj0_kernel,
            grid=(res.shape[1] // rows, bsz),
            in_specs=[uni, uni, own(w5), own(w5), own(w5), own(d),
                      pl.BlockSpec((1, 1, 3 * d), lambda j, b: (mod_row(b), 0, 0))] + consts,
            out_specs=own(d),
            out_shape=jax.ShapeDtypeStruct(res.shape, F32),
            compiler_params=_cparams(("arbitrary", "arbitrary")),
            name=name,
        )(hf, hb, gr, on, gd, res, mod, g_post.reshape(1, d), w_out)

    h_lat = call(OUTPROJ_TM, _uni_lat_spec(OUTPROJ_TM, w5, ctx_len), lat, x, lambda b: b, "outproj0")
    ctx_uni = pl.BlockSpec((1, ctx_len, w5), lambda j, b: (b, 0, 0))
    h_ctx = call(ctx_len, ctx_uni, ctxs, ctx, lambda b: bsz, "outproj0_ctx")
    return h_lat, h_ctx


def _inproj1_kernel(is_ctx, n_steps, hm_ref, hl_ref, hr_ref, mod_ref, g_ref, wz_ref, wx_ref, wd_ref, cw_ref, cb_ref,
                    db_ref, z_ref, xbc_ref, dt_ref, ext_ref):
    d = hm_ref.shape[-1]
    rows = hm_ref.shape[1]
    h_ext = jnp.concatenate([hl_ref[0], hm_ref[0], hr_ref[0]], axis=0)
    u_ext = _normed_rows(h_ext, mod_ref, g_ref, d)
    u = u_ext[HALO:HALO + rows]
    left_ok, right_ok = _segment_edges(is_ctx, n_steps)
    nz, nx = wz_ref.shape[1], wx_ref.shape[1]
    cw = INPROJ_COLS
    spc = cw // LANES
    n_x, n_z = nx // cw, nz // cw
    for i in range(max(n_x, n_z)):
        if i < n_x:
            x_ext = jnp.dot(u_ext, wx_ref[:, i * cw:(i + 1) * cw], preferred_element_type=F32)
            _store_ext(ext_ref, i * spc, x_ext, left_ok, right_ok)
            for cs in range(i * spc, (i + 1) * spc):
                xbc_ref[0, :, cs * LANES:(cs + 1) * LANES] = _silu(_conv4(ext_ref, cs, cw_ref, cb_ref)).astype(BF16)
        if i < n_z:
            z_ref[0, :, i * cw:(i + 1) * cw] = _silu(jnp.dot(u, wz_ref[:, i * cw:(i + 1) * cw],
                                                             preferred_element_type=F32)).astype(BF16)
    dt_ref[0] = jax.nn.softplus(jnp.dot(u, wd_ref[...], preferred_element_type=F32) + db_ref[...])


def _inproj1(h_lat, h_ctx, mod, g_pre, wz, wx, wd, conv_w, conv_b, dt_bias):
    bsz, _, d = h_lat.shape
    nz, nx, nd = wz.shape[1], wx.shape[1], wd.shape[1]
    const2 = lambda j, b: (0, 0)
    consts = [pl.BlockSpec((1, d), const2), pl.BlockSpec(wz.shape, const2), pl.BlockSpec(wx.shape, const2),
              pl.BlockSpec(wd.shape, const2), pl.BlockSpec(conv_w.shape, const2), pl.BlockSpec((1, nx), const2),
              pl.BlockSpec((1, nd), const2)]

    def call(is_ctx, src, rows, mod_row, name):
        n_rows = src.shape[1]
        n_steps = n_rows // rows
        return pl.pallas_call(
            functools.partial(_inproj1_kernel, is_ctx, n_steps),
            grid=(n_steps, bsz),
            in_specs=_halo_specs(rows, d, n_rows)
            + [pl.BlockSpec((1, 1, 3 * d), lambda j, b: (mod_row(b), 0, 0))] + consts,
            out_specs=[pl.BlockSpec((1, rows, w), lambda j, b: (b, j, 0)) for w in (nz, nx, nd)],
            out_shape=[jax.ShapeDtypeStruct((bsz, n_rows, nz), BF16), jax.ShapeDtypeStruct((bsz, n_rows, nx), BF16),
                       jax.ShapeDtypeStruct((bsz, n_rows, nd), F32)],
            scratch_shapes=[pltpu.VMEM((nx // LANES, rows + 2 * HALO, LANES), F32)],
            compiler_params=_cparams(("arbitrary", "arbitrary")),
            name=name,
        )(src, src, src, mod, g_pre.reshape(1, d), wz, wx, wd, conv_w, conv_b.reshape(1, nx), dt_bias)

    return (call(False, h_lat, INPROJ_TM, lambda b: b, "inproj1"),
            call(True, h_ctx, h_ctx.shape[1], lambda b: bsz, "inproj1_ctx"))


def _ssd_prologue(reverse, dr, dt_ref, alog_ref):
    ch = SSD_CHUNK
    dt = dt_ref[0]
    adt = dt * (-jnp.exp(alog_ref[dr:dr + 1, :]) * LOG2E)
    row = lax.broadcasted_iota(jnp.int32, (ch, ch), 0)
    col = lax.broadcasted_iota(jnp.int32, (ch, ch), 1)
    mask = (row <= col) if reverse else (row >= col)
    cs = jnp.dot(mask.astype(F32), adt, preferred_element_type=F32, precision=HIGHEST)
    last = 0 if reverse else ch - 1
    cs_t = cs.T
    dt_t = dt.T
    w_t = jnp.exp2(cs_t[:, last:last + 1] - cs_t) * dt_t
    crow = cs_t - jnp.log2(dt_t)
    e_tot = jnp.exp2(cs[last:last + 1, :])
    return mask, cs, w_t, crow, e_tot


def _ssd_group(dr, g, xbc_ref, s_scr, inner):
    gn = SSD_GROUPS * SSD_STATE
    bg = xbc_ref[0, :, inner + g * SSD_STATE:inner + (g + 1) * SSD_STATE]
    cg = xbc_ref[0, :, inner + gn + g * SSD_STATE:inner + gn + (g + 1) * SSD_STATE]
    cb = lax.dot_general(cg, bg, (((1,), (1,)), ((), ())), preferred_element_type=F32).astype(BF16)
    bg_t = bg.astype(F32).T.astype(BF16)
    s_g = s_scr[dr, g]
    y_off = jnp.dot(cg, s_g.astype(BF16), preferred_element_type=F32)
    return cb, bg_t, s_g, y_off


def _ssd_pair(dr, g, pp, n_heads, pro, grp, xbc_ref, s_scr, y_ref):
    ch = SSD_CHUNK
    mask, cs, w_t, crow, e_tot = pro
    cb, bg_t, s_g, y_off = grp
    pairs_per_group = n_heads // SSD_GROUPS // 2
    p = g * pairs_per_group + pp
    h1 = dr * n_heads + 2 * p
    ls = slice(pp * LANES, (pp + 1) * LANES)
    left = lax.broadcasted_iota(jnp.int32, (ch, LANES), 1) < SSD_HEAD_DIM
    x2 = xbc_ref[0, :, p * LANES:(p + 1) * LANES]
    zero = jnp.zeros_like(x2)
    wx = jnp.concatenate([jnp.where(left, x2, zero), jnp.where(left, zero, x2)], axis=0)
    ms, bws, cols = [], [], []
    for h in (h1, h1 + 1):
        ccol = jnp.broadcast_to(cs[:, h:h + 1], (ch, ch))
        cols.append(ccol)
        ms.append(cb * jnp.exp2(jnp.where(mask, ccol - crow[h:h + 1, :], -jnp.inf)).astype(BF16))
        bws.append(bg_t * w_t[h:h + 1, :].astype(BF16))
    lhs = jnp.concatenate([jnp.concatenate(ms, axis=1), jnp.concatenate(bws, axis=1)], axis=0)
    r = jnp.dot(lhs, wx, preferred_element_type=F32)
    y = r[0:ch] + jnp.exp2(jnp.where(left, cols[0], cols[1])) * y_off[:, ls]
    y_ref[0, :, p * LANES:(p + 1) * LANES] = y.astype(BF16)
    dec = jnp.where(left[0:1, :], e_tot[:, h1:h1 + 1], e_tot[:, h1 + 1:h1 + 2])
    s_scr[dr, g, :, ls] = s_g[:, ls] * dec + r[ch:2 * ch]


def _ssd_kernel(n_ctx, xfl_ref, xfc_ref, xbl_ref, xbc_ref, dfl_ref, dfc_ref, dbl_ref, dbc_ref, alog_ref,
                yf_ref, yb_ref, s_scr):
    i = pl.program_id(1)

    @pl.when(i == 0)
    def _():
        s_scr[...] = jnp.zeros_like(s_scr)

    inner = yf_ref.shape[-1]
    n_heads = inner // SSD_HEAD_DIM

    def chunk_pair(xf_ref, df_ref, xb_ref, db_ref):
        dirs = ((xf_ref, df_ref, yf_ref, False), (xb_ref, db_ref, yb_ref, True))
        pros = [_ssd_prologue(rev, dr, d_ref, alog_ref) for dr, (_, d_ref, _, rev) in enumerate(dirs)]
        for g in range(SSD_GROUPS):
            grps = [_ssd_group(dr, g, x_ref, s_scr, inner) for dr, (x_ref, _, _, _) in enumerate(dirs)]
            for pp in range(n_heads // SSD_GROUPS // 2):
                for dr, (x_ref, _, y_ref, _) in enumerate(dirs):
                    _ssd_pair(dr, g, pp, n_heads, pros[dr], grps[dr], x_ref, s_scr, y_ref)

    pl.when(i < n_ctx)(lambda: chunk_pair(xfc_ref, dfc_ref, xbc_ref, dbc_ref))
    pl.when(i >= n_ctx)(lambda: chunk_pair(xfl_ref, dfl_ref, xbl_ref, dbl_ref))


def _ssd(xbc_lat, dt_lat, xbc_ctx, dt_ctx, a_log):
    bsz, s, nx = xbc_lat.shape
    nd = dt_lat.shape[-1]
    n_heads = a_log.shape[-1]
    inner = n_heads * SSD_HEAD_DIM
    a_log = jnp.stack([jnp.pad(a_log[dr], (dr * n_heads, nd - (dr + 1) * n_heads)) for dr in range(2)])
    ch = SSD_CHUNK
    n_lat = s // ch
    n_ctx = xbc_ctx.shape[1] // ch
    n_all = n_lat + n_ctx
    fwd = lambda b, i: (b, i, 0)
    bwd = lambda b, i: (b, _bwd_tile(i, n_ctx, n_all), 0)
    lead = lambda b, i: (b,)
    return pl.pallas_call(
        functools.partial(_ssd_kernel, n_ctx),
        grid=(bsz, n_all),
        in_specs=_walk_specs((1, ch, nx), n_ctx, n_lat, lead) + _walk_specs((1, ch, nd), n_ctx, n_lat, lead)
        + [pl.BlockSpec(a_log.shape, lambda b, i: (0, 0))],
        out_specs=[pl.BlockSpec((1, ch, inner), fwd), pl.BlockSpec((1, ch, inner), bwd)],
        out_shape=[jax.ShapeDtypeStruct((bsz, n_all * ch, inner), BF16)] * 2,
        scratch_shapes=[pltpu.VMEM((2, SSD_GROUPS, SSD_STATE, inner // SSD_GROUPS), F32)],
        compiler_params=_cparams(("arbitrary", "arbitrary")),
        name="ssd",
    )(xbc_lat, xbc_ctx, xbc_lat, xbc_ctx, dt_lat, dt_ctx, dt_lat, dt_ctx, a_log)


def _finish1_kernel(yf_ref, yb_ref, xs_ref, gz_ref, h_ref, mod_ref, dsk_ref, nw_ref, w_ref, g_ref, o_ref):
    d = o_ref.shape[-1]
    inner = yf_ref.shape[-1]
    gw = inner // SSD_GROUPS
    gt = mod_ref[0, :, 2 * d:3 * d]
    for r0 in range(0, o_ref.shape[1], TM):
        rs = slice(r0, r0 + TM)
        out = None
        for g in range(SSD_GROUPS):
            sl = slice(g * gw, (g + 1) * gw)
            y = (yf_ref[0, rs, sl] + yb_ref[0, rs, sl]).astype(F32) + dsk_ref[:, sl] * xs_ref[0, rs, sl].astype(F32)
            y = y * gz_ref[0, rs, sl].astype(F32)
            t = jnp.dot((_rms(y) * nw_ref[:, sl]).astype(BF16), w_ref[sl, :], preferred_element_type=F32)
            out = t if out is None else out + t
        o_ref[0, rs, :] = h_ref[0, rs, :] + gt * (_rms(out) * g_ref[...])


def _finish1(yf, yb, xbc, gz, h, mod, d_skip, norm_w, w_out, g_post):
    bsz, t_all, inner = yf.shape
    d = h.shape[-1]
    rows = FINISH_TM
    n_lat = h.shape[1] // rows
    ctx_len = t_all - h.shape[1]
    own = lambda width: pl.BlockSpec((1, rows, width), lambda j, b: (b, j, 0))
    const2 = lambda j, b: (0, 0)
    return pl.pallas_call(
        _finish1_kernel,
        grid=(n_lat, bsz),
        in_specs=[_uni_lat_spec(rows, inner, ctx_len)] * 2 + [own(inner), own(inner)] + [
            pl.BlockSpec((1, rows, d), lambda j, b: (b, j, 0)),
            pl.BlockSpec((1, 1, 3 * d), lambda j, b: (b, 0, 0)),
            pl.BlockSpec((1, inner), const2),
            pl.BlockSpec((1, inner), const2),
            pl.BlockSpec(w_out.shape, const2),
            pl.BlockSpec((1, d), const2),
        ],
        out_specs=pl.BlockSpec((1, rows, d), lambda j, b: (b, j, 0)),
        out_shape=jax.ShapeDtypeStruct((bsz, n_lat * rows, d), F32),
        compiler_params=_cparams(("arbitrary", "arbitrary")),
        name="finish1",
    )(yf, yb, xbc, gz, h, mod, d_skip, norm_w, w_out, g_post.reshape(1, d))


def _rope_tables(n_tokens):
    rows = n_tokens // GRID_W
    row = jnp.repeat(jnp.arange(rows, dtype=F32), GRID_W)
    col = jnp.tile(jnp.arange(GRID_W, dtype=F32), rows)
    n_freq = DA_HEAD_DIM // 4
    inv = ROPE_BASE ** (-jnp.arange(n_freq, dtype=F32) / n_freq)
    ang = jnp.concatenate([row[:, None] * inv, col[:, None] * inv], axis=-1)
    cos, sin = jnp.cos(ang), jnp.sin(ang)
    cr, cc, sr, sc = cos[:, :n_freq], cos[:, n_freq:], sin[:, :n_freq], sin[:, n_freq:]
    zr = jnp.zeros_like(sr)
    c64 = jnp.concatenate([cr, cr, cc, cc], axis=1)
    s1 = jnp.concatenate([-sr, zr, -sc, zr], axis=1)
    s2 = jnp.concatenate([zr, sr, zr, sc], axis=1)
    reps = LANES // DA_HEAD_DIM
    return tuple(jnp.tile(t, (1, reps)) for t in (c64, s1, s2))


def _block_diag(w):
    n, c, d = w.shape
    eye = jnp.eye(n, dtype=w.dtype)
    return (eye[:, None, :, None] * w[:, :, None, :]).reshape(n * c, n * d)


def kernel(x, c, ctx, c_ctx, w_mod, b_mod, g_pre, g_post, e_w_in, e_w_out, lru_conv_w, lru_conv_b, lru_w_r, lru_b_r, lru_w_i, lru_b_i, lru_lambda, da_lambda, da_subln, o_w_in, o_w_out, ssd_conv_w, ssd_conv_b, ssd_a_log, ssd_dt_bias, ssd_d, ssd_norm):
    bsz, s, d = x.shape
    ctx_len = ctx.shape[1]
    assert bsz == SUBLANES and ctx_len == TM and s % KV_CHUNK == 0 and w_mod.shape[0] == 2
    assert e_w_in.shape[0] == 1 and o_w_in.shape[0] == 1

    n_rows = 2 * SUBLANES
    c_rows = jnp.concatenate([c, c_ctx[None, :], jnp.zeros((n_rows - bsz - 1, d), F32)], axis=0)
    mod = _modulation(c_rows, w_mod, b_mod)
    mod0 = mod[0].reshape(n_rows, 1, 3 * d)
    mod1 = mod[1].reshape(n_rows, 1, 3 * d)

    w5 = lru_conv_w.shape[-1]
    w_in0 = e_w_in[0].astype(BF16)
    (xc, gr, q, k, v, gd), (xc_c, gr_c, q_c, k_c, v_c, gd_c) = _inproj0(
        x, ctx, mod0, g_pre[0], w_in0, lru_conv_w[0], lru_conv_b[0], _rope_tables(s))

    wg = (0.5 * jnp.stack([jnp.concatenate([_block_diag(lru_w_r[0, dr]), _block_diag(lru_w_i[0, dr])], axis=1)
                           for dr in range(2)])).astype(BF16)
    bias = 0.5 * jnp.stack([lru_b_r[0, 0], lru_b_i[0, 0], lru_b_r[0, 1], lru_b_i[0, 1]])
    k4 = (0.5 * LRU_C) * jax.nn.softplus(-lru_lambda[0])
    hf, hb = _lru(xc, xc_c, wg, bias, k4)

    lambda_init = 0.8 - 0.6 * math.exp(-0.3 * 0)
    on, on_c = _attention((q, k, v), (q_c, k_c, v_c), da_lambda[0], da_subln[0], lambda_init)
    h1_lat, h1_ctx = _outproj0(hf, hb, (gr, on, gd), (gr_c, on_c, gd_c), x, ctx, mod0, g_post[0],
                               e_w_out[0].astype(BF16))

    n_heads = ssd_a_log.shape[-1]
    inner = n_heads * SSD_HEAD_DIM
    nx = ssd_conv_w.shape[-1]
    w1 = o_w_in[0]
    wz = w1[:, :inner].astype(BF16)
    wx = w1[:, inner:inner + nx].astype(BF16)
    nd = 2 * n_heads
    wd = jnp.pad(w1[:, inner + nx:], ((0, 0), (0, LANES - nd))).astype(BF16)
    dt_bias = jnp.pad(ssd_dt_bias[0].reshape(1, nd), ((0, 0), (0, LANES - nd)))
    (gz, xbc, dt), (_, xbc_c, dt_c) = _inproj1(h1_lat, h1_ctx, mod1, g_pre[1], wz, wx, wd, ssd_conv_w[0],
                                               ssd_conv_b[0], dt_bias)
    yf, yb = _ssd(xbc, dt, xbc_c, dt_c, ssd_a_log[0])
    d_skip = jnp.repeat(ssd_d[0], SSD_HEAD_DIM).reshape(1, inner)
    return _finish1(yf, yb, xbc, gz, h1_lat, mod1, d_skip, ssd_norm[0].reshape(1, inner),
                    o_w_out[0].astype(BF16), g_post[1])
```

```python
import functools
import math

import jax
import jax.numpy as jnp
from jax import lax
from jax.experimental import pallas as pl
from jax.experimental.pallas import tpu as pltpu

F32 = jnp.float32
BF16 = jnp.bfloat16

EPS = 1e-6
GRID_W = 64
ROPE_BASE = 10000.0
LRU_C = 8.0
LRU_BLOCKS = 8
DA_HEADS = 4
DA_HEAD_DIM = 64
DA_V_DIM = 128
ROPE_QUARTER = DA_HEAD_DIM // 4
SSD_HEAD_DIM = 64
SSD_STATE = 128
SSD_GROUPS = 4
SSD_CHUNK = 128

MOD_COLS = 1024
TM = 256
HALO = 8
LRU_TT = 128
LRU_PITCH = 9
KV_CHUNK = 512
ATTN_TQ = 1024
INPROJ_TM = 512
INPROJ_COLS = 512
OUTPROJ_K = 256
FINISH_TM = 512
OUTPROJ_TM = 512
LANES = 128
SUBLANES = 8
VMEM_LIMIT = 56 * 1024 * 1024

HIGHEST = lax.Precision.HIGHEST
LOG2E = math.log2(math.e)
Q_SCALE = DA_HEAD_DIM ** -0.5 * LOG2E


def _sigmoid(x):
    return 0.5 * (1.0 + jnp.tanh(0.5 * x))


def _silu(x):
    h = 0.5 * x
    return h * (1.0 + jnp.tanh(h))


def _rms(x):
    return x * lax.rsqrt(jnp.mean(x * x, axis=-1, keepdims=True) + EPS)


def _cparams(sem):
    return pltpu.CompilerParams(dimension_semantics=sem, vmem_limit_bytes=VMEM_LIMIT)


def _mod_kernel(c_ref, w_ref, b_ref, o_ref):
    s = _silu(c_ref[...])
    o_ref[0] = jnp.dot(s, w_ref[0], preferred_element_type=F32, precision=HIGHEST) + b_ref[0]


def _modulation(c_rows, w_mod, b_mod):
    depth, d, n3 = w_mod.shape
    rows = c_rows.shape[0]
    nt = MOD_COLS
    return pl.pallas_call(
        _mod_kernel,
        grid=(depth, n3 // nt),
        in_specs=[
            pl.BlockSpec((rows, d), lambda l, n: (0, 0)),
            pl.BlockSpec((1, d, nt), lambda l, n: (l, 0, n)),
            pl.BlockSpec((1, 1, nt), lambda l, n: (l, 0, n)),
        ],
        out_specs=pl.BlockSpec((1, rows, nt), lambda l, n: (l, 0, n)),
        out_shape=jax.ShapeDtypeStruct((depth, rows, n3), F32),
        compiler_params=_cparams(("arbitrary", "arbitrary")),
        name="modulation",
    )(c_rows, w_mod, b_mod.reshape(depth, 1, n3))


def _normed_rows(h_ext, mod_ref, g_ref, d):
    sh = mod_ref[0, :, 0:d]
    sc = mod_ref[0, :, d:2 * d]
    u = (_rms(h_ext) * g_ref[...]) * (1.0 + sc) + sh
    return u.astype(BF16)


def _store_ext(ext_ref, slab0, x_ext, left_ok, right_ok):
    n = x_ext.shape[1] // LANES
    rows = x_ext.shape[0] - 2 * HALO
    zero = jnp.zeros((HALO, LANES), F32)
    for c in range(n):
        blk = x_ext[:, c * LANES:(c + 1) * LANES]
        ext_ref[slab0 + c, 0:HALO, :] = jnp.where(left_ok, blk[0:HALO], zero)
        ext_ref[slab0 + c, HALO:HALO + rows, :] = blk[HALO:HALO + rows]
        ext_ref[slab0 + c, HALO + rows:, :] = jnp.where(right_ok, blk[HALO + rows:], zero)


def _conv4(ext_ref, c, w_ref, b_ref):
    rows = ext_ref.shape[1] - 2 * HALO
    ls = slice(c * LANES, (c + 1) * LANES)
    acc = b_ref[:, ls] + w_ref[0:1, ls] * ext_ref[c, pl.ds(HALO - 2, rows), :]
    acc = acc + w_ref[1:2, ls] * ext_ref[c, pl.ds(HALO - 1, rows), :]
    acc = acc + w_ref[2:3, ls] * ext_ref[c, pl.ds(HALO, rows), :]
    acc = acc + w_ref[3:4, ls] * ext_ref[c, pl.ds(HALO + 1, rows), :]
    return acc


def _segment_edges(is_ctx, n_steps):
    if is_ctx:
        return jnp.bool_(False), jnp.bool_(False)
    j = pl.program_id(0)
    return j >= 1, j <= n_steps - 2


def _uni_lat_spec(rows, width, ctx_len):
    return pl.BlockSpec((pl.Element(1), pl.Element(rows), pl.Element(width)),
                        lambda j, b: (b, pl.multiple_of(ctx_len + j * rows, TM), 0))


def _inproj0_kernel(is_ctx, n_steps, hm_ref, hl_ref, hr_ref, mod_ref, g_ref, w_ref, cw_ref, cb_ref,
                    rc_ref, rs1_ref, rs2_ref, xc_ref, gr_ref, q_ref, k_ref, v_ref, gd_ref, ext_ref):
    d = hm_ref.shape[-1]
    rows = hm_ref.shape[1]
    w5 = xc_ref.shape[-1]
    h_ext = jnp.concatenate([hl_ref[0], hm_ref[0], hr_ref[0]], axis=0)
    u_ext = _normed_rows(h_ext, mod_ref, g_ref, d)
    u = u_ext[HALO:HALO + rows]

    x_ext = jnp.dot(u_ext, w_ref[:, 0:w5], preferred_element_type=F32)
    _store_ext(ext_ref, 0, x_ext, *_segment_edges(is_ctx, n_steps))
    for cs in range(w5 // LANES):
        xc_ref[0, :, cs * LANES:(cs + 1) * LANES] = _conv4(ext_ref, cs, cw_ref, cb_ref)

    gr_ref[0] = jnp.dot(u, w_ref[:, w5:2 * w5], preferred_element_type=F32).astype(BF16)
    reps = w5 // LANES
    for idx, o_ref, post in ((2, q_ref, Q_SCALE), (3, k_ref, None)):
        t = jnp.dot(u, w_ref[:, idx * w5:(idx + 1) * w5], preferred_element_type=F32)
        if not is_ctx:
            c = jnp.concatenate([rc_ref[...]] * reps, axis=1)
            s1 = jnp.concatenate([rs1_ref[...]] * reps, axis=1)
            s2 = jnp.concatenate([rs2_ref[...]] * reps, axis=1)
            t = t * c + pltpu.roll(t, w5 - ROPE_QUARTER, 1) * s1 + pltpu.roll(t, ROPE_QUARTER, 1) * s2
        o_ref[0] = (t if post is None else t * post).astype(BF16)
    v_ref[0] = jnp.dot(u, w_ref[:, 4 * w5:5 * w5], preferred_element_type=F32).astype(BF16)
    gd_ref[0] = jnp.dot(u, w_ref[:, 5 * w5:6 * w5], preferred_element_type=F32).astype(BF16)


def _halo_specs(rows, d, n_rows):
    hb = rows // HALO
    return [pl.BlockSpec((1, rows, d), lambda j, b: (b, j, 0)),
            pl.BlockSpec((1, HALO, d), lambda j, b: (b, jnp.maximum(j * hb - 1, 0), 0)),
            pl.BlockSpec((1, HALO, d), lambda j, b: (b, jnp.minimum((j + 1) * hb, n_rows // HALO - 1), 0))]


def _inproj0(x, ctx, mod, g_pre, w_in, conv_w, conv_b, rope):
    bsz, s, d = x.shape
    w5 = conv_w.shape[-1]
    const2 = lambda j, b: (0, 0)
    consts = [pl.BlockSpec((1, d), const2), pl.BlockSpec(w_in.shape, const2), pl.BlockSpec(conv_w.shape, const2),
              pl.BlockSpec((1, w5), const2)]

    def call(is_ctx, src, rows, mod_row, name):
        n_rows = src.shape[1]
        n_steps = n_rows // rows
        rope_spec = pl.BlockSpec((rows, LANES), lambda j, b: (j, 0))
        return pl.pallas_call(
            functools.partial(_inproj0_kernel, is_ctx, n_steps),
            grid=(n_steps, bsz),
            in_specs=_halo_specs(rows, d, n_rows)
            + [pl.BlockSpec((1, 1, 3 * d), lambda j, b: (mod_row(b), 0, 0))] + consts + [rope_spec] * 3,
            out_specs=[pl.BlockSpec((1, rows, w5), lambda j, b: (b, j, 0))] * 6,
            out_shape=[jax.ShapeDtypeStruct((bsz, n_rows, w5), F32)]
            + [jax.ShapeDtypeStruct((bsz, n_rows, w5), BF16)] * 5,
            scratch_shapes=[pltpu.VMEM((w5 // LANES, rows + 2 * HALO, LANES), F32)],
            compiler_params=_cparams(("arbitrary", "arbitrary")),
            name=name,
        )(src, src, src, mod, g_pre.reshape(1, d), w_in, conv_w, conv_b.reshape(1, w5), *rope)

    return (call(False, x, INPROJ_TM, lambda b: b, "inproj0"),
            call(True, ctx, ctx.shape[1], lambda b: bsz, "inproj0_ctx"))


def _sqrt_pos(y):
    return jnp.where(y > 0.0, y * lax.rsqrt(y), 0.0)


def _lru_kernel(bsz, n_ctx, xfl_ref, xfc_ref, xbl_ref, xbc_ref, wg_ref, bias_ref, k4_ref, of_ref, ob_ref,
                g_scr, a_scr, b_scr, hs_scr, h_scr):
    i = pl.program_id(0)
    w5 = xfl_ref.shape[-1]
    nslab = w5 // LANES
    tt = LRU_TT

    @pl.when(i == 0)
    def _():
        h_scr[...] = jnp.zeros_like(h_scr)

    def coefficients(xf_ref, xb_ref):
        for dr, x_ref in enumerate((xf_ref, xb_ref)):
            k4 = k4_ref[dr:dr + 1, :]
            g_scr[...] = jnp.dot(x_ref[...].reshape(bsz * tt, w5).astype(BF16), wg_ref[dr],
                                 preferred_element_type=F32)
            for b in range(bsz):
                xc = x_ref[b]
                tr = jnp.tanh(g_scr[b * tt:(b + 1) * tt, 0:w5] + bias_ref[2 * dr:2 * dr + 1, :])
                ti = jnp.tanh(g_scr[b * tt:(b + 1) * tt, w5:2 * w5] + bias_ref[2 * dr + 1:2 * dr + 2, :])
                nla = k4 * tr + k4
                a = jnp.exp2(nla * (-LOG2E))
                bc = _sqrt_pos(jnp.tanh(nla) * (a * a + 1.0)) * ((0.5 * xc) * (1.0 + ti))
                for c in range(nslab):
                    a_scr[dr, c, pl.ds(b, tt, stride=LRU_PITCH), :] = a[:, c * LANES:(c + 1) * LANES]
                    b_scr[dr, c, pl.ds(b, tt, stride=LRU_PITCH), :] = bc[:, c * LANES:(c + 1) * LANES]

    pl.when(i < n_ctx)(lambda: coefficients(xfc_ref, xbc_ref))
    pl.when(i >= n_ctx)(lambda: coefficients(xfl_ref, xbl_ref))

    def step(t, carry):
        hf, hb = carry
        rf = t * LRU_PITCH
        rb = (tt - 1 - t) * LRU_PITCH
        nf, nb = [], []
        for c in range(nslab):
            h = a_scr[0, c, pl.ds(rf, bsz), :] * hf[c] + b_scr[0, c, pl.ds(rf, bsz), :]
            hs_scr[0, c, pl.ds(rf, bsz), :] = h
            nf.append(h)
            h = a_scr[1, c, pl.ds(rb, bsz), :] * hb[c] + b_scr[1, c, pl.ds(rb, bsz), :]
            hs_scr[1, c, pl.ds(rb, bsz), :] = h
            nb.append(h)
        return tuple(nf), tuple(nb)

    h0f = tuple(h_scr[0, c] for c in range(nslab))
    h0b = tuple(h_scr[1, c] for c in range(nslab))
    hf, hb = lax.fori_loop(0, tt, step, (h0f, h0b), unroll=8)
    for c in range(nslab):
        h_scr[0, c] = hf[c]
        h_scr[1, c] = hb[c]

    for dr, o_ref in enumerate((of_ref, ob_ref)):
        for b in range(bsz):
            for c in range(nslab):
                o_ref[b, :, c * LANES:(c + 1) * LANES] = hs_scr[dr, c, pl.ds(b, tt, stride=LRU_PITCH), :].astype(BF16)


def _bwd_tile(i, n_ctx, n_all):
    return jnp.where(i < n_ctx, n_ctx - 1 - i, n_all - 1 - (i - n_ctx))


def _walk_specs(block, n_ctx, n_lat, lead):
    def spec(tile):
        return pl.BlockSpec(block, lambda *ids: lead(*ids) + (tile(ids[-1]), 0))
    lat_pos = lambda i: jnp.clip(i - n_ctx, 0, n_lat - 1)
    ctx_pos = lambda i: jnp.minimum(i, n_ctx - 1)
    return [spec(lat_pos), spec(ctx_pos), spec(lambda i: n_lat - 1 - lat_pos(i)), spec(lambda i: n_ctx - 1 - ctx_pos(i))]


def _lru(xc_lat, xc_ctx, wg, bias, k4):
    bsz, s, w5 = xc_lat.shape
    tt = LRU_TT
    n_lat = s // tt
    n_ctx = xc_ctx.shape[1] // tt
    n_all = n_lat + n_ctx
    nslab = w5 // LANES
    fwd_spec = pl.BlockSpec((bsz, tt, w5), lambda i: (0, i, 0))
    bwd_spec = pl.BlockSpec((bsz, tt, w5), lambda i: (0, _bwd_tile(i, n_ctx, n_all), 0))
    const = lambda shape: pl.BlockSpec(shape, lambda i: (0,) * len(shape))
    assert bsz <= LRU_PITCH
    coef = pltpu.VMEM((2, nslab, tt * LRU_PITCH, LANES), F32)
    return pl.pallas_call(
        functools.partial(_lru_kernel, bsz, n_ctx),
        grid=(n_all,),
        in_specs=_walk_specs((bsz, tt, w5), n_ctx, n_lat, lambda i: (0,))
        + [const(wg.shape), const(bias.shape), const(k4.shape)],
        out_specs=[fwd_spec, bwd_spec],
        out_shape=[jax.ShapeDtypeStruct((bsz, n_all * tt, w5), BF16)] * 2,
        scratch_shapes=[pltpu.VMEM((bsz * tt, 2 * w5), F32), coef, coef, coef,
                        pltpu.VMEM((2, nslab, bsz, LANES), F32)],
        compiler_params=_cparams(("arbitrary",)),
        name="rglru",
    )(xc_lat, xc_ctx, xc_lat, xc_ctx, wg, bias, k4)


def _attn_kernel(lambda_init, n_streams, n_lat_chunks, q_ref, kc_ref, vc_ref, kl_ref, vl_ref, lam_ref, sub_ref,
                 o_ref, m_scr, acc_scr):
    key_chunks = [(kc_ref, vc_ref, 0, kc_ref.shape[1])]
    key_chunks += [(kl_ref, vl_ref, c * KV_CHUNK, KV_CHUNK) for c in range(n_lat_chunks)]
    qqs = []
    for st in range(n_streams):
        q = q_ref[0, st * TM:(st + 1) * TM, :]
        lane = lax.broadcasted_iota(jnp.int32, q.shape, 1)
        zero = jnp.zeros_like(q)
        qqs.append(jnp.concatenate([jnp.where(lane < DA_HEAD_DIM, q, zero),
                                    jnp.where(lane >= DA_HEAD_DIM, q, zero)], axis=0))
    m_scr[...] = jnp.full_like(m_scr, -jnp.inf)
    acc_scr[...] = jnp.zeros_like(acc_scr)

    for k_ref, v_ref, start, size in key_chunks:
        kc = k_ref[0, start:start + size, :]
        va = jnp.concatenate([v_ref[0, start:start + size, :], jnp.ones((size, LANES), BF16)], axis=1)
        for st in range(n_streams):
            s = lax.dot_general(qqs[st], kc, (((1,), (1,)), ((), ())), preferred_element_type=F32)
            m_prev = m_scr[st]
            m_next = jnp.maximum(m_prev, jnp.max(s, axis=1, keepdims=True))
            p = jnp.exp2(s - jnp.concatenate([m_next] * (size // LANES), axis=1))
            alpha = jnp.exp2(m_prev - m_next)
            acc_scr[st] = (acc_scr[st] * jnp.concatenate([alpha, alpha], axis=1)
                           + jnp.dot(p.astype(BF16), va, preferred_element_type=F32))
            m_scr[st] = m_next

    lm = lam_ref[...]
    lam = (jnp.exp(jnp.sum(lm[0:1] * lm[1:2], axis=1, keepdims=True))
           - jnp.exp(jnp.sum(lm[2:3] * lm[3:4], axis=1, keepdims=True)) + lambda_init)
    for st in range(n_streams):
        o = (acc_scr[st, 0:TM, 0:LANES] / acc_scr[st, 0:TM, LANES:2 * LANES]
             - lam * (acc_scr[st, TM:2 * TM, 0:LANES] / acc_scr[st, TM:2 * TM, LANES:2 * LANES]))
        o_ref[0, st * TM:(st + 1) * TM, :] = ((_rms(o) * sub_ref[...]) * (1.0 - lambda_init)).astype(BF16)


def _attention(qkv_lat, qkv_ctx, da_lambda, da_subln, lambda_init):
    q_lat, k_lat, v_lat = qkv_lat
    q_ctx, k_ctx, v_ctx = qkv_ctx
    bsz, n_lat, w = q_lat.shape
    ctx_len = q_ctx.shape[1]
    hd = DA_V_DIM
    small = [pl.BlockSpec(da_lambda.shape, lambda b, h, j: (0, 0)), pl.BlockSpec((1, hd), lambda b, h, j: (0, 0))]
    sub = da_subln.reshape(1, hd)
    ctx_spec = pl.BlockSpec((1, ctx_len, hd), lambda b, h, j: (b, 0, h))

    def call(q, q_rows, kl, vl, n_lat_chunks, name):
        n_streams = q_rows // TM
        q_spec = pl.BlockSpec((1, q_rows, hd), lambda b, h, j: (b, j, h))
        lat_spec = pl.BlockSpec((1, kl.shape[1], hd), lambda b, h, j: (b, 0, h))
        return pl.pallas_call(
            functools.partial(_attn_kernel, lambda_init, n_streams, n_lat_chunks),
            grid=(bsz, DA_HEADS, q.shape[1] // q_rows),
            in_specs=[q_spec, ctx_spec, ctx_spec, lat_spec, lat_spec] + small,
            out_specs=q_spec,
            out_shape=jax.ShapeDtypeStruct(q.shape, BF16),
            scratch_shapes=[pltpu.VMEM((n_streams, 2 * TM, LANES), F32),
                            pltpu.VMEM((n_streams, 2 * TM, 2 * LANES), F32)],
            compiler_params=_cparams(("arbitrary", "arbitrary", "arbitrary")),
            name=name,
        )(q, k_ctx, v_ctx, kl, vl, da_lambda, sub)

    on_lat = call(q_lat, ATTN_TQ, k_lat, v_lat, n_lat // KV_CHUNK, "diffattn")
    on_ctx = call(q_ctx, ctx_len, k_ctx, v_ctx, 0, "diffattn_ctx")
    return on_lat, on_ctx


def _outproj0_kernel(hf_ref, hb_ref, gr_ref, on_ref, gd_ref, res_ref, mod_ref, g_ref, w_ref, o_ref):
    d = o_ref.shape[-1]
    w5 = hf_ref.shape[-1]
    kc = OUTPROJ_K
    gt = mod_ref[0, :, 2 * d:3 * d]
    halves = [slice(r0, r0 + TM) for r0 in range(0, o_ref.shape[1], TM)]
    ys = [None] * len(halves)
    for c in range(w5 // kc):
        sl = slice(c * kc, (c + 1) * kc)
        for k, rs in enumerate(halves):
            r = hf_ref[0, rs, sl].astype(F32) + hb_ref[0, rs, sl].astype(F32)
            m1 = (r * _silu(gr_ref[0, rs, sl].astype(F32))).astype(BF16)
            m2 = (on_ref[0, rs, sl].astype(F32) * _silu(gd_ref[0, rs, sl].astype(F32))).astype(BF16)
            t = jnp.dot(m1, w_ref[c * kc:(c + 1) * kc, :], preferred_element_type=F32)
            t = t + jnp.dot(m2, w_ref[w5 + c * kc:w5 + (c + 1) * kc, :], preferred_element_type=F32)
            ys[k] = t if ys[k] is None else ys[k] + t
    for k, rs in enumerate(halves):
        o_ref[0, rs, :] = res_ref[0, rs, :] + gt * (_rms(ys[k]) * g_ref[...])


def _outproj0(hf, hb, lat, ctxs, x, ctx, mod, g_post, w_out):
    bsz, _, w5 = hf.shape
    d = x.shape[-1]
    ctx_len = ctx.shape[1]
    const2 = lambda j, b: (0, 0)
    consts = [pl.BlockSpec((1, d), const2), pl.BlockSpec(w_out.shape, const2)]

    def call(rows, uni, parts, res, mod_row, name):
        own = lambda width: pl.BlockSpec((1, rows, width), lambda j, b: (b, j, 0))
        gr, on, gd = parts
        return pl.pallas_call(
            _outproj0_kernel,
            grid=(res.shape[1] // rows, bsz),
            in_specs=[uni, uni, own(w5), own(w5), own(w5), own(d),
                      pl.BlockSpec((1, 1, 3 * d), lambda j, b: (mod_row(b), 0, 0))] + consts,
            out_specs=own(d),
            out_shape=jax.ShapeDtypeStruct(res.shape, F32),
            compiler_params=_cparams(("arbitrary", "arbitrary")),
            name=name,
        )(hf, hb, gr, on, gd, res, mod, g_post.reshape(1, d), w_out)

    h_lat = call(OUTPROJ_TM, _uni_lat_spec(OUTPROJ_TM, w5, ctx_len), lat, x, lambda b: b, "outproj0")
    ctx_uni = pl.BlockSpec((1, ctx_len, w5), lambda j, b: (b, 0, 0))
    h_ctx = call(ctx_len, ctx_uni, ctxs, ctx, lambda b: bsz, "outproj0_ctx")
    return h_lat, h_ctx


def _inproj1_kernel(is_ctx, n_steps, hm_ref, hl_ref, hr_ref, mod_ref, g_ref, wz_ref, wx_ref, wd_ref, cw_ref, cb_ref,
                    db_ref, z_ref, xbc_ref, dt_ref, ext_ref):
    d = hm_ref.shape[-1]
    rows = hm_ref.shape[1]
    h_ext = jnp.concatenate([hl_ref[0], hm_ref[0], hr_ref[0]], axis=0)
    u_ext = _normed_rows(h_ext, mod_ref, g_ref, d)
    u = u_ext[HALO:HALO + rows]
    left_ok, right_ok = _segment_edges(is_ctx, n_steps)
    nz, nx = wz_ref.shape[1], wx_ref.shape[1]
    cw = INPROJ_COLS
    spc = cw // LANES
    n_x, n_z = nx // cw, nz // cw
    for i in range(max(n_x, n_z)):
        if i < n_x:
            x_ext = jnp.dot(u_ext, wx_ref[:, i * cw:(i + 1) * cw], preferred_element_type=F32)
            _store_ext(ext_ref, i * spc, x_ext, left_ok, right_ok)
            for cs in range(i * spc, (i + 1) * spc):
                xbc_ref[0, :, cs * LANES:(cs + 1) * LANES] = _silu(_conv4(ext_ref, cs, cw_ref, cb_ref)).astype(BF16)
        if i < n_z:
            z_ref[0, :, i * cw:(i + 1) * cw] = _silu(jnp.dot(u, wz_ref[:, i * cw:(i + 1) * cw],
                                                             preferred_element_type=F32)).astype(BF16)
    dt_ref[0] = jax.nn.softplus(jnp.dot(u, wd_ref[...], preferred_element_type=F32) + db_ref[...])


def _inproj1(h_lat, h_ctx, mod, g_pre, w_in, nz, wd, conv_w, conv_b, dt_bias):
    bsz, _, d = h_lat.shape
    nx, nd = conv_w.shape[1], wd.shape[1]
    const2 = lambda j, b: (0, 0)
    consts = [pl.BlockSpec((1, d), const2), pl.BlockSpec((d, nz), const2),
              pl.BlockSpec((pl.Element(d), pl.Element(nx)), lambda j, b: (0, nz)),
              pl.BlockSpec(wd.shape, const2), pl.BlockSpec(conv_w.shape, const2), pl.BlockSpec((1, nx), const2),
              pl.BlockSpec((1, nd), const2)]

    def call(is_ctx, src, rows, mod_row, name):
        n_rows = src.shape[1]
        n_steps = n_rows // rows
        return pl.pallas_call(
            functools.partial(_inproj1_kernel, is_ctx, n_steps),
            grid=(n_steps, bsz),
            in_specs=_halo_specs(rows, d, n_rows)
            + [pl.BlockSpec((1, 1, 3 * d), lambda j, b: (mod_row(b), 0, 0))] + consts,
            out_specs=[pl.BlockSpec((1, rows, w), lambda j, b: (b, j, 0)) for w in (nz, nx, nd)],
            out_shape=[jax.ShapeDtypeStruct((bsz, n_rows, nz), BF16), jax.ShapeDtypeStruct((bsz, n_rows, nx), BF16),
                       jax.ShapeDtypeStruct((bsz, n_rows, nd), F32)],
            scratch_shapes=[pltpu.VMEM((nx // LANES, rows + 2 * HALO, LANES), F32)],
            compiler_params=_cparams(("arbitrary", "arbitrary")),
            name=name,
        )(src, src, src, mod, g_pre.reshape(1, d), w_in, w_in, wd, conv_w, conv_b.reshape(1, nx), dt_bias)

    return (call(False, h_lat, INPROJ_TM, lambda b: b, "inproj1"),
            call(True, h_ctx, h_ctx.shape[1], lambda b: bsz, "inproj1_ctx"))


def _ssd_prologue(reverse, dr, dt_ref, alog_ref):
    ch = SSD_CHUNK
    dt = dt_ref[0]
    adt = dt * (-jnp.exp(alog_ref[dr:dr + 1, :]) * LOG2E)
    row = lax.broadcasted_iota(jnp.int32, (ch, ch), 0)
    col = lax.broadcasted_iota(jnp.int32, (ch, ch), 1)
    mask = (row <= col) if reverse else (row >= col)
    cs = jnp.dot(mask.astype(F32), adt, preferred_element_type=F32, precision=HIGHEST)
    last = 0 if reverse else ch - 1
    cs_t = cs.T
    dt_t = dt.T
    w_t = jnp.exp2(cs_t[:, last:last + 1] - cs_t) * dt_t
    crow = cs_t - jnp.log2(dt_t)
    e_tot = jnp.exp2(cs[last:last + 1, :])
    return mask, cs, w_t, crow, e_tot


def _ssd_group(dr, g, xbc_ref, s_scr, inner):
    gn = SSD_GROUPS * SSD_STATE
    bg = xbc_ref[0, :, inner + g * SSD_STATE:inner + (g + 1) * SSD_STATE]
    cg = xbc_ref[0, :, inner + gn + g * SSD_STATE:inner + gn + (g + 1) * SSD_STATE]
    cb = lax.dot_general(cg, bg, (((1,), (1,)), ((), ())), preferred_element_type=F32).astype(BF16)
    bg_t = bg.astype(F32).T.astype(BF16)
    s_g = s_scr[dr, g]
    y_off = jnp.dot(cg, s_g.astype(BF16), preferred_element_type=F32)
    return cb, bg_t, s_g, y_off


def _ssd_pair(dr, g, pp, n_heads, pro, grp, xbc_ref, s_scr, y_ref):
    ch = SSD_CHUNK
    mask, cs, w_t, crow, e_tot = pro
    cb, bg_t, s_g, y_off = grp
    pairs_per_group = n_heads // SSD_GROUPS // 2
    p = g * pairs_per_group + pp
    h1 = dr * n_heads + 2 * p
    ls = slice(pp * LANES, (pp + 1) * LANES)
    left = lax.broadcasted_iota(jnp.int32, (ch, LANES), 1) < SSD_HEAD_DIM
    x2 = xbc_ref[0, :, p * LANES:(p + 1) * LANES]
    zero = jnp.zeros_like(x2)
    wx = jnp.concatenate([jnp.where(left, x2, zero), jnp.where(left, zero, x2)], axis=0)
    ms, bws, cols = [], [], []
    for h in (h1, h1 + 1):
        ccol = jnp.broadcast_to(cs[:, h:h + 1], (ch, ch))
        cols.append(ccol)
        ms.append(cb * jnp.exp2(jnp.where(mask, ccol - crow[h:h + 1, :], -jnp.inf)).astype(BF16))
        bws.append(bg_t * w_t[h:h + 1, :].astype(BF16))
    lhs = jnp.concatenate([jnp.concatenate(ms, axis=1), jnp.concatenate(bws, axis=1)], axis=0)
    r = jnp.dot(lhs, wx, preferred_element_type=F32)
    y = r[0:ch] + jnp.exp2(jnp.where(left, cols[0], cols[1])) * y_off[:, ls]
    y_ref[0, :, p * LANES:(p + 1) * LANES] = y.astype(BF16)
    dec = jnp.where(left[0:1, :], e_tot[:, h1:h1 + 1], e_tot[:, h1 + 1:h1 + 2])
    s_scr[dr, g, :, ls] = s_g[:, ls] * dec + r[ch:2 * ch]


def _ssd_kernel(n_ctx, xfl_ref, xfc_ref, xbl_ref, xbc_ref, dfl_ref, dfc_ref, dbl_ref, dbc_ref, alog_ref,
                yf_ref, yb_ref, s_scr):
    i = pl.program_id(1)

    @pl.when(i == 0)
    def _():
        s_scr[...] = jnp.zeros_like(s_scr)

    inner = yf_ref.shape[-1]
    n_heads = inner // SSD_HEAD_DIM

    def chunk_pair(xf_ref, df_ref, xb_ref, db_ref):
        dirs = ((xf_ref, df_ref, yf_ref, False), (xb_ref, db_ref, yb_ref, True))
        pros = [_ssd_prologue(rev, dr, d_ref, alog_ref) for dr, (_, d_ref, _, rev) in enumerate(dirs)]
        for g in range(SSD_GROUPS):
            grps = [_ssd_group(dr, g, x_ref, s_scr, inner) for dr, (x_ref, _, _, _) in enumerate(dirs)]
            for pp in range(n_heads // SSD_GROUPS // 2):
                for dr, (x_ref, _, y_ref, _) in enumerate(dirs):
                    _ssd_pair(dr, g, pp, n_heads, pros[dr], grps[dr], x_ref, s_scr, y_ref)

    pl.when(i < n_ctx)(lambda: chunk_pair(xfc_ref, dfc_ref, xbc_ref, dbc_ref))
    pl.when(i >= n_ctx)(lambda: chunk_pair(xfl_ref, dfl_ref, xbl_ref, dbl_ref))


def _ssd(xbc_lat, dt_lat, xbc_ctx, dt_ctx, a_log):
    bsz, s, nx = xbc_lat.shape
    nd = dt_lat.shape[-1]
    n_heads = a_log.shape[-1]
    inner = n_heads * SSD_HEAD_DIM
    a_log = jnp.stack([jnp.pad(a_log[dr], (dr * n_heads, nd - (dr + 1) * n_heads)) for dr in range(2)])
    ch = SSD_CHUNK
    n_lat = s // ch
    n_ctx = xbc_ctx.shape[1] // ch
    n_all = n_lat + n_ctx
    fwd = lambda b, i: (b, i, 0)
    bwd = lambda b, i: (b, _bwd_tile(i, n_ctx, n_all), 0)
    lead = lambda b, i: (b,)
    return pl.pallas_call(
        functools.partial(_ssd_kernel, n_ctx),
        grid=(bsz, n_all),
        in_specs=_walk_specs((1, ch, nx), n_ctx, n_lat, lead) + _walk_specs((1, ch, nd), n_ctx, n_lat, lead)
        + [pl.BlockSpec(a_log.shape, lambda b, i: (0, 0))],
        out_specs=[pl.BlockSpec((1, ch, inner), fwd), pl.BlockSpec((1, ch, inner), bwd)],
        out_shape=[jax.ShapeDtypeStruct((bsz, n_all * ch, inner), BF16)] * 2,
        scratch_shapes=[pltpu.VMEM((2, SSD_GROUPS, SSD_STATE, inner // SSD_GROUPS), F32)],
        compiler_params=_cparams(("arbitrary", "arbitrary")),
        name="ssd",
    )(xbc_lat, xbc_ctx, xbc_lat, xbc_ctx, dt_lat, dt_ctx, dt_lat, dt_ctx, a_log)


def _finish1_kernel(yf_ref, yb_ref, xs_ref, gz_ref, h_ref, mod_ref, dsk_ref, nw_ref, w_ref, g_ref, o_ref):
    d = o_ref.shape[-1]
    inner = yf_ref.shape[-1]
    gw = inner // SSD_GROUPS
    gt = mod_ref[0, :, 2 * d:3 * d]
    halves = [slice(r0, r0 + TM) for r0 in range(0, o_ref.shape[1], TM)]
    outs = [None] * len(halves)
    for g in range(SSD_GROUPS):
        sl = slice(g * gw, (g + 1) * gw)
        for k, rs in enumerate(halves):
            y = (yf_ref[0, rs, sl] + yb_ref[0, rs, sl]).astype(F32) + dsk_ref[:, sl] * xs_ref[0, rs, sl].astype(F32)
            y = y * gz_ref[0, rs, sl].astype(F32)
            t = jnp.dot((_rms(y) * nw_ref[:, sl]).astype(BF16), w_ref[sl, :], preferred_element_type=F32)
            outs[k] = t if outs[k] is None else outs[k] + t
    for k, rs in enumerate(halves):
        o_ref[0, rs, :] = h_ref[0, rs, :] + gt * (_rms(outs[k]) * g_ref[...])


def _finish1(yf, yb, xbc, gz, h, mod, d_skip, norm_w, w_out, g_post):
    bsz, t_all, inner = yf.shape
    d = h.shape[-1]
    rows = FINISH_TM
    n_lat = h.shape[1] // rows
    ctx_len = t_all - h.shape[1]
    own = lambda width: pl.BlockSpec((1, rows, width), lambda j, b: (b, j, 0))
    const2 = lambda j, b: (0, 0)
    return pl.pallas_call(
        _finish1_kernel,
        grid=(n_lat, bsz),
        in_specs=[_uni_lat_spec(rows, inner, ctx_len)] * 2 + [own(inner), own(inner)] + [
            pl.BlockSpec((1, rows, d), lambda j, b: (b, j, 0)),
            pl.BlockSpec((1, 1, 3 * d), lambda j, b: (b, 0, 0)),
            pl.BlockSpec((1, inner), const2),
            pl.BlockSpec((1, inner), const2),
            pl.BlockSpec(w_out.shape, const2),
            pl.BlockSpec((1, d), const2),
        ],
        out_specs=pl.BlockSpec((1, rows, d), lambda j, b: (b, j, 0)),
        out_shape=jax.ShapeDtypeStruct((bsz, n_lat * rows, d), F32),
        compiler_params=_cparams(("arbitrary", "arbitrary")),
        name="finish1",
    )(yf, yb, xbc, gz, h, mod, d_skip, norm_w, w_out, g_post.reshape(1, d))


def _rope_tables(n_tokens):
    rows = n_tokens // GRID_W
    row = jnp.repeat(jnp.arange(rows, dtype=F32), GRID_W)
    col = jnp.tile(jnp.arange(GRID_W, dtype=F32), rows)
    n_freq = DA_HEAD_DIM // 4
    inv = ROPE_BASE ** (-jnp.arange(n_freq, dtype=F32) / n_freq)
    ang = jnp.concatenate([row[:, None] * inv, col[:, None] * inv], axis=-1)
    cos, sin = jnp.cos(ang), jnp.sin(ang)
    cr, cc, sr, sc = cos[:, :n_freq], cos[:, n_freq:], sin[:, :n_freq], sin[:, n_freq:]
    zr = jnp.zeros_like(sr)
    c64 = jnp.concatenate([cr, cr, cc, cc], axis=1)
    s1 = jnp.concatenate([-sr, zr, -sc, zr], axis=1)
    s2 = jnp.concatenate([zr, sr, zr, sc], axis=1)
    reps = LANES // DA_HEAD_DIM
    return tuple(jnp.tile(t, (1, reps)) for t in (c64, s1, s2))


def _block_diag(w):
    n, c, d = w.shape
    eye = jnp.eye(n, dtype=w.dtype)
    return (eye[:, None, :, None] * w[:, :, None, :]).reshape(n * c, n * d)


def kernel(x, c, ctx, c_ctx, w_mod, b_mod, g_pre, g_post, e_w_in, e_w_out, lru_conv_w, lru_conv_b, lru_w_r, lru_b_r, lru_w_i, lru_b_i, lru_lambda, da_lambda, da_subln, o_w_in, o_w_out, ssd_conv_w, ssd_conv_b, ssd_a_log, ssd_dt_bias, ssd_d, ssd_norm):
    bsz, s, d = x.shape
    ctx_len = ctx.shape[1]
    assert bsz == SUBLANES and ctx_len == TM and s % KV_CHUNK == 0 and w_mod.shape[0] == 2
    assert e_w_in.shape[0] == 1 and o_w_in.shape[0] == 1

    n_rows = 2 * SUBLANES
    c_rows = jnp.concatenate([c, c_ctx[None, :], jnp.zeros((n_rows - bsz - 1, d), F32)], axis=0)
    mod = _modulation(c_rows, w_mod, b_mod)
    mod0 = mod[0].reshape(n_rows, 1, 3 * d)
    mod1 = mod[1].reshape(n_rows, 1, 3 * d)

    w5 = lru_conv_w.shape[-1]
    w_in0 = e_w_in[0].astype(BF16)
    (xc, gr, q, k, v, gd), (xc_c, gr_c, q_c, k_c, v_c, gd_c) = _inproj0(
        x, ctx, mod0, g_pre[0], w_in0, lru_conv_w[0], lru_conv_b[0], _rope_tables(s))

    wg = (0.5 * jnp.stack([jnp.concatenate([_block_diag(lru_w_r[0, dr]), _block_diag(lru_w_i[0, dr])], axis=1)
                           for dr in range(2)])).astype(BF16)
    bias = 0.5 * jnp.stack([lru_b_r[0, 0], lru_b_i[0, 0], lru_b_r[0, 1], lru_b_i[0, 1]])
    k4 = (0.5 * LRU_C) * jax.nn.softplus(-lru_lambda[0])
    hf, hb = _lru(xc, xc_c, wg, bias, k4)

    lambda_init = 0.8 - 0.6 * math.exp(-0.3 * 0)
    on, on_c = _attention((q, k, v), (q_c, k_c, v_c), da_lambda[0], da_subln[0], lambda_init)
    h1_lat, h1_ctx = _outproj0(hf, hb, (gr, on, gd), (gr_c, on_c, gd_c), x, ctx, mod0, g_post[0],
                               e_w_out[0].astype(BF16))

    n_heads = ssd_a_log.shape[-1]
    inner = n_heads * SSD_HEAD_DIM
    nx = ssd_conv_w.shape[-1]
    w1 = o_w_in[0].astype(BF16)
    nd = 2 * n_heads
    wd = jnp.pad(w1[:, inner + nx:], ((0, 0), (0, LANES - nd)))
    dt_bias = jnp.pad(ssd_dt_bias[0].reshape(1, nd), ((0, 0), (0, LANES - nd)))
    (gz, xbc, dt), (_, xbc_c, dt_c) = _inproj1(h1_lat, h1_ctx, mod1, g_pre[1], w1, inner, wd, ssd_conv_w[0],
                                               ssd_conv_b[0], dt_bias)
    yf, yb = _ssd(xbc, dt, xbc_c, dt_c, ssd_a_log[0])
    d_skip = jnp.repeat(ssd_d[0], SSD_HEAD_DIM).reshape(1, inner)
    return _finish1(yf, yb, xbc, gz, h1_lat, mod1, d_skip, ssd_norm[0].reshape(1, inner),
                    o_w_out[0].astype(BF16), g_post[1])
```

```python
import functools
import math

import jax
import jax.numpy as jnp
from jax import lax
from jax.experimental import pallas as pl
from jax.experimental.pallas import tpu as pltpu

F32 = jnp.float32
BF16 = jnp.bfloat16

EPS = 1e-6
GRID_W = 64
ROPE_BASE = 10000.0
LRU_C = 8.0
LRU_BLOCKS = 8
DA_HEADS = 4
DA_HEAD_DIM = 64
DA_V_DIM = 128
ROPE_QUARTER = DA_HEAD_DIM // 4
SSD_HEAD_DIM = 64
SSD_STATE = 128
SSD_GROUPS = 4
SSD_CHUNK = 128

MOD_COLS = 1024
TM = 256
HALO = 8
LRU_TT = 128
LRU_PITCH = 9
KV_CHUNK = 512
ATTN_TQ = 1024
INPROJ_TM = 512
INPROJ_COLS = 512
OUTPROJ_K = 256
FINISH_TM = 512
OUTPROJ_TM = 512
LANES = 128
SUBLANES = 8
VMEM_LIMIT = 56 * 1024 * 1024

HIGHEST = lax.Precision.HIGHEST
LOG2E = math.log2(math.e)
Q_SCALE = DA_HEAD_DIM ** -0.5 * LOG2E


def _sigmoid(x):
    return 0.5 * (1.0 + jnp.tanh(0.5 * x))


def _silu(x):
    h = 0.5 * x
    return h * (1.0 + jnp.tanh(h))


def _rms(x):
    return x * lax.rsqrt(jnp.mean(x * x, axis=-1, keepdims=True) + EPS)


def _cparams(sem):
    return pltpu.CompilerParams(dimension_semantics=sem, vmem_limit_bytes=VMEM_LIMIT)


def _mod_kernel(c_ref, w_ref, b_ref, o_ref):
    s = _silu(c_ref[...])
    o_ref[0] = jnp.dot(s, w_ref[0], preferred_element_type=F32, precision=HIGHEST) + b_ref[0]


def _modulation(c_rows, w_mod, b_mod):
    depth, d, n3 = w_mod.shape
    rows = c_rows.shape[0]
    nt = MOD_COLS
    return pl.pallas_call(
        _mod_kernel,
        grid=(depth, n3 // nt),
        in_specs=[
            pl.BlockSpec((rows, d), lambda l, n: (0, 0)),
            pl.BlockSpec((1, d, nt), lambda l, n: (l, 0, n)),
            pl.BlockSpec((1, 1, nt), lambda l, n: (l, 0, n)),
        ],
        out_specs=pl.BlockSpec((1, rows, nt), lambda l, n: (l, 0, n)),
        out_shape=jax.ShapeDtypeStruct((depth, rows, n3), F32),
        compiler_params=_cparams(("arbitrary", "arbitrary")),
        name="modulation",
    )(c_rows, w_mod, b_mod.reshape(depth, 1, n3))


def _normed_rows(h_ext, mod_ref, g_ref, d):
    sh = mod_ref[0, :, 0:d]
    sc = mod_ref[0, :, d:2 * d]
    u = (_rms(h_ext) * g_ref[...]) * (1.0 + sc) + sh
    return u.astype(BF16)


def _store_ext(ext_ref, slab0, x_ext, left_ok, right_ok):
    n = x_ext.shape[1] // LANES
    rows = x_ext.shape[0] - 2 * HALO
    zero = jnp.zeros((HALO, LANES), F32)
    for c in range(n):
        blk = x_ext[:, c * LANES:(c + 1) * LANES]
        ext_ref[slab0 + c, 0:HALO, :] = jnp.where(left_ok, blk[0:HALO], zero)
        ext_ref[slab0 + c, HALO:HALO + rows, :] = blk[HALO:HALO + rows]
        ext_ref[slab0 + c, HALO + rows:, :] = jnp.where(right_ok, blk[HALO + rows:], zero)


def _conv4(ext_ref, c, w_ref, b_ref):
    rows = ext_ref.shape[1] - 2 * HALO
    ls = slice(c * LANES, (c + 1) * LANES)
    acc = b_ref[:, ls] + w_ref[0:1, ls] * ext_ref[c, pl.ds(HALO - 2, rows), :]
    acc = acc + w_ref[1:2, ls] * ext_ref[c, pl.ds(HALO - 1, rows), :]
    acc = acc + w_ref[2:3, ls] * ext_ref[c, pl.ds(HALO, rows), :]
    acc = acc + w_ref[3:4, ls] * ext_ref[c, pl.ds(HALO + 1, rows), :]
    return acc


def _segment_edges(is_ctx, n_steps):
    if is_ctx:
        return jnp.bool_(False), jnp.bool_(False)
    j = pl.program_id(0)
    return j >= 1, j <= n_steps - 2


def _uni_lat_spec(rows, width, ctx_len):
    return pl.BlockSpec((pl.Element(1), pl.Element(rows), pl.Element(width)),
                        lambda j, b: (b, pl.multiple_of(ctx_len + j * rows, TM), 0))


def _inproj0_kernel(is_ctx, n_steps, hm_ref, hl_ref, hr_ref, mod_ref, g_ref, w_ref, cw_ref, cb_ref,
                    rc_ref, rs1_ref, rs2_ref, xc_ref, gr_ref, q_ref, k_ref, v_ref, gd_ref, ext_ref):
    d = hm_ref.shape[-1]
    rows = hm_ref.shape[1]
    w5 = xc_ref.shape[-1]
    h_ext = jnp.concatenate([hl_ref[0], hm_ref[0], hr_ref[0]], axis=0)
    u_ext = _normed_rows(h_ext, mod_ref, g_ref, d)
    u = u_ext[HALO:HALO + rows]

    x_ext = jnp.dot(u_ext, w_ref[:, 0:w5], preferred_element_type=F32)
    _store_ext(ext_ref, 0, x_ext, *_segment_edges(is_ctx, n_steps))
    for cs in range(w5 // LANES):
        xc_ref[0, :, cs * LANES:(cs + 1) * LANES] = _conv4(ext_ref, cs, cw_ref, cb_ref)

    gr_ref[0] = jnp.dot(u, w_ref[:, w5:2 * w5], preferred_element_type=F32).astype(BF16)
    reps = w5 // LANES
    for idx, o_ref, post in ((2, q_ref, Q_SCALE), (3, k_ref, None)):
        t = jnp.dot(u, w_ref[:, idx * w5:(idx + 1) * w5], preferred_element_type=F32)
        if not is_ctx:
            c = jnp.concatenate([rc_ref[...]] * reps, axis=1)
            s1 = jnp.concatenate([rs1_ref[...]] * reps, axis=1)
            s2 = jnp.concatenate([rs2_ref[...]] * reps, axis=1)
            t = t * c + pltpu.roll(t, w5 - ROPE_QUARTER, 1) * s1 + pltpu.roll(t, ROPE_QUARTER, 1) * s2
        o_ref[0] = (t if post is None else t * post).astype(BF16)
    v_ref[0] = jnp.dot(u, w_ref[:, 4 * w5:5 * w5], preferred_element_type=F32).astype(BF16)
    gd_ref[0] = jnp.dot(u, w_ref[:, 5 * w5:6 * w5], preferred_element_type=F32).astype(BF16)


def _halo_specs(rows, d, n_rows):
    hb = rows // HALO
    return [pl.BlockSpec((1, rows, d), lambda j, b: (b, j, 0)),
            pl.BlockSpec((1, HALO, d), lambda j, b: (b, jnp.maximum(j * hb - 1, 0), 0)),
            pl.BlockSpec((1, HALO, d), lambda j, b: (b, jnp.minimum((j + 1) * hb, n_rows // HALO - 1), 0))]


def _inproj0(x, ctx, mod, g_pre, w_in, conv_w, conv_b, rope):
    bsz, s, d = x.shape
    w5 = conv_w.shape[-1]
    const2 = lambda j, b: (0, 0)
    consts = [pl.BlockSpec((1, d), const2), pl.BlockSpec(w_in.shape, const2), pl.BlockSpec(conv_w.shape, const2),
              pl.BlockSpec((1, w5), const2)]

    def call(is_ctx, src, rows, mod_row, name):
        n_rows = src.shape[1]
        n_steps = n_rows // rows
        rope_spec = pl.BlockSpec((rows, LANES), lambda j, b: (j, 0))
        return pl.pallas_call(
            functools.partial(_inproj0_kernel, is_ctx, n_steps),
            grid=(n_steps, bsz),
            in_specs=_halo_specs(rows, d, n_rows)
            + [pl.BlockSpec((1, 1, 3 * d), lambda j, b: (mod_row(b), 0, 0))] + consts + [rope_spec] * 3,
            out_specs=[pl.BlockSpec((1, rows, w5), lambda j, b: (b, j, 0))] * 6,
            out_shape=[jax.ShapeDtypeStruct((bsz, n_rows, w5), F32)]
            + [jax.ShapeDtypeStruct((bsz, n_rows, w5), BF16)] * 5,
            scratch_shapes=[pltpu.VMEM((w5 // LANES, rows + 2 * HALO, LANES), F32)],
            compiler_params=_cparams(("arbitrary", "arbitrary")),
            name=name,
        )(src, src, src, mod, g_pre.reshape(1, d), w_in, conv_w, conv_b.reshape(1, w5), *rope)

    return (call(False, x, INPROJ_TM, lambda b: b, "inproj0"),
            call(True, ctx, ctx.shape[1], lambda b: bsz, "inproj0_ctx"))


def _sqrt_pos(y):
    return jnp.where(y > 0.0, y * lax.rsqrt(y), 0.0)


def _lru_kernel(bsz, n_ctx, xfl_ref, xfc_ref, xbl_ref, xbc_ref, wg_ref, bias_ref, k4_ref, of_ref, ob_ref,
                g_scr, a_scr, b_scr, hs_scr, h_scr):
    i = pl.program_id(0)
    w5 = xfl_ref.shape[-1]
    nslab = w5 // LANES
    tt = LRU_TT

    @pl.when(i == 0)
    def _():
        h_scr[...] = jnp.zeros_like(h_scr)

    def coefficients(xf_ref, xb_ref):
        for dr, x_ref in enumerate((xf_ref, xb_ref)):
            k4 = k4_ref[dr:dr + 1, :]
            g_scr[...] = jnp.dot(x_ref[...].reshape(bsz * tt, w5).astype(BF16), wg_ref[dr],
                                 preferred_element_type=F32)
            for b in range(bsz):
                xc = x_ref[b]
                tr = jnp.tanh(g_scr[b * tt:(b + 1) * tt, 0:w5] + bias_ref[2 * dr:2 * dr + 1, :])
                ti = jnp.tanh(g_scr[b * tt:(b + 1) * tt, w5:2 * w5] + bias_ref[2 * dr + 1:2 * dr + 2, :])
                nla = k4 * tr + k4
                a = jnp.exp2(nla * (-LOG2E))
                bc = _sqrt_pos(jnp.tanh(nla) * (a * a + 1.0)) * ((0.5 * xc) * (1.0 + ti))
                for c in range(nslab):
                    a_scr[dr, c, pl.ds(b, tt, stride=LRU_PITCH), :] = a[:, c * LANES:(c + 1) * LANES]
                    b_scr[dr, c, pl.ds(b, tt, stride=LRU_PITCH), :] = bc[:, c * LANES:(c + 1) * LANES]

    pl.when(i < n_ctx)(lambda: coefficients(xfc_ref, xbc_ref))
    pl.when(i >= n_ctx)(lambda: coefficients(xfl_ref, xbl_ref))

    def step(t, carry):
        hf, hb = carry
        rf = t * LRU_PITCH
        rb = (tt - 1 - t) * LRU_PITCH
        nf, nb = [], []
        for c in range(nslab):
            h = a_scr[0, c, pl.ds(rf, bsz), :] * hf[c] + b_scr[0, c, pl.ds(rf, bsz), :]
            hs_scr[0, c, pl.ds(rf, bsz), :] = h
            nf.append(h)
            h = a_scr[1, c, pl.ds(rb, bsz), :] * hb[c] + b_scr[1, c, pl.ds(rb, bsz), :]
            hs_scr[1, c, pl.ds(rb, bsz), :] = h
            nb.append(h)
        return tuple(nf), tuple(nb)

    h0f = tuple(h_scr[0, c] for c in range(nslab))
    h0b = tuple(h_scr[1, c] for c in range(nslab))
    hf, hb = lax.fori_loop(0, tt, step, (h0f, h0b), unroll=8)
    for c in range(nslab):
        h_scr[0, c] = hf[c]
        h_scr[1, c] = hb[c]

    for dr, o_ref in enumerate((of_ref, ob_ref)):
        for b in range(bsz):
            for c in range(nslab):
                o_ref[b, :, c * LANES:(c + 1) * LANES] = hs_scr[dr, c, pl.ds(b, tt, stride=LRU_PITCH), :].astype(BF16)


def _bwd_tile(i, n_ctx, n_all):
    return jnp.where(i < n_ctx, n_ctx - 1 - i, n_all - 1 - (i - n_ctx))


def _walk_specs(block, n_ctx, n_lat, lead):
    def spec(tile):
        return pl.BlockSpec(block, lambda *ids: lead(*ids) + (tile(ids[-1]), 0))
    lat_pos = lambda i: jnp.clip(i - n_ctx, 0, n_lat - 1)
    ctx_pos = lambda i: jnp.minimum(i, n_ctx - 1)
    return [spec(lat_pos), spec(ctx_pos), spec(lambda i: n_lat - 1 - lat_pos(i)), spec(lambda i: n_ctx - 1 - ctx_pos(i))]


def _lru(xc_lat, xc_ctx, wg, bias, k4):
    bsz, s, w5 = xc_lat.shape
    tt = LRU_TT
    n_lat = s // tt
    n_ctx = xc_ctx.shape[1] // tt
    n_all = n_lat + n_ctx
    nslab = w5 // LANES
    fwd_spec = pl.BlockSpec((bsz, tt, w5), lambda i: (0, i, 0))
    bwd_spec = pl.BlockSpec((bsz, tt, w5), lambda i: (0, _bwd_tile(i, n_ctx, n_all), 0))
    const = lambda shape: pl.BlockSpec(shape, lambda i: (0,) * len(shape))
    assert bsz <= LRU_PITCH
    coef = pltpu.VMEM((2, nslab, tt * LRU_PITCH, LANES), F32)
    return pl.pallas_call(
        functools.partial(_lru_kernel, bsz, n_ctx),
        grid=(n_all,),
        in_specs=_walk_specs((bsz, tt, w5), n_ctx, n_lat, lambda i: (0,))
        + [const(wg.shape), const(bias.shape), const(k4.shape)],
        out_specs=[fwd_spec, bwd_spec],
        out_shape=[jax.ShapeDtypeStruct((bsz, n_all * tt, w5), BF16)] * 2,
        scratch_shapes=[pltpu.VMEM((bsz * tt, 2 * w5), F32), coef, coef, coef,
                        pltpu.VMEM((2, nslab, bsz, LANES), F32)],
        compiler_params=_cparams(("arbitrary",)),
        name="rglru",
    )(xc_lat, xc_ctx, xc_lat, xc_ctx, wg, bias, k4)


def _attn_kernel(lambda_init, n_streams, n_lat_chunks, q_ref, kc_ref, vc_ref, kl_ref, vl_ref, lam_ref, sub_ref,
                 o_ref, m_scr, acc_scr):
    key_chunks = [(kc_ref, vc_ref, 0, kc_ref.shape[1])]
    key_chunks += [(kl_ref, vl_ref, c * KV_CHUNK, KV_CHUNK) for c in range(n_lat_chunks)]
    qqs = []
    for st in range(n_streams):
        q = q_ref[0, st * TM:(st + 1) * TM, :]
        lane = lax.broadcasted_iota(jnp.int32, q.shape, 1)
        zero = jnp.zeros_like(q)
        qqs.append(jnp.concatenate([jnp.where(lane < DA_HEAD_DIM, q, zero),
                                    jnp.where(lane >= DA_HEAD_DIM, q, zero)], axis=0))
    m_scr[...] = jnp.full_like(m_scr, -jnp.inf)
    acc_scr[...] = jnp.zeros_like(acc_scr)

    for k_ref, v_ref, start, size in key_chunks:
        kc = k_ref[0, start:start + size, :]
        va = jnp.concatenate([v_ref[0, start:start + size, :], jnp.ones((size, LANES), BF16)], axis=1)
        for st in range(n_streams):
            s = lax.dot_general(qqs[st], kc, (((1,), (1,)), ((), ())), preferred_element_type=F32)
            m_prev = m_scr[st]
            m_next = jnp.maximum(m_prev, jnp.max(s, axis=1, keepdims=True))
            p = jnp.exp2(s - jnp.concatenate([m_next] * (size // LANES), axis=1))
            alpha = jnp.exp2(m_prev - m_next)
            acc_scr[st] = (acc_scr[st] * jnp.concatenate([alpha, alpha], axis=1)
                           + jnp.dot(p.astype(BF16), va, preferred_element_type=F32))
            m_scr[st] = m_next

    lm = lam_ref[...]
    lam = (jnp.exp(jnp.sum(lm[0:1] * lm[1:2], axis=1, keepdims=True))
           - jnp.exp(jnp.sum(lm[2:3] * lm[3:4], axis=1, keepdims=True)) + lambda_init)
    for st in range(n_streams):
        o = (acc_scr[st, 0:TM, 0:LANES] / acc_scr[st, 0:TM, LANES:2 * LANES]
             - lam * (acc_scr[st, TM:2 * TM, 0:LANES] / acc_scr[st, TM:2 * TM, LANES:2 * LANES]))
        o_ref[0, st * TM:(st + 1) * TM, :] = ((_rms(o) * sub_ref[...]) * (1.0 - lambda_init)).astype(BF16)


def _attention(qkv_lat, qkv_ctx, da_lambda, da_subln, lambda_init):
    q_lat, k_lat, v_lat = qkv_lat
    q_ctx, k_ctx, v_ctx = qkv_ctx
    bsz, n_lat, w = q_lat.shape
    ctx_len = q_ctx.shape[1]
    hd = DA_V_DIM
    small = [pl.BlockSpec(da_lambda.shape, lambda b, h, j: (0, 0)), pl.BlockSpec((1, hd), lambda b, h, j: (0, 0))]
    sub = da_subln.reshape(1, hd)
    ctx_spec = pl.BlockSpec((1, ctx_len, hd), lambda b, h, j: (b, 0, h))

    def call(q, q_rows, kl, vl, n_lat_chunks, name):
        n_streams = q_rows // TM
        q_spec = pl.BlockSpec((1, q_rows, hd), lambda b, h, j: (b, j, h))
        lat_spec = pl.BlockSpec((1, kl.shape[1], hd), lambda b, h, j: (b, 0, h))
        return pl.pallas_call(
            functools.partial(_attn_kernel, lambda_init, n_streams, n_lat_chunks),
            grid=(bsz, DA_HEADS, q.shape[1] // q_rows),
            in_specs=[q_spec, ctx_spec, ctx_spec, lat_spec, lat_spec] + small,
            out_specs=q_spec,
            out_shape=jax.ShapeDtypeStruct(q.shape, BF16),
            scratch_shapes=[pltpu.VMEM((n_streams, 2 * TM, LANES), F32),
                            pltpu.VMEM((n_streams, 2 * TM, 2 * LANES), F32)],
            compiler_params=_cparams(("arbitrary", "arbitrary", "arbitrary")),
            name=name,
        )(q, k_ctx, v_ctx, kl, vl, da_lambda, sub)

    on_lat = call(q_lat, ATTN_TQ, k_lat, v_lat, n_lat // KV_CHUNK, "diffattn")
    on_ctx = call(q_ctx, ctx_len, k_ctx, v_ctx, 0, "diffattn_ctx")
    return on_lat, on_ctx


def _outproj0_kernel(hf_ref, hb_ref, gr_ref, on_ref, gd_ref, res_ref, mod_ref, g_ref, w_ref, o_ref):
    d = o_ref.shape[-1]
    w5 = hf_ref.shape[-1]
    kc = OUTPROJ_K
    gt = mod_ref[0, :, 2 * d:3 * d]
    halves = [slice(r0, r0 + TM) for r0 in range(0, o_ref.shape[1], TM)]
    ys = [None] * len(halves)
    for c in range(w5 // kc):
        sl = slice(c * kc, (c + 1) * kc)
        for k, rs in enumerate(halves):
            r = hf_ref[0, rs, sl].astype(F32) + hb_ref[0, rs, sl].astype(F32)
            m1 = (r * _silu(gr_ref[0, rs, sl].astype(F32))).astype(BF16)
            m2 = (on_ref[0, rs, sl].astype(F32) * _silu(gd_ref[0, rs, sl].astype(F32))).astype(BF16)
            t = jnp.dot(m1, w_ref[c * kc:(c + 1) * kc, :], preferred_element_type=F32)
            t = t + jnp.dot(m2, w_ref[w5 + c * kc:w5 + (c + 1) * kc, :], preferred_element_type=F32)
            ys[k] = t if ys[k] is None else ys[k] + t
    for k, rs in enumerate(halves):
        o_ref[0, rs, :] = res_ref[0, rs, :] + gt * (_rms(ys[k]) * g_ref[...])


def _outproj0(hf, hb, lat, ctxs, x, ctx, mod, g_post, w_out):
    bsz, _, w5 = hf.shape
    d = x.shape[-1]
    ctx_len = ctx.shape[1]
    const2 = lambda j, b: (0, 0)
    consts = [pl.BlockSpec((1, d), const2), pl.BlockSpec(w_out.shape, const2)]

    def call(rows, uni, parts, res, mod_row, name):
        own = lambda width: pl.BlockSpec((1, rows, width), lambda j, b: (b, j, 0))
        gr, on, gd = parts
        return pl.pallas_call(
            _outproj0_kernel,
            grid=(res.shape[1] // rows, bsz),
            in_specs=[uni, uni, own(w5), own(w5), own(w5), own(d),
                      pl.BlockSpec((1, 1, 3 * d), lambda j, b: (mod_row(b), 0, 0))] + consts,
            out_specs=own(d),
            out_shape=jax.ShapeDtypeStruct(res.shape, F32),
            compiler_params=_cparams(("arbitrary", "arbitrary")),
            name=name,
        )(hf, hb, gr, on, gd, res, mod, g_post.reshape(1, d), w_out)

    h_lat = call(OUTPROJ_TM, _uni_lat_spec(OUTPROJ_TM, w5, ctx_len), lat, x, lambda b: b, "outproj0")
    ctx_uni = pl.BlockSpec((1, ctx_len, w5), lambda j, b: (b, 0, 0))
    h_ctx = call(ctx_len, ctx_uni, ctxs, ctx, lambda b: bsz, "outproj0_ctx")
    return h_lat, h_ctx


def _inproj1_kernel(is_ctx, n_steps, hm_ref, hl_ref, hr_ref, mod_ref, g_ref, wz_ref, wx_ref, wd_ref, cw_ref, cb_ref,
                    db_ref, z_ref, xbc_ref, dt_ref, ext_ref):
    d = hm_ref.shape[-1]
    rows = hm_ref.shape[1]
    h_ext = jnp.concatenate([hl_ref[0], hm_ref[0], hr_ref[0]], axis=0)
    u_ext = _normed_rows(h_ext, mod_ref, g_ref, d)
    u = u_ext[HALO:HALO + rows]
    left_ok, right_ok = _segment_edges(is_ctx, n_steps)
    nz, nx = wz_ref.shape[1], wx_ref.shape[1]
    cw = INPROJ_COLS
    spc = cw // LANES
    n_x, n_z = nx // cw, nz // cw
    for i in range(max(n_x, n_z)):
        if i < n_x:
            x_ext = jnp.dot(u_ext, wx_ref[:, i * cw:(i + 1) * cw], preferred_element_type=F32)
            _store_ext(ext_ref, i * spc, x_ext, left_ok, right_ok)
            for cs in range(i * spc, (i + 1) * spc):
                xbc_ref[0, :, cs * LANES:(cs + 1) * LANES] = _silu(_conv4(ext_ref, cs, cw_ref, cb_ref)).astype(BF16)
        if i < n_z:
            z_ref[0, :, i * cw:(i + 1) * cw] = _silu(jnp.dot(u, wz_ref[:, i * cw:(i + 1) * cw],
                                                             preferred_element_type=F32)).astype(BF16)
    dt_ref[0] = jax.nn.softplus(jnp.dot(u, wd_ref[...], preferred_element_type=F32) + db_ref[...])


def _inproj1(h_lat, h_ctx, mod, g_pre, w_in, nz, wd, conv_w, conv_b, dt_bias):
    bsz, _, d = h_lat.shape
    nx, nd = conv_w.shape[1], wd.shape[1]
    const2 = lambda j, b: (0, 0)
    consts = [pl.BlockSpec((1, d), const2), pl.BlockSpec((d, nz), const2),
              pl.BlockSpec((pl.Element(d), pl.Element(nx)), lambda j, b: (0, nz)),
              pl.BlockSpec(wd.shape, const2), pl.BlockSpec(conv_w.shape, const2), pl.BlockSpec((1, nx), const2),
              pl.BlockSpec((1, nd), const2)]

    def call(is_ctx, src, rows, mod_row, name):
        n_rows = src.shape[1]
        n_steps = n_rows // rows
        return pl.pallas_call(
            functools.partial(_inproj1_kernel, is_ctx, n_steps),
            grid=(n_steps, bsz),
            in_specs=_halo_specs(rows, d, n_rows)
            + [pl.BlockSpec((1, 1, 3 * d), lambda j, b: (mod_row(b), 0, 0))] + consts,
            out_specs=[pl.BlockSpec((1, rows, w), lambda j, b: (b, j, 0)) for w in (nz, nx, nd)],
            out_shape=[jax.ShapeDtypeStruct((bsz, n_rows, nz), BF16), jax.ShapeDtypeStruct((bsz, n_rows, nx), BF16),
                       jax.ShapeDtypeStruct((bsz, n_rows, nd), F32)],
            scratch_shapes=[pltpu.VMEM((nx // LANES, rows + 2 * HALO, LANES), F32)],
            compiler_params=_cparams(("arbitrary", "arbitrary")),
            name=name,
        )(src, src, src, mod, g_pre.reshape(1, d), w_in, w_in, wd, conv_w, conv_b.reshape(1, nx), dt_bias)

    return (call(False, h_lat, INPROJ_TM, lambda b: b, "inproj1"),
            call(True, h_ctx, h_ctx.shape[1], lambda b: bsz, "inproj1_ctx"))


def _ssd_prologue(reverse, dr, dt_ref, alog_ref):
    ch = SSD_CHUNK
    dt = dt_ref[0]
    adt = dt * (-jnp.exp(alog_ref[dr:dr + 1, :]) * LOG2E)
    row = lax.broadcasted_iota(jnp.int32, (ch, ch), 0)
    col = lax.broadcasted_iota(jnp.int32, (ch, ch), 1)
    mask = (row <= col) if reverse else (row >= col)
    tri = mask.astype(BF16)
    hi = adt.astype(BF16)
    rest = adt - hi.astype(F32)
    mid = rest.astype(BF16)
    lo = (rest - mid.astype(F32)).astype(BF16)
    cs = (jnp.dot(tri, hi, preferred_element_type=F32) + jnp.dot(tri, mid, preferred_element_type=F32)
          + jnp.dot(tri, lo, preferred_element_type=F32))
    last = 0 if reverse else ch - 1
    cs_t = cs.T
    dt_t = dt.T
    w_t = jnp.exp2(cs_t[:, last:last + 1] - cs_t) * dt_t
    crow = cs_t - jnp.log2(dt_t)
    e_tot = jnp.exp2(cs[last:last + 1, :])
    return mask, cs, w_t, crow, e_tot


def _ssd_group(dr, g, xbc_ref, s_scr, inner):
    gn = SSD_GROUPS * SSD_STATE
    bg = xbc_ref[0, :, inner + g * SSD_STATE:inner + (g + 1) * SSD_STATE]
    cg = xbc_ref[0, :, inner + gn + g * SSD_STATE:inner + gn + (g + 1) * SSD_STATE]
    cb = lax.dot_general(cg, bg, (((1,), (1,)), ((), ())), preferred_element_type=F32).astype(BF16)
    bg_t = bg.astype(F32).T.astype(BF16)
    s_g = s_scr[dr, g]
    y_off = jnp.dot(cg, s_g.astype(BF16), preferred_element_type=F32)
    return cb, bg_t, s_g, y_off


def _ssd_pair(dr, g, pp, n_heads, pro, grp, xbc_ref, s_scr, y_ref):
    ch = SSD_CHUNK
    mask, cs, w_t, crow, e_tot = pro
    cb, bg_t, s_g, y_off = grp
    pairs_per_group = n_heads // SSD_GROUPS // 2
    p = g * pairs_per_group + pp
    h1 = dr * n_heads + 2 * p
    ls = slice(pp * LANES, (pp + 1) * LANES)
    left = lax.broadcasted_iota(jnp.int32, (ch, LANES), 1) < SSD_HEAD_DIM
    x2 = xbc_ref[0, :, p * LANES:(p + 1) * LANES]
    zero = jnp.zeros_like(x2)
    wx = jnp.concatenate([jnp.where(left, x2, zero), jnp.where(left, zero, x2)], axis=0)
    ms, bws, cols = [], [], []
    for h in (h1, h1 + 1):
        ccol = jnp.broadcast_to(cs[:, h:h + 1], (ch, ch))
        cols.append(ccol)
        ms.append(cb * jnp.exp2(jnp.where(mask, ccol - crow[h:h + 1, :], -jnp.inf)).astype(BF16))
        bws.append(bg_t * w_t[h:h + 1, :].astype(BF16))
    lhs = jnp.concatenate([jnp.concatenate(ms, axis=1), jnp.concatenate(bws, axis=1)], axis=0)
    r = jnp.dot(lhs, wx, preferred_element_type=F32)
    y = r[0:ch] + jnp.exp2(jnp.where(left, cols[0], cols[1])) * y_off[:, ls]
    y_ref[0, :, p * LANES:(p + 1) * LANES] = y.astype(BF16)
    dec = jnp.where(left[0:1, :], e_tot[:, h1:h1 + 1], e_tot[:, h1 + 1:h1 + 2])
    s_scr[dr, g, :, ls] = s_g[:, ls] * dec + r[ch:2 * ch]


def _ssd_kernel(n_ctx, xfl_ref, xfc_ref, xbl_ref, xbc_ref, dfl_ref, dfc_ref, dbl_ref, dbc_ref, alog_ref,
                yf_ref, yb_ref, s_scr):
    i = pl.program_id(1)

    @pl.when(i == 0)
    def _():
        s_scr[...] = jnp.zeros_like(s_scr)

    inner = yf_ref.shape[-1]
    n_heads = inner // SSD_HEAD_DIM

    def chunk_pair(xf_ref, df_ref, xb_ref, db_ref):
        dirs = ((xf_ref, df_ref, yf_ref, False), (xb_ref, db_ref, yb_ref, True))
        pros = [_ssd_prologue(rev, dr, d_ref, alog_ref) for dr, (_, d_ref, _, rev) in enumerate(dirs)]
        for g in range(SSD_GROUPS):
            grps = [_ssd_group(dr, g, x_ref, s_scr, inner) for dr, (x_ref, _, _, _) in enumerate(dirs)]
            for pp in range(n_heads // SSD_GROUPS // 2):
                for dr, (x_ref, _, y_ref, _) in enumerate(dirs):
                    _ssd_pair(dr, g, pp, n_heads, pros[dr], grps[dr], x_ref, s_scr, y_ref)

    pl.when(i < n_ctx)(lambda: chunk_pair(xfc_ref, dfc_ref, xbc_ref, dbc_ref))
    pl.when(i >= n_ctx)(lambda: chunk_pair(xfl_ref, dfl_ref, xbl_ref, dbl_ref))


def _ssd(xbc_lat, dt_lat, xbc_ctx, dt_ctx, a_log):
    bsz, s, nx = xbc_lat.shape
    nd = dt_lat.shape[-1]
    n_heads = a_log.shape[-1]
    inner = n_heads * SSD_HEAD_DIM
    a_log = jnp.stack([jnp.pad(a_log[dr], (dr * n_heads, nd - (dr + 1) * n_heads)) for dr in range(2)])
    ch = SSD_CHUNK
    n_lat = s // ch
    n_ctx = xbc_ctx.shape[1] // ch
    n_all = n_lat + n_ctx
    fwd = lambda b, i: (b, i, 0)
    bwd = lambda b, i: (b, _bwd_tile(i, n_ctx, n_all), 0)
    lead = lambda b, i: (b,)
    return pl.pallas_call(
        functools.partial(_ssd_kernel, n_ctx),
        grid=(bsz, n_all),
        in_specs=_walk_specs((1, ch, nx), n_ctx, n_lat, lead) + _walk_specs((1, ch, nd), n_ctx, n_lat, lead)
        + [pl.BlockSpec(a_log.shape, lambda b, i: (0, 0))],
        out_specs=[pl.BlockSpec((1, ch, inner), fwd), pl.BlockSpec((1, ch, inner), bwd)],
        out_shape=[jax.ShapeDtypeStruct((bsz, n_all * ch, inner), BF16)] * 2,
        scratch_shapes=[pltpu.VMEM((2, SSD_GROUPS, SSD_STATE, inner // SSD_GROUPS), F32)],
        compiler_params=_cparams(("arbitrary", "arbitrary")),
        name="ssd",
    )(xbc_lat, xbc_ctx, xbc_lat, xbc_ctx, dt_lat, dt_ctx, dt_lat, dt_ctx, a_log)


def _finish1_kernel(yf_ref, yb_ref, xs_ref, gz_ref, h_ref, mod_ref, dsk_ref, nw_ref, w_ref, g_ref, o_ref):
    d = o_ref.shape[-1]
    inner = yf_ref.shape[-1]
    gw = inner // SSD_GROUPS
    gt = mod_ref[0, :, 2 * d:3 * d]
    halves = [slice(r0, r0 + TM) for r0 in range(0, o_ref.shape[1], TM)]
    outs = [None] * len(halves)
    for g in range(SSD_GROUPS):
        sl = slice(g * gw, (g + 1) * gw)
        for k, rs in enumerate(halves):
            y = (yf_ref[0, rs, sl] + yb_ref[0, rs, sl]).astype(F32) + dsk_ref[:, sl] * xs_ref[0, rs, sl].astype(F32)
            y = y * gz_ref[0, rs, sl].astype(F32)
            t = jnp.dot((_rms(y) * nw_ref[:, sl]).astype(BF16), w_ref[sl, :], preferred_element_type=F32)
            outs[k] = t if outs[k] is None else outs[k] + t
    for k, rs in enumerate(halves):
        o_ref[0, rs, :] = h_ref[0, rs, :] + gt * (_rms(outs[k]) * g_ref[...])


def _finish1(yf, yb, xbc, gz, h, mod, d_skip, norm_w, w_out, g_post):
    bsz, t_all, inner = yf.shape
    d = h.shape[-1]
    rows = FINISH_TM
    n_lat = h.shape[1] // rows
    ctx_len = t_all - h.shape[1]
    own = lambda width: pl.BlockSpec((1, rows, width), lambda j, b: (b, j, 0))
    const2 = lambda j, b: (0, 0)
    return pl.pallas_call(
        _finish1_kernel,
        grid=(n_lat, bsz),
        in_specs=[_uni_lat_spec(rows, inner, ctx_len)] * 2 + [own(inner), own(inner)] + [
            pl.BlockSpec((1, rows, d), lambda j, b: (b, j, 0)),
            pl.BlockSpec((1, 1, 3 * d), lambda j, b: (b, 0, 0)),
            pl.BlockSpec((1, inner), const2),
            pl.BlockSpec((1, inner), const2),
            pl.BlockSpec(w_out.shape, const2),
            pl.BlockSpec((1, d), const2),
        ],
        out_specs=pl.BlockSpec((1, rows, d), lambda j, b: (b, j, 0)),
        out_shape=jax.ShapeDtypeStruct((bsz, n_lat * rows, d), F32),
        compiler_params=_cparams(("arbitrary", "arbitrary")),
        name="finish1",
    )(yf, yb, xbc, gz, h, mod, d_skip, norm_w, w_out, g_post.reshape(1, d))


def _rope_tables(n_tokens):
    rows = n_tokens // GRID_W
    row = jnp.repeat(jnp.arange(rows, dtype=F32), GRID_W)
    col = jnp.tile(jnp.arange(GRID_W, dtype=F32), rows)
    n_freq = DA_HEAD_DIM // 4
    inv = ROPE_BASE ** (-jnp.arange(n_freq, dtype=F32) / n_freq)
    ang = jnp.concatenate([row[:, None] * inv, col[:, None] * inv], axis=-1)
    cos, sin = jnp.cos(ang), jnp.sin(ang)
    cr, cc, sr, sc = cos[:, :n_freq], cos[:, n_freq:], sin[:, :n_freq], sin[:, n_freq:]
    zr = jnp.zeros_like(sr)
    c64 = jnp.concatenate([cr, cr, cc, cc], axis=1)
    s1 = jnp.concatenate([-sr, zr, -sc, zr], axis=1)
    s2 = jnp.concatenate([zr, sr, zr, sc], axis=1)
    reps = LANES // DA_HEAD_DIM
    return tuple(jnp.tile(t, (1, reps)) for t in (c64, s1, s2))


def _block_diag(w):
    n, c, d = w.shape
    eye = jnp.eye(n, dtype=w.dtype)
    return (eye[:, None, :, None] * w[:, :, None, :]).reshape(n * c, n * d)


def kernel(x, c, ctx, c_ctx, w_mod, b_mod, g_pre, g_post, e_w_in, e_w_out, lru_conv_w, lru_conv_b, lru_w_r, lru_b_r, lru_w_i, lru_b_i, lru_lambda, da_lambda, da_subln, o_w_in, o_w_out, ssd_conv_w, ssd_conv_b, ssd_a_log, ssd_dt_bias, ssd_d, ssd_norm):
    bsz, s, d = x.shape
    ctx_len = ctx.shape[1]
    assert bsz == SUBLANES and ctx_len == TM and s % KV_CHUNK == 0 and w_mod.shape[0] == 2
    assert e_w_in.shape[0] == 1 and o_w_in.shape[0] == 1

    n_rows = 2 * SUBLANES
    c_rows = jnp.concatenate([c, c_ctx[None, :], jnp.zeros((n_rows - bsz - 1, d), F32)], axis=0)
    mod = _modulation(c_rows, w_mod, b_mod)
    mod0 = mod[0].reshape(n_rows, 1, 3 * d)
    mod1 = mod[1].reshape(n_rows, 1, 3 * d)

    w5 = lru_conv_w.shape[-1]
    w_in0 = e_w_in[0].astype(BF16)
    (xc, gr, q, k, v, gd), (xc_c, gr_c, q_c, k_c, v_c, gd_c) = _inproj0(
        x, ctx, mod0, g_pre[0], w_in0, lru_conv_w[0], lru_conv_b[0], _rope_tables(s))

    wg = (0.5 * jnp.stack([jnp.concatenate([_block_diag(lru_w_r[0, dr]), _block_diag(lru_w_i[0, dr])], axis=1)
                           for dr in range(2)])).astype(BF16)
    bias = 0.5 * jnp.stack([lru_b_r[0, 0], lru_b_i[0, 0], lru_b_r[0, 1], lru_b_i[0, 1]])
    k4 = (0.5 * LRU_C) * jax.nn.softplus(-lru_lambda[0])
    hf, hb = _lru(xc, xc_c, wg, bias, k4)

    lambda_init = 0.8 - 0.6 * math.exp(-0.3 * 0)
    on, on_c = _attention((q, k, v), (q_c, k_c, v_c), da_lambda[0], da_subln[0], lambda_init)
    h1_lat, h1_ctx = _outproj0(hf, hb, (gr, on, gd), (gr_c, on_c, gd_c), x, ctx, mod0, g_post[0],
                               e_w_out[0].astype(BF16))

    n_heads = ssd_a_log.shape[-1]
    inner = n_heads * SSD_HEAD_DIM
    nx = ssd_conv_w.shape[-1]
    w1 = o_w_in[0].astype(BF16)
    nd = 2 * n_heads
    wd = jnp.pad(w1[:, inner + nx:], ((0, 0), (0, LANES - nd)))
    dt_bias = jnp.pad(ssd_dt_bias[0].reshape(1, nd), ((0, 0), (0, LANES - nd)))
    (gz, xbc, dt), (_, xbc_c, dt_c) = _inproj1(h1_lat, h1_ctx, mod1, g_pre[1], w1, inner, wd, ssd_conv_w[0],
                                               ssd_conv_b[0], dt_bias)
    yf, yb = _ssd(xbc, dt, xbc_c, dt_c, ssd_a_log[0])
    d_skip = jnp.repeat(ssd_d[0], SSD_HEAD_DIM).reshape(1, inner)
    return _finish1(yf, yb, xbc, gz, h1_lat, mod1, d_skip, ssd_norm[0].reshape(1, inner),
                    o_w_out[0].astype(BF16), g_post[1])
```

```python
import functools
import math

import jax
import jax.numpy as jnp
from jax import lax
from jax.experimental import pallas as pl
from jax.experimental.pallas import tpu as pltpu

F32 = jnp.float32
BF16 = jnp.bfloat16

EPS = 1e-6
GRID_W = 64
ROPE_BASE = 10000.0
LRU_C = 8.0
LRU_BLOCKS = 8
DA_HEADS = 4
DA_HEAD_DIM = 64
DA_V_DIM = 128
ROPE_QUARTER = DA_HEAD_DIM // 4
SSD_HEAD_DIM = 64
SSD_STATE = 128
SSD_GROUPS = 4
SSD_CHUNK = 128

MOD_COLS = 1024
TM = 256
HALO = 8
LRU_TT = 128
LRU_PITCH = 9
KV_CHUNK = 512
ATTN_TQ = 1024
INPROJ_TM = 512
INPROJ_COLS = 512
OUTPROJ_K = 256
FINISH_TM = 512
OUTPROJ_TM = 512
LANES = 128
SUBLANES = 8
VMEM_LIMIT = 56 * 1024 * 1024

HIGHEST = lax.Precision.HIGHEST
LOG2E = math.log2(math.e)
Q_SCALE = DA_HEAD_DIM ** -0.5 * LOG2E


def _sigmoid(x):
    return 0.5 * (1.0 + jnp.tanh(0.5 * x))


def _silu(x):
    h = 0.5 * x
    return h * (1.0 + jnp.tanh(h))


def _rms(x):
    return x * lax.rsqrt(jnp.mean(x * x, axis=-1, keepdims=True) + EPS)


def _cparams(sem):
    return pltpu.CompilerParams(dimension_semantics=sem, vmem_limit_bytes=VMEM_LIMIT)


def _mod_kernel(c_ref, w_ref, b_ref, o_ref):
    s = _silu(c_ref[...])
    o_ref[0] = jnp.dot(s, w_ref[0], preferred_element_type=F32, precision=HIGHEST) + b_ref[0]


def _modulation(c_rows, w_mod, b_mod):
    depth, d, n3 = w_mod.shape
    rows = c_rows.shape[0]
    nt = MOD_COLS
    return pl.pallas_call(
        _mod_kernel,
        grid=(depth, n3 // nt),
        in_specs=[
            pl.BlockSpec((rows, d), lambda l, n: (0, 0)),
            pl.BlockSpec((1, d, nt), lambda l, n: (l, 0, n)),
            pl.BlockSpec((1, 1, nt), lambda l, n: (l, 0, n)),
        ],
        out_specs=pl.BlockSpec((1, rows, nt), lambda l, n: (l, 0, n)),
        out_shape=jax.ShapeDtypeStruct((depth, rows, n3), F32),
        compiler_params=_cparams(("arbitrary", "arbitrary")),
        name="modulation",
    )(c_rows, w_mod, b_mod.reshape(depth, 1, n3))


def _normed_rows(h_ext, mod_ref, g_ref, d):
    sh = mod_ref[0, :, 0:d]
    sc = mod_ref[0, :, d:2 * d]
    u = (_rms(h_ext) * g_ref[...]) * (1.0 + sc) + sh
    return u.astype(BF16)


def _store_ext(ext_ref, slab0, x_ext, left_ok, right_ok):
    n = x_ext.shape[1] // LANES
    rows = x_ext.shape[0] - 2 * HALO
    zero = jnp.zeros((HALO, LANES), F32)
    for c in range(n):
        blk = x_ext[:, c * LANES:(c + 1) * LANES]
        ext_ref[slab0 + c, 0:HALO, :] = jnp.where(left_ok, blk[0:HALO], zero)
        ext_ref[slab0 + c, HALO:HALO + rows, :] = blk[HALO:HALO + rows]
        ext_ref[slab0 + c, HALO + rows:, :] = jnp.where(right_ok, blk[HALO + rows:], zero)


def _conv4(ext_ref, c, w_ref, b_ref):
    rows = ext_ref.shape[1] - 2 * HALO
    ls = slice(c * LANES, (c + 1) * LANES)
    acc = b_ref[:, ls] + w_ref[0:1, ls] * ext_ref[c, pl.ds(HALO - 2, rows), :]
    acc = acc + w_ref[1:2, ls] * ext_ref[c, pl.ds(HALO - 1, rows), :]
    acc = acc + w_ref[2:3, ls] * ext_ref[c, pl.ds(HALO, rows), :]
    acc = acc + w_ref[3:4, ls] * ext_ref[c, pl.ds(HALO + 1, rows), :]
    return acc


def _segment_edges(is_ctx, n_steps):
    if is_ctx:
        return jnp.bool_(False), jnp.bool_(False)
    j = pl.program_id(0)
    return j >= 1, j <= n_steps - 2


def _uni_lat_spec(rows, width, ctx_len):
    return pl.BlockSpec((pl.Element(1), pl.Element(rows), pl.Element(width)),
                        lambda j, b: (b, pl.multiple_of(ctx_len + j * rows, TM), 0))


def _inproj0_kernel(is_ctx, n_steps, hm_ref, hl_ref, hr_ref, mod_ref, g_ref, w_ref, cw_ref, cb_ref,
                    rc_ref, rs1_ref, rs2_ref, xc_ref, gr_ref, q_ref, k_ref, v_ref, gd_ref, ext_ref):
    d = hm_ref.shape[-1]
    rows = hm_ref.shape[1]
    w5 = xc_ref.shape[-1]
    h_ext = jnp.concatenate([hl_ref[0], hm_ref[0], hr_ref[0]], axis=0)
    u_ext = _normed_rows(h_ext, mod_ref, g_ref, d)
    u = u_ext[HALO:HALO + rows]

    x_ext = jnp.dot(u_ext, w_ref[:, 0:w5], preferred_element_type=F32)
    _store_ext(ext_ref, 0, x_ext, *_segment_edges(is_ctx, n_steps))
    for cs in range(w5 // LANES):
        xc_ref[0, :, cs * LANES:(cs + 1) * LANES] = _conv4(ext_ref, cs, cw_ref, cb_ref)

    gr_ref[0] = jnp.dot(u, w_ref[:, w5:2 * w5], preferred_element_type=F32).astype(BF16)
    reps = w5 // LANES
    for idx, o_ref, post in ((2, q_ref, Q_SCALE), (3, k_ref, None)):
        t = jnp.dot(u, w_ref[:, idx * w5:(idx + 1) * w5], preferred_element_type=F32)
        if not is_ctx:
            c = jnp.concatenate([rc_ref[...]] * reps, axis=1)
            s1 = jnp.concatenate([rs1_ref[...]] * reps, axis=1)
            s2 = jnp.concatenate([rs2_ref[...]] * reps, axis=1)
            t = t * c + pltpu.roll(t, w5 - ROPE_QUARTER, 1) * s1 + pltpu.roll(t, ROPE_QUARTER, 1) * s2
        o_ref[0] = (t if post is None else t * post).astype(BF16)
    v_ref[0] = jnp.dot(u, w_ref[:, 4 * w5:5 * w5], preferred_element_type=F32).astype(BF16)
    gd_ref[0] = jnp.dot(u, w_ref[:, 5 * w5:6 * w5], preferred_element_type=F32).astype(BF16)


def _halo_specs(rows, d, n_rows):
    hb = rows // HALO
    return [pl.BlockSpec((1, rows, d), lambda j, b: (b, j, 0)),
            pl.BlockSpec((1, HALO, d), lambda j, b: (b, jnp.maximum(j * hb - 1, 0), 0)),
            pl.BlockSpec((1, HALO, d), lambda j, b: (b, jnp.minimum((j + 1) * hb, n_rows // HALO - 1), 0))]


def _inproj0(x, ctx, mod, g_pre, w_in, conv_w, conv_b, rope):
    bsz, s, d = x.shape
    w5 = conv_w.shape[-1]
    const2 = lambda j, b: (0, 0)
    consts = [pl.BlockSpec((1, d), const2), pl.BlockSpec(w_in.shape, const2), pl.BlockSpec(conv_w.shape, const2),
              pl.BlockSpec((1, w5), const2)]

    def call(is_ctx, src, rows, mod_row, name):
        n_rows = src.shape[1]
        n_steps = n_rows // rows
        rope_spec = pl.BlockSpec((rows, LANES), lambda j, b: (j, 0))
        return pl.pallas_call(
            functools.partial(_inproj0_kernel, is_ctx, n_steps),
            grid=(n_steps, bsz),
            in_specs=_halo_specs(rows, d, n_rows)
            + [pl.BlockSpec((1, 1, 3 * d), lambda j, b: (mod_row(b), 0, 0))] + consts + [rope_spec] * 3,
            out_specs=[pl.BlockSpec((1, rows, w5), lambda j, b: (b, j, 0))] * 6,
            out_shape=[jax.ShapeDtypeStruct((bsz, n_rows, w5), F32)]
            + [jax.ShapeDtypeStruct((bsz, n_rows, w5), BF16)] * 5,
            scratch_shapes=[pltpu.VMEM((w5 // LANES, rows + 2 * HALO, LANES), F32)],
            compiler_params=_cparams(("arbitrary", "arbitrary")),
            name=name,
        )(src, src, src, mod, g_pre.reshape(1, d), w_in, conv_w, conv_b.reshape(1, w5), *rope)

    return (call(False, x, INPROJ_TM, lambda b: b, "inproj0"),
            call(True, ctx, ctx.shape[1], lambda b: bsz, "inproj0_ctx"))


def _sqrt_pos(y):
    return jnp.where(y > 0.0, y * lax.rsqrt(y), 0.0)


def _lru_kernel(bsz, n_ctx, xfl_ref, xfc_ref, xbl_ref, xbc_ref, wg_ref, bias_ref, k4_ref, of_ref, ob_ref,
                g_scr, a_scr, b_scr, hs_scr, h_scr):
    i = pl.program_id(0)
    w5 = xfl_ref.shape[-1]
    nslab = w5 // LANES
    tt = LRU_TT

    @pl.when(i == 0)
    def _():
        h_scr[...] = jnp.zeros_like(h_scr)

    def coefficients(xf_ref, xb_ref):
        for dr, x_ref in enumerate((xf_ref, xb_ref)):
            k4 = k4_ref[dr:dr + 1, :]
            g_scr[...] = jnp.dot(x_ref[...].reshape(bsz * tt, w5).astype(BF16), wg_ref[dr],
                                 preferred_element_type=F32)
            for b in range(bsz):
                xc = x_ref[b]
                tr = jnp.tanh(g_scr[b * tt:(b + 1) * tt, 0:w5] + bias_ref[2 * dr:2 * dr + 1, :])
                ti = jnp.tanh(g_scr[b * tt:(b + 1) * tt, w5:2 * w5] + bias_ref[2 * dr + 1:2 * dr + 2, :])
                nla = k4 * tr + k4
                a = jnp.exp2(nla * (-LOG2E))
                bc = _sqrt_pos(jnp.tanh(nla) * (a * a + 1.0)) * ((0.5 * xc) * (1.0 + ti))
                for c in range(nslab):
                    a_scr[dr, c, pl.ds(b, tt, stride=LRU_PITCH), :] = a[:, c * LANES:(c + 1) * LANES]
                    b_scr[dr, c, pl.ds(b, tt, stride=LRU_PITCH), :] = bc[:, c * LANES:(c + 1) * LANES]

    def step(t, carry):
        hf, hb = carry
        rf = t * LRU_PITCH
        rb = (tt - 1 - t) * LRU_PITCH
        nf, nb = [], []
        for c in range(nslab):
            h = a_scr[0, c, pl.ds(rf, bsz), :] * hf[c] + b_scr[0, c, pl.ds(rf, bsz), :]
            hs_scr[0, c, pl.ds(rf, bsz), :] = h
            nf.append(h)
            h = a_scr[1, c, pl.ds(rb, bsz), :] * hb[c] + b_scr[1, c, pl.ds(rb, bsz), :]
            hs_scr[1, c, pl.ds(rb, bsz), :] = h
            nb.append(h)
        return tuple(nf), tuple(nb)

    pl.when(i < n_ctx)(lambda: coefficients(xfc_ref, xbc_ref))
    pl.when(i >= n_ctx)(lambda: coefficients(xfl_ref, xbl_ref))

    h0f = tuple(h_scr[0, c] for c in range(nslab))
    h0b = tuple(h_scr[1, c] for c in range(nslab))
    hf, hb = lax.fori_loop(0, tt, step, (h0f, h0b), unroll=True)
    for c in range(nslab):
        h_scr[0, c] = hf[c]
        h_scr[1, c] = hb[c]

    for dr, o_ref in enumerate((of_ref, ob_ref)):
        for b in range(bsz):
            for c in range(nslab):
                o_ref[b, :, c * LANES:(c + 1) * LANES] = hs_scr[dr, c, pl.ds(b, tt, stride=LRU_PITCH), :].astype(BF16)


def _bwd_tile(i, n_ctx, n_all):
    return jnp.where(i < n_ctx, n_ctx - 1 - i, n_all - 1 - (i - n_ctx))


def _walk_specs(block, n_ctx, n_lat, lead):
    def spec(tile):
        return pl.BlockSpec(block, lambda *ids: lead(*ids) + (tile(ids[-1]), 0))
    lat_pos = lambda i: jnp.clip(i - n_ctx, 0, n_lat - 1)
    ctx_pos = lambda i: jnp.minimum(i, n_ctx - 1)
    return [spec(lat_pos), spec(ctx_pos), spec(lambda i: n_lat - 1 - lat_pos(i)), spec(lambda i: n_ctx - 1 - ctx_pos(i))]


def _lru(xc_lat, xc_ctx, wg, bias, k4):
    bsz, s, w5 = xc_lat.shape
    tt = LRU_TT
    n_lat = s // tt
    n_ctx = xc_ctx.shape[1] // tt
    n_all = n_lat + n_ctx
    nslab = w5 // LANES
    fwd_spec = pl.BlockSpec((bsz, tt, w5), lambda i: (0, i, 0))
    bwd_spec = pl.BlockSpec((bsz, tt, w5), lambda i: (0, _bwd_tile(i, n_ctx, n_all), 0))
    const = lambda shape: pl.BlockSpec(shape, lambda i: (0,) * len(shape))
    assert bsz <= LRU_PITCH
    coef = pltpu.VMEM((2, nslab, tt * LRU_PITCH, LANES), F32)
    return pl.pallas_call(
        functools.partial(_lru_kernel, bsz, n_ctx),
        grid=(n_all,),
        in_specs=_walk_specs((bsz, tt, w5), n_ctx, n_lat, lambda i: (0,))
        + [const(wg.shape), const(bias.shape), const(k4.shape)],
        out_specs=[fwd_spec, bwd_spec],
        out_shape=[jax.ShapeDtypeStruct((bsz, n_all * tt, w5), BF16)] * 2,
        scratch_shapes=[pltpu.VMEM((bsz * tt, 2 * w5), F32), coef, coef, coef,
                        pltpu.VMEM((2, nslab, bsz, LANES), F32)],
        compiler_params=_cparams(("arbitrary",)),
        name="rglru",
    )(xc_lat, xc_ctx, xc_lat, xc_ctx, wg, bias, k4)


def _attn_kernel(lambda_init, n_streams, n_lat_chunks, q_ref, kc_ref, vc_ref, kl_ref, vl_ref, lam_ref, sub_ref,
                 o_ref, m_scr, acc_scr):
    key_chunks = [(kc_ref, vc_ref, 0, kc_ref.shape[1])]
    key_chunks += [(kl_ref, vl_ref, c * KV_CHUNK, KV_CHUNK) for c in range(n_lat_chunks)]
    qqs = []
    for st in range(n_streams):
        q = q_ref[0, st * TM:(st + 1) * TM, :]
        lane = lax.broadcasted_iota(jnp.int32, q.shape, 1)
        zero = jnp.zeros_like(q)
        qqs.append(jnp.concatenate([jnp.where(lane < DA_HEAD_DIM, q, zero),
                                    jnp.where(lane >= DA_HEAD_DIM, q, zero)], axis=0))
    m_scr[...] = jnp.full_like(m_scr, -jnp.inf)
    acc_scr[...] = jnp.zeros_like(acc_scr)

    for k_ref, v_ref, start, size in key_chunks:
        kc = k_ref[0, start:start + size, :]
        va = jnp.concatenate([v_ref[0, start:start + size, :], jnp.ones((size, LANES), BF16)], axis=1)
        for st in range(n_streams):
            s = lax.dot_general(qqs[st], kc, (((1,), (1,)), ((), ())), preferred_element_type=F32)
            m_prev = m_scr[st]
            m_next = jnp.maximum(m_prev, jnp.max(s, axis=1, keepdims=True))
            p = jnp.exp2(s - jnp.concatenate([m_next] * (size // LANES), axis=1))
            alpha = jnp.exp2(m_prev - m_next)
            acc_scr[st] = (acc_scr[st] * jnp.concatenate([alpha, alpha], axis=1)
                           + jnp.dot(p.astype(BF16), va, preferred_element_type=F32))
            m_scr[st] = m_next

    lm = lam_ref[...]
    lam = (jnp.exp(jnp.sum(lm[0:1] * lm[1:2], axis=1, keepdims=True))
           - jnp.exp(jnp.sum(lm[2:3] * lm[3:4], axis=1, keepdims=True)) + lambda_init)
    for st in range(n_streams):
        o = (acc_scr[st, 0:TM, 0:LANES] / acc_scr[st, 0:TM, LANES:2 * LANES]
             - lam * (acc_scr[st, TM:2 * TM, 0:LANES] / acc_scr[st, TM:2 * TM, LANES:2 * LANES]))
        o_ref[0, st * TM:(st + 1) * TM, :] = ((_rms(o) * sub_ref[...]) * (1.0 - lambda_init)).astype(BF16)


def _attention(qkv_lat, qkv_ctx, da_lambda, da_subln, lambda_init):
    q_lat, k_lat, v_lat = qkv_lat
    q_ctx, k_ctx, v_ctx = qkv_ctx
    bsz, n_lat, w = q_lat.shape
    ctx_len = q_ctx.shape[1]
    hd = DA_V_DIM
    small = [pl.BlockSpec(da_lambda.shape, lambda b, h, j: (0, 0)), pl.BlockSpec((1, hd), lambda b, h, j: (0, 0))]
    sub = da_subln.reshape(1, hd)
    ctx_spec = pl.BlockSpec((1, ctx_len, hd), lambda b, h, j: (b, 0, h))

    def call(q, q_rows, kl, vl, n_lat_chunks, name):
        n_streams = q_rows // TM
        q_spec = pl.BlockSpec((1, q_rows, hd), lambda b, h, j: (b, j, h))
        lat_spec = pl.BlockSpec((1, kl.shape[1], hd), lambda b, h, j: (b, 0, h))
        return pl.pallas_call(
            functools.partial(_attn_kernel, lambda_init, n_streams, n_lat_chunks),
            grid=(bsz, DA_HEADS, q.shape[1] // q_rows),
            in_specs=[q_spec, ctx_spec, ctx_spec, lat_spec, lat_spec] + small,
            out_specs=q_spec,
            out_shape=jax.ShapeDtypeStruct(q.shape, BF16),
            scratch_shapes=[pltpu.VMEM((n_streams, 2 * TM, LANES), F32),
                            pltpu.VMEM((n_streams, 2 * TM, 2 * LANES), F32)],
            compiler_params=_cparams(("arbitrary", "arbitrary", "arbitrary")),
            name=name,
        )(q, k_ctx, v_ctx, kl, vl, da_lambda, sub)

    on_lat = call(q_lat, ATTN_TQ, k_lat, v_lat, n_lat // KV_CHUNK, "diffattn")
    on_ctx = call(q_ctx, ctx_len, k_ctx, v_ctx, 0, "diffattn_ctx")
    return on_lat, on_ctx


def _outproj0_kernel(hf_ref, hb_ref, gr_ref, on_ref, gd_ref, res_ref, mod_ref, g_ref, w_ref, o_ref):
    d = o_ref.shape[-1]
    w5 = hf_ref.shape[-1]
    kc = OUTPROJ_K
    gt = mod_ref[0, :, 2 * d:3 * d]
    halves = [slice(r0, r0 + TM) for r0 in range(0, o_ref.shape[1], TM)]
    ys = [None] * len(halves)
    for c in range(w5 // kc):
        sl = slice(c * kc, (c + 1) * kc)
        for k, rs in enumerate(halves):
            r = hf_ref[0, rs, sl].astype(F32) + hb_ref[0, rs, sl].astype(F32)
            m1 = (r * _silu(gr_ref[0, rs, sl].astype(F32))).astype(BF16)
            m2 = (on_ref[0, rs, sl].astype(F32) * _silu(gd_ref[0, rs, sl].astype(F32))).astype(BF16)
            t = jnp.dot(m1, w_ref[c * kc:(c + 1) * kc, :], preferred_element_type=F32)
            t = t + jnp.dot(m2, w_ref[w5 + c * kc:w5 + (c + 1) * kc, :], preferred_element_type=F32)
            ys[k] = t if ys[k] is None else ys[k] + t
    for k, rs in enumerate(halves):
        o_ref[0, rs, :] = res_ref[0, rs, :] + gt * (_rms(ys[k]) * g_ref[...])


def _outproj0(hf, hb, lat, ctxs, x, ctx, mod, g_post, w_out):
    bsz, _, w5 = hf.shape
    d = x.shape[-1]
    ctx_len = ctx.shape[1]
    const2 = lambda j, b: (0, 0)
    consts = [pl.BlockSpec((1, d), const2), pl.BlockSpec(w_out.shape, const2)]

    def call(rows, uni, parts, res, mod_row, name):
        own = lambda width: pl.BlockSpec((1, rows, width), lambda j, b: (b, j, 0))
        gr, on, gd = parts
        return pl.pallas_call(
            _outproj0_kernel,
            grid=(res.shape[1] // rows, bsz),
            in_specs=[uni, uni, own(w5), own(w5), own(w5), own(d),
                      pl.BlockSpec((1, 1, 3 * d), lambda j, b: (mod_row(b), 0, 0))] + consts,
            out_specs=own(d),
            out_shape=jax.ShapeDtypeStruct(res.shape, F32),
            compiler_params=_cparams(("arbitrary", "arbitrary")),
            name=name,
        )(hf, hb, gr, on, gd, res, mod, g_post.reshape(1, d), w_out)

    h_lat = call(OUTPROJ_TM, _uni_lat_spec(OUTPROJ_TM, w5, ctx_len), lat, x, lambda b: b, "outproj0")
    ctx_uni = pl.BlockSpec((1, ctx_len, w5), lambda j, b: (b, 0, 0))
    h_ctx = call(ctx_len, ctx_uni, ctxs, ctx, lambda b: bsz, "outproj0_ctx")
    return h_lat, h_ctx


def _inproj1_kernel(is_ctx, n_steps, hm_ref, hl_ref, hr_ref, mod_ref, g_ref, wz_ref, wx_ref, wd_ref, cw_ref, cb_ref,
                    db_ref, z_ref, xbc_ref, dt_ref, ext_ref):
    d = hm_ref.shape[-1]
    rows = hm_ref.shape[1]
    h_ext = jnp.concatenate([hl_ref[0], hm_ref[0], hr_ref[0]], axis=0)
    u_ext = _normed_rows(h_ext, mod_ref, g_ref, d)
    u = u_ext[HALO:HALO + rows]
    left_ok, right_ok = _segment_edges(is_ctx, n_steps)
    nz, nx = wz_ref.shape[1], wx_ref.shape[1]
    cw = INPROJ_COLS
    spc = cw // LANES
    n_x, n_z = nx // cw, nz // cw
    for i in range(max(n_x, n_z)):
        if i < n_x:
            x_ext = jnp.dot(u_ext, wx_ref[:, i * cw:(i + 1) * cw], preferred_element_type=F32)
            _store_ext(ext_ref, i * spc, x_ext, left_ok, right_ok)
            for cs in range(i * spc, (i + 1) * spc):
                xbc_ref[0, :, cs * LANES:(cs + 1) * LANES] = _silu(_conv4(ext_ref, cs, cw_ref, cb_ref)).astype(BF16)
        if i < n_z:
            z_ref[0, :, i * cw:(i + 1) * cw] = _silu(jnp.dot(u, wz_ref[:, i * cw:(i + 1) * cw],
                                                             preferred_element_type=F32)).astype(BF16)
    dt_ref[0] = jax.nn.softplus(jnp.dot(u, wd_ref[...], preferred_element_type=F32) + db_ref[...])


def _inproj1(h_lat, h_ctx, mod, g_pre, w_in, nz, wd, conv_w, conv_b, dt_bias):
    bsz, _, d = h_lat.shape
    nx, nd = conv_w.shape[1], wd.shape[1]
    const2 = lambda j, b: (0, 0)
    consts = [pl.BlockSpec((1, d), const2), pl.BlockSpec((d, nz), const2),
              pl.BlockSpec((pl.Element(d), pl.Element(nx)), lambda j, b: (0, nz)),
              pl.BlockSpec(wd.shape, const2), pl.BlockSpec(conv_w.shape, const2), pl.BlockSpec((1, nx), const2),
              pl.BlockSpec((1, nd), const2)]

    def call(is_ctx, src, rows, mod_row, name):
        n_rows = src.shape[1]
        n_steps = n_rows // rows
        return pl.pallas_call(
            functools.partial(_inproj1_kernel, is_ctx, n_steps),
            grid=(n_steps, bsz),
            in_specs=_halo_specs(rows, d, n_rows)
            + [pl.BlockSpec((1, 1, 3 * d), lambda j, b: (mod_row(b), 0, 0))] + consts,
            out_specs=[pl.BlockSpec((1, rows, w), lambda j, b: (b, j, 0)) for w in (nz, nx, nd)],
            out_shape=[jax.ShapeDtypeStruct((bsz, n_rows, nz), BF16), jax.ShapeDtypeStruct((bsz, n_rows, nx), BF16),
                       jax.ShapeDtypeStruct((bsz, n_rows, nd), F32)],
            scratch_shapes=[pltpu.VMEM((nx // LANES, rows + 2 * HALO, LANES), F32)],
            compiler_params=_cparams(("arbitrary", "arbitrary")),
            name=name,
        )(src, src, src, mod, g_pre.reshape(1, d), w_in, w_in, wd, conv_w, conv_b.reshape(1, nx), dt_bias)

    return (call(False, h_lat, INPROJ_TM, lambda b: b, "inproj1"),
            call(True, h_ctx, h_ctx.shape[1], lambda b: bsz, "inproj1_ctx"))


def _ssd_prologue(reverse, dr, dt_ref, alog_ref):
    ch = SSD_CHUNK
    dt = dt_ref[0]
    adt = dt * (-jnp.exp(alog_ref[dr:dr + 1, :]) * LOG2E)
    row = lax.broadcasted_iota(jnp.int32, (ch, ch), 0)
    col = lax.broadcasted_iota(jnp.int32, (ch, ch), 1)
    mask = (row <= col) if reverse else (row >= col)
    tri = mask.astype(BF16)
    hi = adt.astype(BF16)
    rest = adt - hi.astype(F32)
    mid = rest.astype(BF16)
    lo = (rest - mid.astype(F32)).astype(BF16)
    cs = (jnp.dot(tri, hi, preferred_element_type=F32) + jnp.dot(tri, mid, preferred_element_type=F32)
          + jnp.dot(tri, lo, preferred_element_type=F32))
    last = 0 if reverse else ch - 1
    cs_t = cs.T
    dt_t = dt.T
    w_t = jnp.exp2(cs_t[:, last:last + 1] - cs_t) * dt_t
    crow = cs_t - jnp.log2(dt_t)
    e_tot = jnp.exp2(cs[last:last + 1, :])
    return mask, cs, w_t, crow, e_tot


def _ssd_group(dr, g, xbc_ref, s_scr, inner):
    gn = SSD_GROUPS * SSD_STATE
    bg = xbc_ref[0, :, inner + g * SSD_STATE:inner + (g + 1) * SSD_STATE]
    cg = xbc_ref[0, :, inner + gn + g * SSD_STATE:inner + gn + (g + 1) * SSD_STATE]
    cb = lax.dot_general(cg, bg, (((1,), (1,)), ((), ())), preferred_element_type=F32).astype(BF16)
    bg_t = bg.astype(F32).T.astype(BF16)
    s_g = s_scr[dr, g]
    y_off = jnp.dot(cg, s_g.astype(BF16), preferred_element_type=F32)
    return cb, bg_t, s_g, y_off


def _ssd_pair(dr, g, pp, n_heads, pro, grp, xbc_ref, s_scr, y_ref):
    ch = SSD_CHUNK
    mask, cs, w_t, crow, e_tot = pro
    cb, bg_t, s_g, y_off = grp
    pairs_per_group = n_heads // SSD_GROUPS // 2
    p = g * pairs_per_group + pp
    h1 = dr * n_heads + 2 * p
    ls = slice(pp * LANES, (pp + 1) * LANES)
    left = lax.broadcasted_iota(jnp.int32, (ch, LANES), 1) < SSD_HEAD_DIM
    x2 = xbc_ref[0, :, p * LANES:(p + 1) * LANES]
    zero = jnp.zeros_like(x2)
    wx = jnp.concatenate([jnp.where(left, x2, zero), jnp.where(left, zero, x2)], axis=0)
    ms, bws, cols = [], [], []
    for h in (h1, h1 + 1):
        ccol = jnp.broadcast_to(cs[:, h:h + 1], (ch, ch))
        cols.append(ccol)
        ms.append(cb * jnp.exp2(jnp.where(mask, ccol - crow[h:h + 1, :], -jnp.inf)).astype(BF16))
        bws.append(bg_t * w_t[h:h + 1, :].astype(BF16))
    lhs = jnp.concatenate([jnp.concatenate(ms, axis=1), jnp.concatenate(bws, axis=1)], axis=0)
    r = jnp.dot(lhs, wx, preferred_element_type=F32)
    y = r[0:ch] + jnp.exp2(jnp.where(left, cols[0], cols[1])) * y_off[:, ls]
    y_ref[0, :, p * LANES:(p + 1) * LANES] = y.astype(BF16)
    dec = jnp.where(left[0:1, :], e_tot[:, h1:h1 + 1], e_tot[:, h1 + 1:h1 + 2])
    s_scr[dr, g, :, ls] = s_g[:, ls] * dec + r[ch:2 * ch]


def _ssd_kernel(n_ctx, xfl_ref, xfc_ref, xbl_ref, xbc_ref, dfl_ref, dfc_ref, dbl_ref, dbc_ref, alog_ref,
                yf_ref, yb_ref, s_scr):
    i = pl.program_id(1)

    @pl.when(i == 0)
    def _():
        s_scr[...] = jnp.zeros_like(s_scr)

    inner = yf_ref.shape[-1]
    n_heads = inner // SSD_HEAD_DIM

    def chunk_pair(xf_ref, df_ref, xb_ref, db_ref):
        dirs = ((xf_ref, df_ref, yf_ref, False), (xb_ref, db_ref, yb_ref, True))
        pros = [_ssd_prologue(rev, dr, d_ref, alog_ref) for dr, (_, d_ref, _, rev) in enumerate(dirs)]
        for g in range(SSD_GROUPS):
            grps = [_ssd_group(dr, g, x_ref, s_scr, inner) for dr, (x_ref, _, _, _) in enumerate(dirs)]
            for pp in range(n_heads // SSD_GROUPS // 2):
                for dr, (x_ref, _, y_ref, _) in enumerate(dirs):
                    _ssd_pair(dr, g, pp, n_heads, pros[dr], grps[dr], x_ref, s_scr, y_ref)

    pl.when(i < n_ctx)(lambda: chunk_pair(xfc_ref, dfc_ref, xbc_ref, dbc_ref))
    pl.when(i >= n_ctx)(lambda: chunk_pair(xfl_ref, dfl_ref, xbl_ref, dbl_ref))


def _ssd(xbc_lat, dt_lat, xbc_ctx, dt_ctx, a_log):
    bsz, s, nx = xbc_lat.shape
    nd = dt_lat.shape[-1]
    n_heads = a_log.shape[-1]
    inner = n_heads * SSD_HEAD_DIM
    a_log = jnp.stack([jnp.pad(a_log[dr], (dr * n_heads, nd - (dr + 1) * n_heads)) for dr in range(2)])
    ch = SSD_CHUNK
    n_lat = s // ch
    n_ctx = xbc_ctx.shape[1] // ch
    n_all = n_lat + n_ctx
    fwd = lambda b, i: (b, i, 0)
    bwd = lambda b, i: (b, _bwd_tile(i, n_ctx, n_all), 0)
    lead = lambda b, i: (b,)
    return pl.pallas_call(
        functools.partial(_ssd_kernel, n_ctx),
        grid=(bsz, n_all),
        in_specs=_walk_specs((1, ch, nx), n_ctx, n_lat, lead) + _walk_specs((1, ch, nd), n_ctx, n_lat, lead)
        + [pl.BlockSpec(a_log.shape, lambda b, i: (0, 0))],
        out_specs=[pl.BlockSpec((1, ch, inner), fwd), pl.BlockSpec((1, ch, inner), bwd)],
        out_shape=[jax.ShapeDtypeStruct((bsz, n_all * ch, inner), BF16)] * 2,
        scratch_shapes=[pltpu.VMEM((2, SSD_GROUPS, SSD_STATE, inner // SSD_GROUPS), F32)],
        compiler_params=_cparams(("arbitrary", "arbitrary")),
        name="ssd",
    )(xbc_lat, xbc_ctx, xbc_lat, xbc_ctx, dt_lat, dt_ctx, dt_lat, dt_ctx, a_log)


def _finish1_kernel(yf_ref, yb_ref, xs_ref, gz_ref, h_ref, mod_ref, dsk_ref, nw_ref, w_ref, g_ref, o_ref):
    d = o_ref.shape[-1]
    inner = yf_ref.shape[-1]
    gw = inner // SSD_GROUPS
    gt = mod_ref[0, :, 2 * d:3 * d]
    halves = [slice(r0, r0 + TM) for r0 in range(0, o_ref.shape[1], TM)]
    outs = [None] * len(halves)
    for g in range(SSD_GROUPS):
        sl = slice(g * gw, (g + 1) * gw)
        for k, rs in enumerate(halves):
            y = (yf_ref[0, rs, sl] + yb_ref[0, rs, sl]).astype(F32) + dsk_ref[:, sl] * xs_ref[0, rs, sl].astype(F32)
            y = y * gz_ref[0, rs, sl].astype(F32)
            t = jnp.dot((_rms(y) * nw_ref[:, sl]).astype(BF16), w_ref[sl, :], preferred_element_type=F32)
            outs[k] = t if outs[k] is None else outs[k] + t
    for k, rs in enumerate(halves):
        o_ref[0, rs, :] = h_ref[0, rs, :] + gt * (_rms(outs[k]) * g_ref[...])


def _finish1(yf, yb, xbc, gz, h, mod, d_skip, norm_w, w_out, g_post):
    bsz, t_all, inner = yf.shape
    d = h.shape[-1]
    rows = FINISH_TM
    n_lat = h.shape[1] // rows
    ctx_len = t_all - h.shape[1]
    own = lambda width: pl.BlockSpec((1, rows, width), lambda j, b: (b, j, 0))
    const2 = lambda j, b: (0, 0)
    return pl.pallas_call(
        _finish1_kernel,
        grid=(n_lat, bsz),
        in_specs=[_uni_lat_spec(rows, inner, ctx_len)] * 2 + [own(inner), own(inner)] + [
            pl.BlockSpec((1, rows, d), lambda j, b: (b, j, 0)),
            pl.BlockSpec((1, 1, 3 * d), lambda j, b: (b, 0, 0)),
            pl.BlockSpec((1, inner), const2),
            pl.BlockSpec((1, inner), const2),
            pl.BlockSpec(w_out.shape, const2),
            pl.BlockSpec((1, d), const2),
        ],
        out_specs=pl.BlockSpec((1, rows, d), lambda j, b: (b, j, 0)),
        out_shape=jax.ShapeDtypeStruct((bsz, n_lat * rows, d), F32),
        compiler_params=_cparams(("arbitrary", "arbitrary")),
        name="finish1",
    )(yf, yb, xbc, gz, h, mod, d_skip, norm_w, w_out, g_post.reshape(1, d))


def _rope_tables(n_tokens):
    rows = n_tokens // GRID_W
    row = jnp.repeat(jnp.arange(rows, dtype=F32), GRID_W)
    col = jnp.tile(jnp.arange(GRID_W, dtype=F32), rows)
    n_freq = DA_HEAD_DIM // 4
    inv = ROPE_BASE ** (-jnp.arange(n_freq, dtype=F32) / n_freq)
    ang = jnp.concatenate([row[:, None] * inv, col[:, None] * inv], axis=-1)
    cos, sin = jnp.cos(ang), jnp.sin(ang)
    cr, cc, sr, sc = cos[:, :n_freq], cos[:, n_freq:], sin[:, :n_freq], sin[:, n_freq:]
    zr = jnp.zeros_like(sr)
    c64 = jnp.concatenate([cr, cr, cc, cc], axis=1)
    s1 = jnp.concatenate([-sr, zr, -sc, zr], axis=1)
    s2 = jnp.concatenate([zr, sr, zr, sc], axis=1)
    reps = LANES // DA_HEAD_DIM
    return tuple(jnp.tile(t, (1, reps)) for t in (c64, s1, s2))


def _block_diag(w):
    n, c, d = w.shape
    eye = jnp.eye(n, dtype=w.dtype)
    return (eye[:, None, :, None] * w[:, :, None, :]).reshape(n * c, n * d)


def kernel(x, c, ctx, c_ctx, w_mod, b_mod, g_pre, g_post, e_w_in, e_w_out, lru_conv_w, lru_conv_b, lru_w_r, lru_b_r, lru_w_i, lru_b_i, lru_lambda, da_lambda, da_subln, o_w_in, o_w_out, ssd_conv_w, ssd_conv_b, ssd_a_log, ssd_dt_bias, ssd_d, ssd_norm):
    bsz, s, d = x.shape
    ctx_len = ctx.shape[1]
    assert bsz == SUBLANES and ctx_len == TM and s % KV_CHUNK == 0 and w_mod.shape[0] == 2
    assert e_w_in.shape[0] == 1 and o_w_in.shape[0] == 1

    n_rows = 2 * SUBLANES
    c_rows = jnp.concatenate([c, c_ctx[None, :], jnp.zeros((n_rows - bsz - 1, d), F32)], axis=0)
    mod = _modulation(c_rows, w_mod, b_mod)
    mod0 = mod[0].reshape(n_rows, 1, 3 * d)
    mod1 = mod[1].reshape(n_rows, 1, 3 * d)

    w5 = lru_conv_w.shape[-1]
    w_in0 = e_w_in[0].astype(BF16)
    (xc, gr, q, k, v, gd), (xc_c, gr_c, q_c, k_c, v_c, gd_c) = _inproj0(
        x, ctx, mod0, g_pre[0], w_in0, lru_conv_w[0], lru_conv_b[0], _rope_tables(s))

    wg = (0.5 * jnp.stack([jnp.concatenate([_block_diag(lru_w_r[0, dr]), _block_diag(lru_w_i[0, dr])], axis=1)
                           for dr in range(2)])).astype(BF16)
    bias = 0.5 * jnp.stack([lru_b_r[0, 0], lru_b_i[0, 0], lru_b_r[0, 1], lru_b_i[0, 1]])
    k4 = (0.5 * LRU_C) * jax.nn.softplus(-lru_lambda[0])
    hf, hb = _lru(xc, xc_c, wg, bias, k4)

    lambda_init = 0.8 - 0.6 * math.exp(-0.3 * 0)
    on, on_c = _attention((q, k, v), (q_c, k_c, v_c), da_lambda[0], da_subln[0], lambda_init)
    h1_lat, h1_ctx = _outproj0(hf, hb, (gr, on, gd), (gr_c, on_c, gd_c), x, ctx, mod0, g_post[0],
                               e_w_out[0].astype(BF16))

    n_heads = ssd_a_log.shape[-1]
    inner = n_heads * SSD_HEAD_DIM
    nx = ssd_conv_w.shape[-1]
    w1 = o_w_in[0].astype(BF16)
    nd = 2 * n_heads
    wd = jnp.pad(w1[:, inner + nx:], ((0, 0), (0, LANES - nd)))
    dt_bias = jnp.pad(ssd_dt_bias[0].reshape(1, nd), ((0, 0), (0, LANES - nd)))
    (gz, xbc, dt), (_, xbc_c, dt_c) = _inproj1(h1_lat, h1_ctx, mod1, g_pre[1], w1, inner, wd, ssd_conv_w[0],
                                               ssd_conv_b[0], dt_bias)
    yf, yb = _ssd(xbc, dt, xbc_c, dt_c, ssd_a_log[0])
    d_skip = jnp.repeat(ssd_d[0], SSD_HEAD_DIM).reshape(1, inner)
    return _finish1(yf, yb, xbc, gz, h1_lat, mod1, d_skip, ssd_norm[0].reshape(1, inner),
                    o_w_out[0].astype(BF16), g_post[1])
```

```python
import functools
import math

import jax
import jax.numpy as jnp
from jax import lax
from jax.experimental import pallas as pl
from jax.experimental.pallas import tpu as pltpu

F32 = jnp.float32
BF16 = jnp.bfloat16

EPS = 1e-6
GRID_W = 64
ROPE_BASE = 10000.0
LRU_C = 8.0
LRU_BLOCKS = 8
DA_HEADS = 4
DA_HEAD_DIM = 64
DA_V_DIM = 128
ROPE_QUARTER = DA_HEAD_DIM // 4
SSD_HEAD_DIM = 64
SSD_STATE = 128
SSD_GROUPS = 4
SSD_CHUNK = 128

MOD_COLS = 1024
TM = 256
HALO = 8
LRU_TT = 128
LRU_PITCH = 9
KV_CHUNK = 512
ATTN_TQ = 1024
INPROJ_TM = 512
INPROJ_COLS = 512
OUTPROJ_K = 256
FINISH_TM = 512
OUTPROJ_TM = 512
LANES = 128
SUBLANES = 8
VMEM_LIMIT = 56 * 1024 * 1024

HIGHEST = lax.Precision.HIGHEST
LOG2E = math.log2(math.e)
Q_SCALE = DA_HEAD_DIM ** -0.5 * LOG2E


def _sigmoid(x):
    return 0.5 * (1.0 + jnp.tanh(0.5 * x))


def _silu(x):
    h = 0.5 * x
    return h * (1.0 + jnp.tanh(h))


def _rms(x):
    return x * lax.rsqrt(jnp.mean(x * x, axis=-1, keepdims=True) + EPS)


def _cparams(sem):
    return pltpu.CompilerParams(dimension_semantics=sem, vmem_limit_bytes=VMEM_LIMIT)


def _mod_kernel(c_ref, w_ref, b_ref, o_ref):
    s = _silu(c_ref[...])
    o_ref[0] = jnp.dot(s, w_ref[0], preferred_element_type=F32, precision=HIGHEST) + b_ref[0]


def _modulation(c_rows, w_mod, b_mod):
    depth, d, n3 = w_mod.shape
    rows = c_rows.shape[0]
    nt = MOD_COLS
    return pl.pallas_call(
        _mod_kernel,
        grid=(depth, n3 // nt),
        in_specs=[
            pl.BlockSpec((rows, d), lambda l, n: (0, 0)),
            pl.BlockSpec((1, d, nt), lambda l, n: (l, 0, n)),
            pl.BlockSpec((1, 1, nt), lambda l, n: (l, 0, n)),
        ],
        out_specs=pl.BlockSpec((1, rows, nt), lambda l, n: (l, 0, n)),
        out_shape=jax.ShapeDtypeStruct((depth, rows, n3), F32),
        compiler_params=_cparams(("arbitrary", "arbitrary")),
        name="modulation",
    )(c_rows, w_mod, b_mod.reshape(depth, 1, n3))


def _normed_rows(h_ext, mod_ref, g_ref, d):
    sh = mod_ref[0, :, 0:d]
    sc = mod_ref[0, :, d:2 * d]
    u = (_rms(h_ext) * g_ref[...]) * (1.0 + sc) + sh
    return u.astype(BF16)


def _store_ext(ext_ref, slab0, x_ext, left_ok, right_ok):
    n = x_ext.shape[1] // LANES
    rows = x_ext.shape[0] - 2 * HALO
    zero = jnp.zeros((HALO, LANES), F32)
    for c in range(n):
        blk = x_ext[:, c * LANES:(c + 1) * LANES]
        ext_ref[slab0 + c, 0:HALO, :] = jnp.where(left_ok, blk[0:HALO], zero)
        ext_ref[slab0 + c, HALO:HALO + rows, :] = blk[HALO:HALO + rows]
        ext_ref[slab0 + c, HALO + rows:, :] = jnp.where(right_ok, blk[HALO + rows:], zero)


def _conv4(ext_ref, c, w_ref, b_ref):
    rows = ext_ref.shape[1] - 2 * HALO
    ls = slice(c * LANES, (c + 1) * LANES)
    acc = b_ref[:, ls] + w_ref[0:1, ls] * ext_ref[c, pl.ds(HALO - 2, rows), :]
    acc = acc + w_ref[1:2, ls] * ext_ref[c, pl.ds(HALO - 1, rows), :]
    acc = acc + w_ref[2:3, ls] * ext_ref[c, pl.ds(HALO, rows), :]
    acc = acc + w_ref[3:4, ls] * ext_ref[c, pl.ds(HALO + 1, rows), :]
    return acc


def _segment_edges(is_ctx, n_steps):
    if is_ctx:
        return jnp.bool_(False), jnp.bool_(False)
    j = pl.program_id(0)
    return j >= 1, j <= n_steps - 2


def _uni_lat_spec(rows, width, ctx_len):
    return pl.BlockSpec((pl.Element(1), pl.Element(rows), pl.Element(width)),
                        lambda j, b: (b, pl.multiple_of(ctx_len + j * rows, TM), 0))


def _inproj0_kernel(is_ctx, n_steps, hm_ref, hl_ref, hr_ref, mod_ref, g_ref, w_ref, cw_ref, cb_ref,
                    rc_ref, rs1_ref, rs2_ref, xc_ref, gr_ref, q_ref, k_ref, v_ref, gd_ref, ext_ref):
    d = hm_ref.shape[-1]
    rows = hm_ref.shape[1]
    w5 = xc_ref.shape[-1]
    h_ext = jnp.concatenate([hl_ref[0], hm_ref[0], hr_ref[0]], axis=0)
    u_ext = _normed_rows(h_ext, mod_ref, g_ref, d)
    u = u_ext[HALO:HALO + rows]

    x_ext = jnp.dot(u_ext, w_ref[:, 0:w5], preferred_element_type=F32)
    _store_ext(ext_ref, 0, x_ext, *_segment_edges(is_ctx, n_steps))
    for cs in range(w5 // LANES):
        xc_ref[0, :, cs * LANES:(cs + 1) * LANES] = _conv4(ext_ref, cs, cw_ref, cb_ref)

    gr_ref[0] = jnp.dot(u, w_ref[:, w5:2 * w5], preferred_element_type=F32).astype(BF16)
    reps = w5 // LANES
    for idx, o_ref, post in ((2, q_ref, Q_SCALE), (3, k_ref, None)):
        t = jnp.dot(u, w_ref[:, idx * w5:(idx + 1) * w5], preferred_element_type=F32)
        if not is_ctx:
            c = jnp.concatenate([rc_ref[...]] * reps, axis=1)
            s1 = jnp.concatenate([rs1_ref[...]] * reps, axis=1)
            s2 = jnp.concatenate([rs2_ref[...]] * reps, axis=1)
            t = t * c + pltpu.roll(t, w5 - ROPE_QUARTER, 1) * s1 + pltpu.roll(t, ROPE_QUARTER, 1) * s2
        o_ref[0] = (t if post is None else t * post).astype(BF16)
    v_ref[0] = jnp.dot(u, w_ref[:, 4 * w5:5 * w5], preferred_element_type=F32).astype(BF16)
    gd_ref[0] = jnp.dot(u, w_ref[:, 5 * w5:6 * w5], preferred_element_type=F32).astype(BF16)


def _halo_specs(rows, d, n_rows):
    hb = rows // HALO
    return [pl.BlockSpec((1, rows, d), lambda j, b: (b, j, 0)),
            pl.BlockSpec((1, HALO, d), lambda j, b: (b, jnp.maximum(j * hb - 1, 0), 0)),
            pl.BlockSpec((1, HALO, d), lambda j, b: (b, jnp.minimum((j + 1) * hb, n_rows // HALO - 1), 0))]


def _inproj0(x, ctx, mod, g_pre, w_in, conv_w, conv_b, rope):
    bsz, s, d = x.shape
    w5 = conv_w.shape[-1]
    const2 = lambda j, b: (0, 0)
    consts = [pl.BlockSpec((1, d), const2), pl.BlockSpec(w_in.shape, const2, pipeline_mode=pl.Buffered(1)),
              pl.BlockSpec(conv_w.shape, const2),
              pl.BlockSpec((1, w5), const2)]

    def call(is_ctx, src, rows, mod_row, name):
        n_rows = src.shape[1]
        n_steps = n_rows // rows
        rope_spec = pl.BlockSpec((rows, LANES), lambda j, b: (j, 0))
        return pl.pallas_call(
            functools.partial(_inproj0_kernel, is_ctx, n_steps),
            grid=(n_steps, bsz),
            in_specs=_halo_specs(rows, d, n_rows)
            + [pl.BlockSpec((1, 1, 3 * d), lambda j, b: (mod_row(b), 0, 0))] + consts + [rope_spec] * 3,
            out_specs=[pl.BlockSpec((1, rows, w5), lambda j, b: (b, j, 0))] * 6,
            out_shape=[jax.ShapeDtypeStruct((bsz, n_rows, w5), F32)]
            + [jax.ShapeDtypeStruct((bsz, n_rows, w5), BF16)] * 5,
            scratch_shapes=[pltpu.VMEM((w5 // LANES, rows + 2 * HALO, LANES), F32)],
            compiler_params=_cparams(("arbitrary", "arbitrary")),
            name=name,
        )(src, src, src, mod, g_pre.reshape(1, d), w_in, conv_w, conv_b.reshape(1, w5), *rope)

    return (call(False, x, INPROJ_TM, lambda b: b, "inproj0"),
            call(True, ctx, ctx.shape[1], lambda b: bsz, "inproj0_ctx"))


def _sqrt_pos(y):
    return jnp.where(y > 0.0, y * lax.rsqrt(y), 0.0)


def _lru_kernel(bsz, n_ctx, xfl_ref, xfc_ref, xbl_ref, xbc_ref, wg_ref, bias_ref, k4_ref, of_ref, ob_ref,
                g_scr, a_scr, b_scr, hs_scr, h_scr):
    i = pl.program_id(0)
    w5 = xfl_ref.shape[-1]
    nslab = w5 // LANES
    tt = LRU_TT

    @pl.when(i == 0)
    def _():
        h_scr[...] = jnp.zeros_like(h_scr)

    def coefficients(xf_ref, xb_ref):
        for dr, x_ref in enumerate((xf_ref, xb_ref)):
            k4 = k4_ref[dr:dr + 1, :]
            g_scr[...] = jnp.dot(x_ref[...].reshape(bsz * tt, w5).astype(BF16), wg_ref[dr],
                                 preferred_element_type=F32)
            for b in range(bsz):
                xc = x_ref[b]
                tr = jnp.tanh(g_scr[b * tt:(b + 1) * tt, 0:w5] + bias_ref[2 * dr:2 * dr + 1, :])
                ti = jnp.tanh(g_scr[b * tt:(b + 1) * tt, w5:2 * w5] + bias_ref[2 * dr + 1:2 * dr + 2, :])
                nla = k4 * tr + k4
                a = jnp.exp2(nla * (-LOG2E))
                bc = _sqrt_pos(jnp.tanh(nla) * (a * a + 1.0)) * ((0.5 * xc) * (1.0 + ti))
                for c in range(nslab):
                    a_scr[dr, c, pl.ds(b, tt, stride=LRU_PITCH), :] = a[:, c * LANES:(c + 1) * LANES]
                    b_scr[dr, c, pl.ds(b, tt, stride=LRU_PITCH), :] = bc[:, c * LANES:(c + 1) * LANES]

    def step(t, carry):
        hf, hb = carry
        rf = t * LRU_PITCH
        rb = (tt - 1 - t) * LRU_PITCH
        nf, nb = [], []
        for c in range(nslab):
            h = a_scr[0, c, pl.ds(rf, bsz), :] * hf[c] + b_scr[0, c, pl.ds(rf, bsz), :]
            hs_scr[0, c, pl.ds(rf, bsz), :] = h
            nf.append(h)
            h = a_scr[1, c, pl.ds(rb, bsz), :] * hb[c] + b_scr[1, c, pl.ds(rb, bsz), :]
            hs_scr[1, c, pl.ds(rb, bsz), :] = h
            nb.append(h)
        return tuple(nf), tuple(nb)

    pl.when(i < n_ctx)(lambda: coefficients(xfc_ref, xbc_ref))
    pl.when(i >= n_ctx)(lambda: coefficients(xfl_ref, xbl_ref))

    h0f = tuple(h_scr[0, c] for c in range(nslab))
    h0b = tuple(h_scr[1, c] for c in range(nslab))
    hf, hb = lax.fori_loop(0, tt, step, (h0f, h0b), unroll=True)
    for c in range(nslab):
        h_scr[0, c] = hf[c]
        h_scr[1, c] = hb[c]

    for dr, o_ref in enumerate((of_ref, ob_ref)):
        for b in range(bsz):
            for c in range(nslab):
                o_ref[b, :, c * LANES:(c + 1) * LANES] = hs_scr[dr, c, pl.ds(b, tt, stride=LRU_PITCH), :].astype(BF16)


def _bwd_tile(i, n_ctx, n_all):
    return jnp.where(i < n_ctx, n_ctx - 1 - i, n_all - 1 - (i - n_ctx))


def _walk_specs(block, n_ctx, n_lat, lead):
    def spec(tile):
        return pl.BlockSpec(block, lambda *ids: lead(*ids) + (tile(ids[-1]), 0))
    lat_pos = lambda i: jnp.clip(i - n_ctx, 0, n_lat - 1)
    ctx_pos = lambda i: jnp.minimum(i, n_ctx - 1)
    return [spec(lat_pos), spec(ctx_pos), spec(lambda i: n_lat - 1 - lat_pos(i)), spec(lambda i: n_ctx - 1 - ctx_pos(i))]


def _lru(xc_lat, xc_ctx, wg, bias, k4):
    bsz, s, w5 = xc_lat.shape
    tt = LRU_TT
    n_lat = s // tt
    n_ctx = xc_ctx.shape[1] // tt
    n_all = n_lat + n_ctx
    nslab = w5 // LANES
    fwd_spec = pl.BlockSpec((bsz, tt, w5), lambda i: (0, i, 0))
    bwd_spec = pl.BlockSpec((bsz, tt, w5), lambda i: (0, _bwd_tile(i, n_ctx, n_all), 0))
    const = lambda shape: pl.BlockSpec(shape, lambda i: (0,) * len(shape))
    assert bsz <= LRU_PITCH
    coef = pltpu.VMEM((2, nslab, tt * LRU_PITCH, LANES), F32)
    return pl.pallas_call(
        functools.partial(_lru_kernel, bsz, n_ctx),
        grid=(n_all,),
        in_specs=_walk_specs((bsz, tt, w5), n_ctx, n_lat, lambda i: (0,))
        + [const(wg.shape), const(bias.shape), const(k4.shape)],
        out_specs=[fwd_spec, bwd_spec],
        out_shape=[jax.ShapeDtypeStruct((bsz, n_all * tt, w5), BF16)] * 2,
        scratch_shapes=[pltpu.VMEM((bsz * tt, 2 * w5), F32), coef, coef, coef,
                        pltpu.VMEM((2, nslab, bsz, LANES), F32)],
        compiler_params=_cparams(("arbitrary",)),
        name="rglru",
    )(xc_lat, xc_ctx, xc_lat, xc_ctx, wg, bias, k4)


def _attn_kernel(lambda_init, n_streams, n_lat_chunks, q_ref, kc_ref, vc_ref, kl_ref, vl_ref, lam_ref, sub_ref,
                 o_ref, m_scr, acc_scr):
    key_chunks = [(kc_ref, vc_ref, 0, kc_ref.shape[1])]
    key_chunks += [(kl_ref, vl_ref, c * KV_CHUNK, KV_CHUNK) for c in range(n_lat_chunks)]
    qqs = []
    for st in range(n_streams):
        q = q_ref[0, st * TM:(st + 1) * TM, :]
        lane = lax.broadcasted_iota(jnp.int32, q.shape, 1)
        zero = jnp.zeros_like(q)
        qqs.append(jnp.concatenate([jnp.where(lane < DA_HEAD_DIM, q, zero),
                                    jnp.where(lane >= DA_HEAD_DIM, q, zero)], axis=0))
    m_scr[...] = jnp.full_like(m_scr, -jnp.inf)
    acc_scr[...] = jnp.zeros_like(acc_scr)

    for k_ref, v_ref, start, size in key_chunks:
        kc = k_ref[0, start:start + size, :]
        va = jnp.concatenate([v_ref[0, start:start + size, :], jnp.ones((size, LANES), BF16)], axis=1)
        for st in range(n_streams):
            s = lax.dot_general(qqs[st], kc, (((1,), (1,)), ((), ())), preferred_element_type=F32)
            m_prev = m_scr[st]
            m_next = jnp.maximum(m_prev, jnp.max(s, axis=1, keepdims=True))
            p = jnp.exp2(s - jnp.concatenate([m_next] * (size // LANES), axis=1))
            alpha = jnp.exp2(m_prev - m_next)
            acc_scr[st] = (acc_scr[st] * jnp.concatenate([alpha, alpha], axis=1)
                           + jnp.dot(p.astype(BF16), va, preferred_element_type=F32))
            m_scr[st] = m_next

    lm = lam_ref[...]
    lam = (jnp.exp(jnp.sum(lm[0:1] * lm[1:2], axis=1, keepdims=True))
           - jnp.exp(jnp.sum(lm[2:3] * lm[3:4], axis=1, keepdims=True)) + lambda_init)
    for st in range(n_streams):
        o = (acc_scr[st, 0:TM, 0:LANES] / acc_scr[st, 0:TM, LANES:2 * LANES]
             - lam * (acc_scr[st, TM:2 * TM, 0:LANES] / acc_scr[st, TM:2 * TM, LANES:2 * LANES]))
        o_ref[0, st * TM:(st + 1) * TM, :] = ((_rms(o) * sub_ref[...]) * (1.0 - lambda_init)).astype(BF16)


def _attention(qkv_lat, qkv_ctx, da_lambda, da_subln, lambda_init):
    q_lat, k_lat, v_lat = qkv_lat
    q_ctx, k_ctx, v_ctx = qkv_ctx
    bsz, n_lat, w = q_lat.shape
    ctx_len = q_ctx.shape[1]
    hd = DA_V_DIM
    small = [pl.BlockSpec(da_lambda.shape, lambda b, h, j: (0, 0)), pl.BlockSpec((1, hd), lambda b, h, j: (0, 0))]
    sub = da_subln.reshape(1, hd)
    ctx_spec = pl.BlockSpec((1, ctx_len, hd), lambda b, h, j: (b, 0, h))

    def call(q, q_rows, kl, vl, n_lat_chunks, name):
        n_streams = q_rows // TM
        q_spec = pl.BlockSpec((1, q_rows, hd), lambda b, h, j: (b, j, h))
        lat_spec = pl.BlockSpec((1, kl.shape[1], hd), lambda b, h, j: (b, 0, h))
        return pl.pallas_call(
            functools.partial(_attn_kernel, lambda_init, n_streams, n_lat_chunks),
            grid=(bsz, DA_HEADS, q.shape[1] // q_rows),
            in_specs=[q_spec, ctx_spec, ctx_spec, lat_spec, lat_spec] + small,
            out_specs=q_spec,
            out_shape=jax.ShapeDtypeStruct(q.shape, BF16),
            scratch_shapes=[pltpu.VMEM((n_streams, 2 * TM, LANES), F32),
                            pltpu.VMEM((n_streams, 2 * TM, 2 * LANES), F32)],
            compiler_params=_cparams(("arbitrary", "arbitrary", "arbitrary")),
            name=name,
        )(q, k_ctx, v_ctx, kl, vl, da_lambda, sub)

    on_lat = call(q_lat, ATTN_TQ, k_lat, v_lat, n_lat // KV_CHUNK, "diffattn")
    on_ctx = call(q_ctx, ctx_len, k_ctx, v_ctx, 0, "diffattn_ctx")
    return on_lat, on_ctx


def _outproj0_kernel(hf_ref, hb_ref, gr_ref, on_ref, gd_ref, res_ref, mod_ref, g_ref, w_ref, o_ref):
    d = o_ref.shape[-1]
    w5 = hf_ref.shape[-1]
    kc = OUTPROJ_K
    gt = mod_ref[0, :, 2 * d:3 * d]
    halves = [slice(r0, r0 + TM) for r0 in range(0, o_ref.shape[1], TM)]
    ys = [None] * len(halves)
    for c in range(w5 // kc):
        sl = slice(c * kc, (c + 1) * kc)
        for k, rs in enumerate(halves):
            r = hf_ref[0, rs, sl].astype(F32) + hb_ref[0, rs, sl].astype(F32)
            m1 = (r * _silu(gr_ref[0, rs, sl].astype(F32))).astype(BF16)
            m2 = (on_ref[0, rs, sl].astype(F32) * _silu(gd_ref[0, rs, sl].astype(F32))).astype(BF16)
            t = jnp.dot(m1, w_ref[c * kc:(c + 1) * kc, :], preferred_element_type=F32)
            t = t + jnp.dot(m2, w_ref[w5 + c * kc:w5 + (c + 1) * kc, :], preferred_element_type=F32)
            ys[k] = t if ys[k] is None else ys[k] + t
    for k, rs in enumerate(halves):
        o_ref[0, rs, :] = res_ref[0, rs, :] + gt * (_rms(ys[k]) * g_ref[...])


def _outproj0(hf, hb, lat, ctxs, x, ctx, mod, g_post, w_out):
    bsz, _, w5 = hf.shape
    d = x.shape[-1]
    ctx_len = ctx.shape[1]
    const2 = lambda j, b: (0, 0)
    consts = [pl.BlockSpec((1, d), const2), pl.BlockSpec(w_out.shape, const2)]

    def call(rows, uni, parts, res, mod_row, name):
        own = lambda width: pl.BlockSpec((1, rows, width), lambda j, b: (b, j, 0))
        gr, on, gd = parts
        return pl.pallas_call(
            _outproj0_kernel,
            grid=(res.shape[1] // rows, bsz),
            in_specs=[uni, uni, own(w5), own(w5), own(w5), own(d),
                      pl.BlockSpec((1, 1, 3 * d), lambda j, b: (mod_row(b), 0, 0))] + consts,
            out_specs=own(d),
            out_shape=jax.ShapeDtypeStruct(res.shape, F32),
            compiler_params=_cparams(("arbitrary", "arbitrary")),
            name=name,
        )(hf, hb, gr, on, gd, res, mod, g_post.reshape(1, d), w_out)

    h_lat = call(OUTPROJ_TM, _uni_lat_spec(OUTPROJ_TM, w5, ctx_len), lat, x, lambda b: b, "outproj0")
    ctx_uni = pl.BlockSpec((1, ctx_len, w5), lambda j, b: (b, 0, 0))
    h_ctx = call(ctx_len, ctx_uni, ctxs, ctx, lambda b: bsz, "outproj0_ctx")
    return h_lat, h_ctx


def _inproj1_kernel(is_ctx, n_steps, hm_ref, hl_ref, hr_ref, mod_ref, g_ref, wz_ref, wx_ref, wd_ref, cw_ref, cb_ref,
                    db_ref, z_ref, xbc_ref, dt_ref, ext_ref):
    d = hm_ref.shape[-1]
    rows = hm_ref.shape[1]
    h_ext = jnp.concatenate([hl_ref[0], hm_ref[0], hr_ref[0]], axis=0)
    u_ext = _normed_rows(h_ext, mod_ref, g_ref, d)
    u = u_ext[HALO:HALO + rows]
    left_ok, right_ok = _segment_edges(is_ctx, n_steps)
    nz, nx = wz_ref.shape[1], wx_ref.shape[1]
    cw = INPROJ_COLS
    spc = cw // LANES
    n_x, n_z = nx // cw, nz // cw
    for i in range(max(n_x, n_z)):
        if i < n_x:
            x_ext = jnp.dot(u_ext, wx_ref[:, i * cw:(i + 1) * cw], preferred_element_type=F32)
            _store_ext(ext_ref, i * spc, x_ext, left_ok, right_ok)
            for cs in range(i * spc, (i + 1) * spc):
                xbc_ref[0, :, cs * LANES:(cs + 1) * LANES] = _silu(_conv4(ext_ref, cs, cw_ref, cb_ref)).astype(BF16)
        if i < n_z:
            z_ref[0, :, i * cw:(i + 1) * cw] = _silu(jnp.dot(u, wz_ref[:, i * cw:(i + 1) * cw],
                                                             preferred_element_type=F32)).astype(BF16)
    dt_ref[0] = jax.nn.softplus(jnp.dot(u, wd_ref[...], preferred_element_type=F32) + db_ref[...])


def _inproj1(h_lat, h_ctx, mod, g_pre, w_in, nz, wd, conv_w, conv_b, dt_bias):
    bsz, _, d = h_lat.shape
    nx, nd = conv_w.shape[1], wd.shape[1]
    const2 = lambda j, b: (0, 0)
    consts = [pl.BlockSpec((1, d), const2), pl.BlockSpec((d, nz), const2, pipeline_mode=pl.Buffered(1)),
              pl.BlockSpec((pl.Element(d), pl.Element(nx)), lambda j, b: (0, nz)),
              pl.BlockSpec(wd.shape, const2), pl.BlockSpec(conv_w.shape, const2), pl.BlockSpec((1, nx), const2),
              pl.BlockSpec((1, nd), const2)]

    def call(is_ctx, src, rows, mod_row, name):
        n_rows = src.shape[1]
        n_steps = n_rows // rows
        return pl.pallas_call(
            functools.partial(_inproj1_kernel, is_ctx, n_steps),
            grid=(n_steps, bsz),
            in_specs=_halo_specs(rows, d, n_rows)
            + [pl.BlockSpec((1, 1, 3 * d), lambda j, b: (mod_row(b), 0, 0))] + consts,
            out_specs=[pl.BlockSpec((1, rows, w), lambda j, b: (b, j, 0)) for w in (nz, nx, nd)],
            out_shape=[jax.ShapeDtypeStruct((bsz, n_rows, nz), BF16), jax.ShapeDtypeStruct((bsz, n_rows, nx), BF16),
                       jax.ShapeDtypeStruct((bsz, n_rows, nd), F32)],
            scratch_shapes=[pltpu.VMEM((nx // LANES, rows + 2 * HALO, LANES), F32)],
            compiler_params=_cparams(("arbitrary", "arbitrary")),
            name=name,
        )(src, src, src, mod, g_pre.reshape(1, d), w_in, w_in, wd, conv_w, conv_b.reshape(1, nx), dt_bias)

    return (call(False, h_lat, INPROJ_TM, lambda b: b, "inproj1"),
            call(True, h_ctx, h_ctx.shape[1], lambda b: bsz, "inproj1_ctx"))


def _ssd_prologue(reverse, dr, dt_ref, alog_ref):
    ch = SSD_CHUNK
    dt = dt_ref[0]
    adt = dt * (-jnp.exp(alog_ref[dr:dr + 1, :]) * LOG2E)
    row = lax.broadcasted_iota(jnp.int32, (ch, ch), 0)
    col = lax.broadcasted_iota(jnp.int32, (ch, ch), 1)
    mask = (row <= col) if reverse else (row >= col)
    tri = mask.astype(BF16)
    hi = adt.astype(BF16)
    rest = adt - hi.astype(F32)
    mid = rest.astype(BF16)
    lo = (rest - mid.astype(F32)).astype(BF16)
    cs = (jnp.dot(tri, hi, preferred_element_type=F32) + jnp.dot(tri, mid, preferred_element_type=F32)
          + jnp.dot(tri, lo, preferred_element_type=F32))
    last = 0 if reverse else ch - 1
    cs_t = cs.T
    dt_t = dt.T
    w_t = jnp.exp2(cs_t[:, last:last + 1] - cs_t) * dt_t
    crow = cs_t - jnp.log2(dt_t)
    e_tot = jnp.exp2(cs[last:last + 1, :])
    return mask, cs, w_t, crow, e_tot


def _ssd_group(dr, g, xbc_ref, s_scr, inner):
    gn = SSD_GROUPS * SSD_STATE
    bg = xbc_ref[0, :, inner + g * SSD_STATE:inner + (g + 1) * SSD_STATE]
    cg = xbc_ref[0, :, inner + gn + g * SSD_STATE:inner + gn + (g + 1) * SSD_STATE]
    cb = lax.dot_general(cg, bg, (((1,), (1,)), ((), ())), preferred_element_type=F32).astype(BF16)
    bg_t = bg.astype(F32).T.astype(BF16)
    s_g = s_scr[dr, g]
    y_off = jnp.dot(cg, s_g.astype(BF16), preferred_element_type=F32)
    return cb, bg_t, s_g, y_off


def _ssd_pair(dr, g, pp, n_heads, pro, grp, xbc_ref, s_scr, y_ref):
    ch = SSD_CHUNK
    mask, cs, w_t, crow, e_tot = pro
    cb, bg_t, s_g, y_off = grp
    pairs_per_group = n_heads // SSD_GROUPS // 2
    p = g * pairs_per_group + pp
    h1 = dr * n_heads + 2 * p
    ls = slice(pp * LANES, (pp + 1) * LANES)
    left = lax.broadcasted_iota(jnp.int32, (ch, LANES), 1) < SSD_HEAD_DIM
    x2 = xbc_ref[0, :, p * LANES:(p + 1) * LANES]
    zero = jnp.zeros_like(x2)
    wx = jnp.concatenate([jnp.where(left, x2, zero), jnp.where(left, zero, x2)], axis=0)
    ms, bws, cols = [], [], []
    for h in (h1, h1 + 1):
        ccol = jnp.broadcast_to(cs[:, h:h + 1], (ch, ch))
        cols.append(ccol)
        ms.append(cb * jnp.exp2(jnp.where(mask, ccol - crow[h:h + 1, :], -jnp.inf)).astype(BF16))
        bws.append(bg_t * w_t[h:h + 1, :].astype(BF16))
    lhs = jnp.concatenate([jnp.concatenate(ms, axis=1), jnp.concatenate(bws, axis=1)], axis=0)
    r = jnp.dot(lhs, wx, preferred_element_type=F32)
    y = r[0:ch] + jnp.exp2(jnp.where(left, cols[0], cols[1])) * y_off[:, ls]
    y_ref[0, :, p * LANES:(p + 1) * LANES] = y.astype(BF16)
    dec = jnp.where(left[0:1, :], e_tot[:, h1:h1 + 1], e_tot[:, h1 + 1:h1 + 2])
    s_scr[dr, g, :, ls] = s_g[:, ls] * dec + r[ch:2 * ch]


def _ssd_kernel(n_ctx, xfl_ref, xfc_ref, xbl_ref, xbc_ref, dfl_ref, dfc_ref, dbl_ref, dbc_ref, alog_ref,
                yf_ref, yb_ref, s_scr):
    i = pl.program_id(1)

    @pl.when(i == 0)
    def _():
        s_scr[...] = jnp.zeros_like(s_scr)

    inner = yf_ref.shape[-1]
    n_heads = inner // SSD_HEAD_DIM

    def chunk_pair(xf_ref, df_ref, xb_ref, db_ref):
        dirs = ((xf_ref, df_ref, yf_ref, False), (xb_ref, db_ref, yb_ref, True))
        pros = [_ssd_prologue(rev, dr, d_ref, alog_ref) for dr, (_, d_ref, _, rev) in enumerate(dirs)]
        for g in range(SSD_GROUPS):
            grps = [_ssd_group(dr, g, x_ref, s_scr, inner) for dr, (x_ref, _, _, _) in enumerate(dirs)]
            for pp in range(n_heads // SSD_GROUPS // 2):
                for dr, (x_ref, _, y_ref, _) in enumerate(dirs):
                    _ssd_pair(dr, g, pp, n_heads, pros[dr], grps[dr], x_ref, s_scr, y_ref)

    pl.when(i < n_ctx)(lambda: chunk_pair(xfc_ref, dfc_ref, xbc_ref, dbc_ref))
    pl.when(i >= n_ctx)(lambda: chunk_pair(xfl_ref, dfl_ref, xbl_ref, dbl_ref))


def _ssd(xbc_lat, dt_lat, xbc_ctx, dt_ctx, a_log):
    bsz, s, nx = xbc_lat.shape
    nd = dt_lat.shape[-1]
    n_heads = a_log.shape[-1]
    inner = n_heads * SSD_HEAD_DIM
    a_log = jnp.stack([jnp.pad(a_log[dr], (dr * n_heads, nd - (dr + 1) * n_heads)) for dr in range(2)])
    ch = SSD_CHUNK
    n_lat = s // ch
    n_ctx = xbc_ctx.shape[1] // ch
    n_all = n_lat + n_ctx
    fwd = lambda b, i: (b, i, 0)
    bwd = lambda b, i: (b, _bwd_tile(i, n_ctx, n_all), 0)
    lead = lambda b, i: (b,)
    return pl.pallas_call(
        functools.partial(_ssd_kernel, n_ctx),
        grid=(bsz, n_all),
        in_specs=_walk_specs((1, ch, nx), n_ctx, n_lat, lead) + _walk_specs((1, ch, nd), n_ctx, n_lat, lead)
        + [pl.BlockSpec(a_log.shape, lambda b, i: (0, 0))],
        out_specs=[pl.BlockSpec((1, ch, inner), fwd), pl.BlockSpec((1, ch, inner), bwd)],
        out_shape=[jax.ShapeDtypeStruct((bsz, n_all * ch, inner), BF16)] * 2,
        scratch_shapes=[pltpu.VMEM((2, SSD_GROUPS, SSD_STATE, inner // SSD_GROUPS), F32)],
        compiler_params=_cparams(("arbitrary", "arbitrary")),
        name="ssd",
    )(xbc_lat, xbc_ctx, xbc_lat, xbc_ctx, dt_lat, dt_ctx, dt_lat, dt_ctx, a_log)


def _finish1_kernel(yf_ref, yb_ref, xs_ref, gz_ref, h_ref, mod_ref, dsk_ref, nw_ref, w_ref, g_ref, o_ref):
    d = o_ref.shape[-1]
    inner = yf_ref.shape[-1]
    gw = inner // SSD_GROUPS
    gt = mod_ref[0, :, 2 * d:3 * d]
    halves = [slice(r0, r0 + TM) for r0 in range(0, o_ref.shape[1], TM)]
    outs = [None] * len(halves)
    for g in range(SSD_GROUPS):
        sl = slice(g * gw, (g + 1) * gw)
        for k, rs in enumerate(halves):
            y = (yf_ref[0, rs, sl] + yb_ref[0, rs, sl]).astype(F32) + dsk_ref[:, sl] * xs_ref[0, rs, sl].astype(F32)
            y = y * gz_ref[0, rs, sl].astype(F32)
            t = jnp.dot((_rms(y) * nw_ref[:, sl]).astype(BF16), w_ref[sl, :], preferred_element_type=F32)
            outs[k] = t if outs[k] is None else outs[k] + t
    for k, rs in enumerate(halves):
        o_ref[0, rs, :] = h_ref[0, rs, :] + gt * (_rms(outs[k]) * g_ref[...])


def _finish1(yf, yb, xbc, gz, h, mod, d_skip, norm_w, w_out, g_post):
    bsz, t_all, inner = yf.shape
    d = h.shape[-1]
    rows = FINISH_TM
    n_lat = h.shape[1] // rows
    ctx_len = t_all - h.shape[1]
    own = lambda width: pl.BlockSpec((1, rows, width), lambda j, b: (b, j, 0))
    const2 = lambda j, b: (0, 0)
    return pl.pallas_call(
        _finish1_kernel,
        grid=(n_lat, bsz),
        in_specs=[_uni_lat_spec(rows, inner, ctx_len)] * 2 + [own(inner), own(inner)] + [
            pl.BlockSpec((1, rows, d), lambda j, b: (b, j, 0)),
            pl.BlockSpec((1, 1, 3 * d), lambda j, b: (b, 0, 0)),
            pl.BlockSpec((1, inner), const2),
            pl.BlockSpec((1, inner), const2),
            pl.BlockSpec(w_out.shape, const2),
            pl.BlockSpec((1, d), const2),
        ],
        out_specs=pl.BlockSpec((1, rows, d), lambda j, b: (b, j, 0)),
        out_shape=jax.ShapeDtypeStruct((bsz, n_lat * rows, d), F32),
        compiler_params=_cparams(("arbitrary", "arbitrary")),
        name="finish1",
    )(yf, yb, xbc, gz, h, mod, d_skip, norm_w, w_out, g_post.reshape(1, d))


def _rope_tables(n_tokens):
    rows = n_tokens // GRID_W
    row = jnp.repeat(jnp.arange(rows, dtype=F32), GRID_W)
    col = jnp.tile(jnp.arange(GRID_W, dtype=F32), rows)
    n_freq = DA_HEAD_DIM // 4
    inv = ROPE_BASE ** (-jnp.arange(n_freq, dtype=F32) / n_freq)
    ang = jnp.concatenate([row[:, None] * inv, col[:, None] * inv], axis=-1)
    cos, sin = jnp.cos(ang), jnp.sin(ang)
    cr, cc, sr, sc = cos[:, :n_freq], cos[:, n_freq:], sin[:, :n_freq], sin[:, n_freq:]
    zr = jnp.zeros_like(sr)
    c64 = jnp.concatenate([cr, cr, cc, cc], axis=1)
    s1 = jnp.concatenate([-sr, zr, -sc, zr], axis=1)
    s2 = jnp.concatenate([zr, sr, zr, sc], axis=1)
    reps = LANES // DA_HEAD_DIM
    return tuple(jnp.tile(t, (1, reps)) for t in (c64, s1, s2))


def _block_diag(w):
    n, c, d = w.shape
    eye = jnp.eye(n, dtype=w.dtype)
    return (eye[:, None, :, None] * w[:, :, None, :]).reshape(n * c, n * d)


def kernel(x, c, ctx, c_ctx, w_mod, b_mod, g_pre, g_post, e_w_in, e_w_out, lru_conv_w, lru_conv_b, lru_w_r, lru_b_r, lru_w_i, lru_b_i, lru_lambda, da_lambda, da_subln, o_w_in, o_w_out, ssd_conv_w, ssd_conv_b, ssd_a_log, ssd_dt_bias, ssd_d, ssd_norm):
    bsz, s, d = x.shape
    ctx_len = ctx.shape[1]
    assert bsz == SUBLANES and ctx_len == TM and s % KV_CHUNK == 0 and w_mod.shape[0] == 2
    assert e_w_in.shape[0] == 1 and o_w_in.shape[0] == 1

    n_rows = 2 * SUBLANES
    c_rows = jnp.concatenate([c, c_ctx[None, :], jnp.zeros((n_rows - bsz - 1, d), F32)], axis=0)
    mod = _modulation(c_rows, w_mod, b_mod)
    mod0 = mod[0].reshape(n_rows, 1, 3 * d)
    mod1 = mod[1].reshape(n_rows, 1, 3 * d)

    w5 = lru_conv_w.shape[-1]
    w_in0 = e_w_in[0].astype(BF16)
    (xc, gr, q, k, v, gd), (xc_c, gr_c, q_c, k_c, v_c, gd_c) = _inproj0(
        x, ctx, mod0, g_pre[0], w_in0, lru_conv_w[0], lru_conv_b[0], _rope_tables(s))

    wg = (0.5 * jnp.stack([jnp.concatenate([_block_diag(lru_w_r[0, dr]), _block_diag(lru_w_i[0, dr])], axis=1)
                           for dr in range(2)])).astype(BF16)
    bias = 0.5 * jnp.stack([lru_b_r[0, 0], lru_b_i[0, 0], lru_b_r[0, 1], lru_b_i[0, 1]])
    k4 = (0.5 * LRU_C) * jax.nn.softplus(-lru_lambda[0])
    hf, hb = _lru(xc, xc_c, wg, bias, k4)

    lambda_init = 0.8 - 0.6 * math.exp(-0.3 * 0)
    on, on_c = _attention((q, k, v), (q_c, k_c, v_c), da_lambda[0], da_subln[0], lambda_init)
    h1_lat, h1_ctx = _outproj0(hf, hb, (gr, on, gd), (gr_c, on_c, gd_c), x, ctx, mod0, g_post[0],
                               e_w_out[0].astype(BF16))

    n_heads = ssd_a_log.shape[-1]
    inner = n_heads * SSD_HEAD_DIM
    nx = ssd_conv_w.shape[-1]
    w1 = o_w_in[0].astype(BF16)
    nd = 2 * n_heads
    wd = jnp.pad(w1[:, inner + nx:], ((0, 0), (0, LANES - nd)))
    dt_bias = jnp.pad(ssd_dt_bias[0].reshape(1, nd), ((0, 0), (0, LANES - nd)))
    (gz, xbc, dt), (_, xbc_c, dt_c) = _inproj1(h1_lat, h1_ctx, mod1, g_pre[1], w1, inner, wd, ssd_conv_w[0],
                                               ssd_conv_b[0], dt_bias)
    yf, yb = _ssd(xbc, dt, xbc_c, dt_c, ssd_a_log[0])
    d_skip = jnp.repeat(ssd_d[0], SSD_HEAD_DIM).reshape(1, inner)
    return _finish1(yf, yb, xbc, gz, h1_lat, mod1, d_skip, ssd_norm[0].reshape(1, inner),
                    o_w_out[0].astype(BF16), g_post[1])
```
